```python
import math
import jax, jax.numpy as jnp
from jax import lax
import numpy as np

D_MODEL = 1024
BATCH = 4
SEQ = 8192
DEPTH = 4

GRID_W = 64
N_MEM = 256
RMS_EPS = 1e-6
NA_HEADS = 8
NA_HEAD_DIM = 64
NA_WIDTH = NA_HEADS * NA_HEAD_DIM
NA_WIN_R = 8
NA_WIN_C = 16
S5_GROUPS = 16
S5_GROUP_CH = 16
S5_WIDTH = S5_GROUPS * S5_GROUP_CH
S5_STATE = 64
S5_DT_MIN = 1e-3
S5_DT_MAX = 1e-1
XA_HEADS = 4
XA_HEAD_DIM = 64
XA_WIDTH = XA_HEADS * XA_HEAD_DIM
N_BRANCHES = 3
IN_COLS = 3 * NA_WIDTH + S5_WIDTH + XA_WIDTH + N_BRANCHES * D_MODEL
IN_SPLITS = (NA_WIDTH, 2 * NA_WIDTH, 3 * NA_WIDTH, 3 * NA_WIDTH + S5_WIDTH,
             3 * NA_WIDTH + S5_WIDTH + XA_WIDTH)
MOE_GROUPS = 4
MOE_EXPERTS_PER_GROUP = 4
MOE_EXPERTS = MOE_GROUPS * MOE_EXPERTS_PER_GROUP
MOE_TOP_K = 2
MOE_FF = 256

kernel_name = "hybrid_na_s5_memxattn_hmoe_encoder"


def rms_norm(x, g):
    x32 = x.astype(jnp.float32)
    y = x32 * lax.rsqrt(jnp.mean(x32 * x32, axis=-1, keepdims=True) + RMS_EPS)
    return (y * g.astype(jnp.float32)).astype(x.dtype)


def neighbourhood_attention(q, k, v, rpb):
    bsz, seq, _ = q.shape
    rows = seq // GRID_W
    win_r = min(NA_WIN_R, rows)
    shape = (bsz, rows, GRID_W, NA_HEADS, NA_HEAD_DIM)
    qg = q.reshape(shape) * (NA_HEAD_DIM ** -0.5)
    kg = k.reshape(shape)
    vg = v.reshape(shape)
    cols = np.arange(GRID_W)
    col_start = np.clip(cols - NA_WIN_C // 2, 0, GRID_W - NA_WIN_C)
    col_idx = col_start[:, None] + np.arange(NA_WIN_C)[None, :]
    col_bias_idx = col_idx - cols[:, None] + (NA_WIN_C - 1)

    def one_row(args):
        q_row, r = args
        r_start = jnp.clip(r - win_r // 2, 0, rows - win_r)
        k_blk = lax.dynamic_slice_in_dim(kg, r_start, win_r, axis=1)
        v_blk = lax.dynamic_slice_in_dim(vg, r_start, win_r, axis=1)
        k_nb = k_blk[:, :, col_idx]
        v_nb = v_blk[:, :, col_idx]
        row_bias_idx = r_start + jnp.arange(win_r) - r + (NA_WIN_R - 1)
        bias = rpb[:, row_bias_idx[None, :, None], col_bias_idx[:, None, :]]
        s = jnp.einsum('bchd,bicjhd->bhcij', q_row, k_nb).astype(jnp.float32) + bias.astype(jnp.float32)
        p = jax.nn.softmax(s.reshape(bsz, NA_HEADS, GRID_W, win_r * NA_WIN_C), axis=-1)
        p = p.reshape(s.shape).astype(v.dtype)
        return jnp.einsum('bhcij,bicjhd->bchd', p, v_nb)

    out = lax.map(one_row, (jnp.moveaxis(qg, 1, 0), jnp.arange(rows)))
    return jnp.moveaxis(out, 0, 1).reshape(bsz, seq, NA_WIDTH)


def _complex_linear_combine(earlier, later):
    a1r, a1i, b1r, b1i = earlier
    a2r, a2i, b2r, b2i = later
    return (a2r * a1r - a2i * a1i,
            a2r * a1i + a2i * a1r,
            a2r * b1r - a2i * b1i + b2r,
            a2r * b1i + a2i * b1r + b2i)


def s5_bidirectional(u, lam_re, lam_im, log_dt, b_re, b_im, c_re, c_im, d_skip, w_glu):
    bsz, seq, _ = u.shape
    uf = u.astype(jnp.float32).reshape(bsz, seq, S5_GROUPS, S5_GROUP_CH)
    y = d_skip.astype(jnp.float32).reshape(S5_GROUPS, S5_GROUP_CH) * uf
    for direction in range(2):
        lr = jnp.minimum(lam_re[direction].astype(jnp.float32), -1e-4)
        li = lam_im[direction].astype(jnp.float32)
        dt = jnp.exp(log_dt[direction].astype(jnp.float32))[:, None]
        mag = jnp.exp(lr * dt)
        ar = mag * jnp.cos(li * dt)
        ai = mag * jnp.sin(li * dt)
        den = lr * lr + li * li
        fr = ((ar - 1.0) * lr + ai * li) / den
        fi = (ai * lr - (ar - 1.0) * li) / den
        br = b_re[direction].astype(jnp.float32)
        bi = b_im[direction].astype(jnp.float32)
        bbar_re = fr[..., None] * br - fi[..., None] * bi
        bbar_im = fr[..., None] * bi + fi[..., None] * br
        bu_re = jnp.einsum('bsgc,gpc->bsgp', uf, bbar_re)
        bu_im = jnp.einsum('bsgc,gpc->bsgp', uf, bbar_im)
        a_re = jnp.broadcast_to(ar, (1, seq, S5_GROUPS, S5_STATE))
        a_im = jnp.broadcast_to(ai, (1, seq, S5_GROUPS, S5_STATE))
        _, _, x_re, x_im = lax.associative_scan(
            _complex_linear_combine, (a_re, a_im, bu_re, bu_im), reverse=(direction == 1), axis=1)
        y = y + jnp.einsum('bsgp,gcp->bsgc', x_re, c_re[direction].astype(jnp.float32)) \
              - jnp.einsum('bsgp,gcp->bsgc', x_im, c_im[direction].astype(jnp.float32))
    y = jax.nn.gelu(y.reshape(bsz, seq, S5_WIDTH))
    g = y @ w_glu.astype(jnp.float32)
    out = g[..., :S5_WIDTH] * jax.nn.sigmoid(g[..., S5_WIDTH:])
    return out.astype(u.dtype)


def memory_cross_attention(q, mem_kv):
    bsz, seq, _ = q.shape
    n_mem = mem_kv.shape[1]
    qh = q.reshape(bsz, seq, XA_HEADS, XA_HEAD_DIM) * (XA_HEAD_DIM ** -0.5)
    k_m = mem_kv[..., :XA_WIDTH].reshape(bsz, n_mem, XA_HEADS, XA_HEAD_DIM)
    v_m = mem_kv[..., XA_WIDTH:].reshape(bsz, n_mem, XA_HEADS, XA_HEAD_DIM)
    s = jnp.einsum('bshd,bmhd->bhsm', qh, k_m).astype(jnp.float32)
    p = jax.nn.softmax(s, axis=-1).astype(v_m.dtype)
    return jnp.einsum('bhsm,bmhd->bshd', p, v_m).reshape(bsz, seq, XA_WIDTH)


def hierarchical_moe(h, w_coarse, b_coarse, w_fine, b_fine, w_gate, w_up, w_down):
    bsz, seq, _ = h.shape
    coarse = (h @ w_coarse).astype(jnp.float32) + b_coarse.astype(jnp.float32)
    p_coarse = jax.nn.softmax(coarse, axis=-1)
    _, grp = lax.top_k(coarse, 1)
    p_grp = jnp.take_along_axis(p_coarse, grp, axis=-1)
    fine = ((h @ w_fine).astype(jnp.float32) + b_fine.astype(jnp.float32)).reshape(
        bsz, seq, MOE_GROUPS, MOE_EXPERTS_PER_GROUP)
    fine_g = jnp.take_along_axis(fine, grp[..., None], axis=2)[..., 0, :]
    top_v, top_i = lax.top_k(fine_g, MOE_TOP_K)
    w_top = jax.nn.softmax(top_v, axis=-1) * p_grp
    expert_idx = grp * MOE_EXPERTS_PER_GROUP + top_i
    comb = jnp.sum(jax.nn.one_hot(expert_idx, MOE_EXPERTS, dtype=jnp.float32) * w_top[..., None], axis=-2)
    comb = comb.astype(h.dtype)

    def per_row(args):
        hb, cb = args
        g = jnp.einsum('sd,edf->sef', hb, w_gate)
        u = jnp.einsum('sd,edf->sef', hb, w_up)
        a = jax.nn.silu(g) * u * cb[:, :, None]
        return jnp.einsum('sef,efd->sd', a, w_down)

    return lax.map(per_row, (h, comb))


def setup_inputs(seed: int = 0) -> dict:
    key = jax.random.key(seed)
    ks = iter(jax.random.split(key, 40))
    f32 = jnp.float32

    def nrm(shape, scale):
        return jax.random.normal(next(ks), shape, f32) * scale

    def gain(shape):
        return 1.0 + 0.02 * jax.random.normal(next(ks), shape, f32)

    L, D = DEPTH, D_MODEL
    n_idx = jnp.arange(S5_STATE, dtype=f32)
    return {
        "x": nrm((BATCH, SEQ, D), 1.0),
        "mem": nrm((BATCH, N_MEM, D), 1.0),
        "norm_mix": gain((L, D)),
        "norm_ffn": gain((L, D)),
        "norm_mem": gain((L, D)),
        "w_in": nrm((L, D, IN_COLS), D ** -0.5),
        "b_gate": nrm((L, N_BRANCHES * D), 0.02),
        "na_rpb": nrm((L, NA_HEADS, 2 * NA_WIN_R - 1, 2 * NA_WIN_C - 1), 0.1),
        "s5_lam_re": -0.5 + nrm((L, 2, S5_GROUPS, S5_STATE), 0.01),
        "s5_lam_im": math.pi * n_idx + nrm((L, 2, S5_GROUPS, S5_STATE), 0.01),
        "s5_log_dt": jax.random.uniform(next(ks), (L, 2, S5_GROUPS), f32,
                                        minval=math.log(S5_DT_MIN), maxval=math.log(S5_DT_MAX)),
        "s5_b_re": nrm((L, 2, S5_GROUPS, S5_STATE, S5_GROUP_CH), S5_GROUP_CH ** -0.5),
        "s5_b_im": nrm((L, 2, S5_GROUPS, S5_STATE, S5_GROUP_CH), S5_GROUP_CH ** -0.5),
        "s5_c_re": nrm((L, 2, S5_GROUPS, S5_GROUP_CH, S5_STATE), S5_STATE ** -0.5),
        "s5_c_im": nrm((L, 2, S5_GROUPS, S5_GROUP_CH, S5_STATE), S5_STATE ** -0.5),
        "s5_d": nrm((L, S5_WIDTH), 1.0),
        "s5_w_glu": nrm((L, S5_WIDTH, 2 * S5_WIDTH), S5_WIDTH ** -0.5),
        "w_mem_kv": nrm((L, D, 2 * XA_WIDTH), D ** -0.5),
        "w_br_na": nrm((L, NA_WIDTH, D), NA_WIDTH ** -0.5),
        "w_br_s5": nrm((L, S5_WIDTH, D), S5_WIDTH ** -0.5),
        "w_br_xa": nrm((L, XA_WIDTH, D), XA_WIDTH ** -0.5),
        "w_out": nrm((L, D, D), D ** -0.5),
        "moe_w_coarse": nrm((L, D, MOE_GROUPS), D ** -0.5),
        "moe_b_coarse": nrm((L, MOE_GROUPS), 0.01),
        "moe_w_fine": nrm((L, D, MOE_EXPERTS), D ** -0.5),
        "moe_b_fine": nrm((L, MOE_EXPERTS), 0.01),
        "moe_w_gate": nrm((L, MOE_EXPERTS, D, MOE_FF), D ** -0.5),
        "moe_w_up": nrm((L, MOE_EXPERTS, D, MOE_FF), D ** -0.5),
        "moe_w_down": nrm((L, MOE_EXPERTS, MOE_FF, D), MOE_FF ** -0.5),
        "final_norm": gain((D,)),
    }


def reference(x, mem, norm_mix, norm_ffn, norm_mem, w_in, b_gate, na_rpb,
              s5_lam_re, s5_lam_im, s5_log_dt, s5_b_re, s5_b_im, s5_c_re, s5_c_im, s5_d, s5_w_glu,
              w_mem_kv, w_br_na, w_br_s5, w_br_xa, w_out,
              moe_w_coarse, moe_b_coarse, moe_w_fine, moe_b_fine, moe_w_gate, moe_w_up, moe_w_down,
              final_norm):
    bsz, seq, d = x.shape
    for l in range(DEPTH):
        h = rms_norm(x, norm_mix[l])
        proj = h @ w_in[l]
        q_na, k_na, v_na, u_s5, q_xa, gate_logits = jnp.split(proj, IN_SPLITS, axis=-1)
        o_na = neighbourhood_attention(q_na, k_na, v_na, na_rpb[l])
        o_s5 = s5_bidirectional(u_s5, s5_lam_re[l], s5_lam_im[l], s5_log_dt[l],
                                s5_b_re[l], s5_b_im[l], s5_c_re[l], s5_c_im[l], s5_d[l], s5_w_glu[l])
        mem_kv = rms_norm(mem, norm_mem[l]) @ w_mem_kv[l]
        o_xa = memory_cross_attention(q_xa, mem_kv)
        gates = jax.nn.sigmoid((gate_logits + b_gate[l]).astype(jnp.float32)).astype(x.dtype)
        gates = gates.reshape(bsz, seq, N_BRANCHES, d)
        merged = gates[:, :, 0] * (o_na @ w_br_na[l]) \
               + gates[:, :, 1] * (o_s5 @ w_br_s5[l]) \
               + gates[:, :, 2] * (o_xa @ w_br_xa[l])
        x = x + merged @ w_out[l]
        h2 = rms_norm(x, norm_ffn[l])
        x = x + hierarchical_moe(h2, moe_w_coarse[l], moe_b_coarse[l], moe_w_fine[l], moe_b_fine[l],
                                 moe_w_gate[l], moe_w_up[l], moe_w_down[l])
    return rms_norm(x, final_norm)
```

```python
import functools
import math

import numpy as np
import jax
import jax.numpy as jnp
from jax import lax
from jax.experimental import pallas as pl
from jax.experimental.pallas import tpu as pltpu

F32 = jnp.float32
BF16 = jnp.bfloat16

D_MODEL = 1024
GRID_W = 64
RMS_EPS = 1e-6
NA_HEADS = 8
NA_HEAD_DIM = 64
NA_WIDTH = NA_HEADS * NA_HEAD_DIM
NA_WIN_R = 8
NA_WIN_C = 16
NA_KV_ROWS = 24
NA_Q_ROWS = 8
S5_GROUPS = 16
S5_GROUP_CH = 16
S5_WIDTH = S5_GROUPS * S5_GROUP_CH
S5_STATE = 64
S5_CHUNK = 16
S5_BPAD = 8
XA_HEADS = 4
XA_HEAD_DIM = 64
XA_WIDTH = XA_HEADS * XA_HEAD_DIM
N_BRANCHES = 3
MOE_GROUPS = 4
MOE_EPG = 4
MOE_EXPERTS = MOE_GROUPS * MOE_EPG
MOE_FF = 256
MOE_ECHUNK = 4
ROUTER_LANES = 128
NEG_BIG = -1e30

VMEM_LIMIT = 52 * 1024 * 1024
TOKEN_TILE = 512


def _cparams(sem):
    return pltpu.CompilerParams(dimension_semantics=sem, vmem_limit_bytes=VMEM_LIMIT)


def _inproj_kernel(x_ref, g_ref, w_ref, bg_ref, q_ref, k_ref, v_ref, u_ref, qx_ref, gate_ref):
    x = x_ref[...]
    ms = jnp.mean(x * x, axis=-1, keepdims=True)
    h = (x * lax.rsqrt(ms + RMS_EPS) * g_ref[...]).astype(BF16)

    def proj(a, b):
        return jnp.dot(h, w_ref[:, a:b], preferred_element_type=F32)

    c0 = NA_WIDTH
    q_ref[...] = (proj(0, c0) * (NA_HEAD_DIM ** -0.5)).astype(BF16)
    k_ref[...] = proj(c0, 2 * c0).astype(BF16)
    v_ref[...] = proj(2 * c0, 3 * c0).astype(BF16)
    c1 = 3 * c0
    u_ref[...] = proj(c1, c1 + S5_WIDTH)
    c2 = c1 + S5_WIDTH
    qx_ref[...] = (proj(c2, c2 + XA_WIDTH) * (XA_HEAD_DIM ** -0.5)).astype(BF16)
    c3 = c2 + XA_WIDTH
    for j in range(N_BRANCHES):
        z = proj(c3 + D_MODEL * j, c3 + D_MODEL * (j + 1)) + bg_ref[:, D_MODEL * j:D_MODEL * (j + 1)]
        gate_ref[:, D_MODEL * j:D_MODEL * (j + 1)] = jax.nn.sigmoid(z).astype(BF16)


def _inproj(x, g, w_bf, bg):
    B, S, D = x.shape
    tm = min(TOKEN_TILE, S)
    ncols = w_bf.shape[1]

    def tok(width):
        return pl.BlockSpec((None, tm, width), lambda b, i: (b, i, 0))

    def whole(shape):
        return pl.BlockSpec(shape, lambda b, i: (0,) * len(shape))

    return pl.pallas_call(
        _inproj_kernel,
        grid=(B, S // tm),
        in_specs=[tok(D), whole((1, D)), whole((D, ncols)), whole((1, N_BRANCHES * D))],
        out_specs=[tok(NA_WIDTH), tok(NA_WIDTH), tok(NA_WIDTH), tok(S5_WIDTH), tok(XA_WIDTH), tok(N_BRANCHES * D)],
        out_shape=[
            jax.ShapeDtypeStruct((B, S, NA_WIDTH), BF16),
            jax.ShapeDtypeStruct((B, S, NA_WIDTH), BF16),
            jax.ShapeDtypeStruct((B, S, NA_WIDTH), BF16),
            jax.ShapeDtypeStruct((B, S, S5_WIDTH), F32),
            jax.ShapeDtypeStruct((B, S, XA_WIDTH), BF16),
            jax.ShapeDtypeStruct((B, S, N_BRANCHES * D), BF16),
        ],
        compiler_params=_cparams(("parallel", "parallel")),
        name="inproj",
    )(x, g.reshape(1, D), w_bf, bg.reshape(1, N_BRANCHES * D))


def _na_bias_table(rpb):
    cols = np.arange(GRID_W)
    col_start = np.clip(cols - NA_WIN_C // 2, 0, GRID_W - NA_WIN_C)
    cj = np.arange(GRID_W)[None, :]
    valid = (cj >= col_start[:, None]) & (cj < col_start[:, None] + NA_WIN_C)
    col_idx = np.clip(cj - cols[:, None] + (NA_WIN_C - 1), 0, 2 * NA_WIN_C - 2)
    d = np.arange(NA_WIN_R)[:, None]
    ri = np.arange(NA_WIN_R)[None, :]
    row_idx = ri - d + (NA_WIN_R - 1)
    t = rpb.astype(F32)[:, row_idx[:, :, None, None], col_idx[None, None, :, :]]
    t = jnp.where(valid[None, None, None], t, NEG_BIG)
    t = jnp.transpose(t, (1, 0, 3, 2, 4))
    return t.reshape(NA_WIN_R, NA_HEADS, GRID_W, NA_WIN_R * GRID_W)


def _na_kernel(q_ref, k_ref, v_ref, bias_ref, o_ref, *, rows_total):
    r0 = pl.program_id(1) * NA_Q_ROWS
    kstart = jnp.clip(r0 - NA_Q_ROWS, 0, rows_total - NA_KV_ROWS)
    lane = lax.broadcasted_iota(jnp.int32, (GRID_W, 128), 1)
    first_head = lane < NA_HEAD_DIM
    win = NA_WIN_R * GRID_W

    def body(i, carry):
        r = r0 + i
        rs = jnp.clip(r - NA_WIN_R // 2, 0, rows_total - NA_WIN_R)
        koff = pl.multiple_of((rs - kstart) * GRID_W, GRID_W)
        didx = r - rs
        qoff = pl.multiple_of(i * GRID_W, GRID_W)
        for p in range(NA_HEADS // 2):
            cs = slice(128 * p, 128 * (p + 1))
            q2 = q_ref[pl.ds(qoff, GRID_W), cs]
            kw = k_ref[pl.ds(koff, win), cs]
            vw = v_ref[pl.ds(koff, win), cs]
            outs = []
            for hh in range(2):
                qm = jnp.where(first_head if hh == 0 else jnp.logical_not(first_head), q2, jnp.zeros_like(q2))
                s = lax.dot_general(qm, kw, (((1,), (1,)), ((), ())), preferred_element_type=F32)
                s = s + bias_ref[didx, 2 * p + hh]
                m = jnp.max(s, axis=-1, keepdims=True)
                e = jnp.exp(s - m)
                l = jnp.sum(e, axis=-1, keepdims=True)
                outs.append(jnp.dot(e.astype(BF16), vw, preferred_element_type=F32) / l)
            o_ref[pl.ds(qoff, GRID_W), cs] = jnp.where(first_head, outs[0], outs[1]).astype(BF16)
        return carry

    lax.fori_loop(0, NA_Q_ROWS, body, 0)


def _na(q, k, v, bias):
    B, S, _ = q.shape
    rows = S // GRID_W
    assert rows >= NA_KV_ROWS and rows % NA_Q_ROWS == 0
    qtok = NA_Q_ROWS * GRID_W
    kvtok = NA_KV_ROWS * GRID_W

    def kv_map(b, rb):
        return (b, jnp.clip(rb * NA_Q_ROWS - NA_Q_ROWS, 0, rows - NA_KV_ROWS) * GRID_W, 0)

    kv_spec = pl.BlockSpec((None, pl.Element(kvtok), pl.Element(NA_WIDTH)), kv_map)
    return pl.pallas_call(
        functools.partial(_na_kernel, rows_total=rows),
        grid=(B, rows // NA_Q_ROWS),
        in_specs=[
            pl.BlockSpec((None, qtok, NA_WIDTH), lambda b, rb: (b, rb, 0)),
            kv_spec,
            kv_spec,
            pl.BlockSpec(bias.shape, lambda b, rb: (0, 0, 0, 0)),
        ],
        out_specs=pl.BlockSpec((None, qtok, NA_WIDTH), lambda b, rb: (b, rb, 0)),
        out_shape=jax.ShapeDtypeStruct((B, S, NA_WIDTH), BF16),
        compiler_params=_cparams(("parallel", "parallel")),
        name="na",
    )(q, k, v, bias)


def _s5_tables(lam_re, lam_im, log_dt, b_re, b_im, c_re, c_im):
    L = S5_CHUNK
    lr = jnp.minimum(lam_re.astype(F32), -1e-4)
    li = lam_im.astype(F32)
    dt = jnp.exp(log_dt.astype(F32))[:, :, None]
    mag = jnp.exp(lr * dt)
    ar = mag * jnp.cos(li * dt)
    ai = mag * jnp.sin(li * dt)
    den = lr * lr + li * li
    fr = ((ar - 1.0) * lr + ai * li) / den
    fi = (ai * lr - (ar - 1.0) * li) / den
    br = b_re.astype(F32)
    bi = b_im.astype(F32)
    bbr = fr[..., None] * br - fi[..., None] * bi
    bbi = fr[..., None] * bi + fi[..., None] * br
    cr = c_re.astype(F32)
    ci = c_im.astype(F32)
    j = jnp.arange(L + 1, dtype=F32)[:, None, None, None]
    pm = jnp.exp(j * (lr * dt)[None])
    pr = pm * jnp.cos(j * (li * dt)[None])
    pi_ = pm * jnp.sin(j * (li * dt)[None])
    abr = pr[..., None] * bbr[None] - pi_[..., None] * bbi[None]
    abi = pr[..., None] * bbi[None] + pi_[..., None] * bbr[None]
    kern = jnp.einsum('dgcp,jdgpe->jdgce', cr, abr) - jnp.einsum('dgcp,jdgpe->jdgce', ci, abi)
    s_idx = np.arange(L)[:, None]
    t_idx = np.arange(L)[None, :]
    lag_f = np.clip(t_idx - s_idx, 0, L)
    lag_b = np.clip(s_idx - t_idx, 0, L)
    kf = jnp.where((t_idx >= s_idx)[:, :, None, None, None], kern[lag_f, 0], 0.0)
    kb = jnp.where((s_idx >= t_idx)[:, :, None, None, None], kern[lag_b, 1], 0.0)
    tmat = jnp.transpose(kf + kb, (2, 0, 4, 1, 3)).reshape(S5_GROUPS, L * S5_GROUP_CH, L * S5_GROUP_CH)
    ef_r = abr[L - 1 - np.arange(L), 0]
    ef_i = abi[L - 1 - np.arange(L), 0]
    eb_r = abr[np.arange(L), 1]
    eb_i = abi[np.arange(L), 1]

    def to_rows(a):
        return jnp.transpose(a, (1, 0, 3, 2)).reshape(S5_GROUPS, L * S5_GROUP_CH, S5_STATE)

    we = jnp.concatenate([to_rows(ef_r), to_rows(eb_r), to_rows(ef_i), to_rows(eb_i)], axis=-1)
    def readout(d, powers):
        p_r = pr[powers, d]
        p_i = pi_[powers, d]
        m_re = cr[d][None] * p_r[:, :, None, :] - ci[d][None] * p_i[:, :, None, :]
        m_im = cr[d][None] * p_i[:, :, None, :] + ci[d][None] * p_r[:, :, None, :]
        to_cols = lambda a: jnp.transpose(a, (1, 3, 0, 2)).reshape(S5_GROUPS, S5_STATE, L * S5_GROUP_CH)
        return to_cols(m_re), to_cols(-m_im)

    yf_re, yf_im = readout(0, np.arange(L) + 1)
    yb_re, yb_im = readout(1, L - np.arange(L))
    wy = jnp.concatenate([yf_re, yb_re, yf_im, yb_im], axis=1)
    al = jnp.stack([jnp.concatenate([pr[L, 0], pr[L, 1]], axis=-1),
                    jnp.concatenate([pi_[L, 0], pi_[L, 1]], axis=-1)], axis=1)
    return tmat.astype(BF16), we.astype(BF16), wy.astype(BF16), al


def _s5_kernel(u_ref, t_ref, we_ref, wy_ref, al_ref, y_ref, e_ref, xp_ref, *, n_chunks):
    u = u_ref[...]
    y_ref[...] = jnp.dot(u, t_ref[...], preferred_element_type=F32)
    e_ref[...] = jnp.dot(u, we_ref[...], preferred_element_type=F32)
    a_re = al_ref[0:1, :]
    a_im = al_ref[1:2, :]
    half = 2 * S5_STATE
    lane = lax.broadcasted_iota(jnp.int32, (S5_BPAD, half), 1)
    is_fwd = lane < S5_STATE

    def body(j, carry):
        xr, xi = carry
        rf = pl.multiple_of(j * S5_BPAD, S5_BPAD)
        rb = pl.multiple_of((n_chunks - 1 - j) * S5_BPAD, S5_BPAD)
        xp_ref[pl.ds(rf, S5_BPAD), 0:S5_STATE] = xr[:, 0:S5_STATE]
        xp_ref[pl.ds(rb, S5_BPAD), S5_STATE:half] = xr[:, S5_STATE:half]
        xp_ref[pl.ds(rf, S5_BPAD), half:half + S5_STATE] = xi[:, 0:S5_STATE]
        xp_ref[pl.ds(rb, S5_BPAD), half + S5_STATE:2 * half] = xi[:, S5_STATE:half]
        er = jnp.where(is_fwd, e_ref[pl.ds(rf, S5_BPAD), 0:half], e_ref[pl.ds(rb, S5_BPAD), 0:half])
        ei = jnp.where(is_fwd, e_ref[pl.ds(rf, S5_BPAD), half:2 * half], e_ref[pl.ds(rb, S5_BPAD), half:2 * half])
        nxr = a_re * xr - a_im * xi + er
        nxi = a_re * xi + a_im * xr + ei
        return nxr, nxi

    zero = jnp.zeros((S5_BPAD, half), F32)
    lax.fori_loop(0, n_chunks, body, (zero, zero))
    y_ref[...] += jnp.dot(xp_ref[...].astype(BF16), wy_ref[...], preferred_element_type=F32)


def _s5(u, tabs):
    tmat, we, wy, al = tabs
    B, S, _ = u.shape
    L = S5_CHUNK
    assert S % L == 0 and B <= S5_BPAD
    n = S // L
    rows = n * S5_BPAD
    lc = L * S5_GROUP_CH
    ug = u.astype(BF16).reshape(B, n, L, S5_GROUPS, S5_GROUP_CH)
    ug = jnp.transpose(ug, (3, 1, 0, 2, 4))
    ug = jnp.pad(ug, ((0, 0), (0, 0), (0, S5_BPAD - B), (0, 0), (0, 0))).reshape(S5_GROUPS, rows, lc)
    grp = lambda shape: pl.BlockSpec((None,) + shape, lambda g: (g, 0, 0))
    yg = pl.pallas_call(
        functools.partial(_s5_kernel, n_chunks=n),
        grid=(S5_GROUPS,),
        in_specs=[grp((rows, lc)), grp((lc, lc)), grp((lc, 4 * S5_STATE)), grp((4 * S5_STATE, lc)),
                  grp((2, 2 * S5_STATE))],
        out_specs=grp((rows, lc)),
        out_shape=jax.ShapeDtypeStruct((S5_GROUPS, rows, lc), F32),
        scratch_shapes=[pltpu.VMEM((rows, 4 * S5_STATE), F32), pltpu.VMEM((rows, 4 * S5_STATE), F32)],
        compiler_params=_cparams(("parallel",)),
        name="s5",
    )(ug, tmat, we, wy, al)
    yg = yg.reshape(S5_GROUPS, n, S5_BPAD, L, S5_GROUP_CH)[:, :, :B]
    return jnp.transpose(yg, (2, 1, 3, 0, 4)).reshape(B, S, S5_WIDTH)


def _memkv_kernel(m_ref, g_ref, w_ref, o_ref):
    x = m_ref[...]
    ms = jnp.mean(x * x, axis=-1, keepdims=True)
    h = (x * lax.rsqrt(ms + RMS_EPS) * g_ref[...]).astype(BF16)
    o_ref[...] = jnp.dot(h, w_ref[...], preferred_element_type=F32).astype(BF16)


def _memkv(mem, norm_mem, w_bf):
    B, M, D = mem.shape
    L = w_bf.shape[0]
    return pl.pallas_call(
        _memkv_kernel,
        grid=(L, B),
        in_specs=[pl.BlockSpec((None, M, D), lambda l, b: (b, 0, 0)),
                  pl.BlockSpec((None, 1, D), lambda l, b: (l, 0, 0)),
                  pl.BlockSpec((None, D, 2 * XA_WIDTH), lambda l, b: (l, 0, 0))],
        out_specs=pl.BlockSpec((None, None, M, 2 * XA_WIDTH), lambda l, b: (l, b, 0, 0)),
        out_shape=jax.ShapeDtypeStruct((L, B, M, 2 * XA_WIDTH), BF16),
        compiler_params=_cparams(("parallel", "parallel")),
        name="memkv",
    )(mem, norm_mem.reshape(L, 1, D), w_bf)


def _gelu_tanh(x):
    return 0.5 * x * (1.0 + jnp.tanh(math.sqrt(2.0 / math.pi) * (x + 0.044715 * (x * x * x))))


def _router(z):
    lane = lax.broadcasted_iota(jnp.int32, z.shape, 1).astype(F32)
    ninf = jnp.full_like(z, -jnp.inf)
    far = jnp.full_like(z, 1e9)
    cm = (lane >= MOE_EXPERTS) & (lane < MOE_EXPERTS + MOE_GROUPS)
    cmax = jnp.max(jnp.where(cm, z, ninf), axis=-1, keepdims=True)
    glane = jnp.min(jnp.where(cm & (z == cmax), lane, far), axis=-1, keepdims=True)
    psum = jnp.sum(jnp.where(cm, jnp.exp(jnp.where(cm, z, cmax) - cmax), 0.0), axis=-1, keepdims=True)
    p_grp = 1.0 / psum
    f0 = (glane - MOE_EXPERTS) * MOE_EPG
    fm = (lane >= f0) & (lane < f0 + MOE_EPG)
    v1 = jnp.max(jnp.where(fm, z, ninf), axis=-1, keepdims=True)
    i1 = jnp.min(jnp.where(fm & (z == v1), lane, far), axis=-1, keepdims=True)
    fm2 = fm & (lane != i1)
    v2 = jnp.max(jnp.where(fm2, z, ninf), axis=-1, keepdims=True)
    i2 = jnp.min(jnp.where(fm2 & (z == v2), lane, far), axis=-1, keepdims=True)
    t = jnp.exp(v2 - v1)
    w1 = p_grp / (1.0 + t)
    w2 = p_grp * t / (1.0 + t)
    return jnp.where(lane == i1, w1, 0.0) + jnp.where(lane == i2, w2, 0.0)


def _merge_kernel(x_ref, ona_ref, ys5_ref, u_ref, qx_ref, gate_ref, mkv_ref,
                  d_ref, wglu_ref, wna_ref, ws5_ref, wxa_ref, wout_ref, nffn_ref, wrh_ref, wrl_ref, br_ref,
                  xo_ref, h2_ref, comb_ref, mrg_ref):
    tm = x_ref.shape[0]
    y = _gelu_tanh(ys5_ref[...] + d_ref[...] * u_ref[...])
    g = jnp.dot(y.astype(BF16), wglu_ref[...], preferred_element_type=F32)
    os5 = (g[:, :S5_WIDTH] * jax.nn.sigmoid(g[:, S5_WIDTH:])).astype(BF16)
    lane = lax.broadcasted_iota(jnp.int32, (tm, 128), 1)
    first_head = lane < XA_HEAD_DIM
    oxa = []
    for p in range(XA_HEADS // 2):
        cs = slice(128 * p, 128 * (p + 1))
        q2 = qx_ref[:, cs]
        k2 = mkv_ref[:, cs]
        v2 = mkv_ref[:, XA_WIDTH + 128 * p:XA_WIDTH + 128 * (p + 1)]
        outs = []
        for hh in range(2):
            qm = jnp.where(first_head if hh == 0 else jnp.logical_not(first_head), q2, jnp.zeros_like(q2))
            s = lax.dot_general(qm, k2, (((1,), (1,)), ((), ())), preferred_element_type=F32)
            m = jnp.max(s, axis=-1, keepdims=True)
            e = jnp.exp(s - m)
            l = jnp.sum(e, axis=-1, keepdims=True)
            outs.append(jnp.dot(e.astype(BF16), v2, preferred_element_type=F32) / l)
        oxa.append(jnp.where(first_head, outs[0], outs[1]).astype(BF16))
    oxa = jnp.concatenate(oxa, axis=-1)
    ona = ona_ref[...]
    cw = 256
    for c in range(D_MODEL // cw):
        cs = slice(cw * c, cw * (c + 1))
        m = gate_ref[:, cw * c:cw * (c + 1)].astype(F32) * jnp.dot(ona, wna_ref[:, cs], preferred_element_type=F32)
        m += gate_ref[:, D_MODEL + cw * c:D_MODEL + cw * (c + 1)].astype(F32) * jnp.dot(
            os5, ws5_ref[:, cs], preferred_element_type=F32)
        m += gate_ref[:, 2 * D_MODEL + cw * c:2 * D_MODEL + cw * (c + 1)].astype(F32) * jnp.dot(
            oxa, wxa_ref[:, cs], preferred_element_type=F32)
        mrg_ref[:, cs] = m.astype(BF16)
    xn = x_ref[...] + jnp.dot(mrg_ref[...], wout_ref[...], preferred_element_type=F32)
    xo_ref[...] = xn
    ms = jnp.mean(xn * xn, axis=-1, keepdims=True)
    h2 = xn * lax.rsqrt(ms + RMS_EPS) * nffn_ref[...]
    hi = h2.astype(BF16)
    lo = (h2 - hi.astype(F32)).astype(BF16)
    h2_ref[...] = hi
    z = (jnp.dot(hi, wrh_ref[...], preferred_element_type=F32)
         + jnp.dot(lo, wrh_ref[...], preferred_element_type=F32)
         + jnp.dot(hi, wrl_ref[...], preferred_element_type=F32)) + br_ref[...]
    comb_ref[...] = _router(z)


def _merge(x, ona, ys5, u, qx, gates, mkv, d, wglu, wna, ws5, wxa, wout, nffn, wrh, wrl, br):
    B, S, D = x.shape
    tm = min(TOKEN_TILE, S)
    M = mkv.shape[1]

    def tok(width):
        return pl.BlockSpec((None, tm, width), lambda b, i: (b, i, 0))

    def whole(a):
        return pl.BlockSpec(a.shape, lambda b, i: (0,) * a.ndim)

    d2 = d.reshape(1, S5_WIDTH)
    nffn2 = nffn.reshape(1, D)
    return pl.pallas_call(
        _merge_kernel,
        grid=(B, S // tm),
        in_specs=[tok(D), tok(NA_WIDTH), tok(S5_WIDTH), tok(S5_WIDTH), tok(XA_WIDTH), tok(N_BRANCHES * D),
                  pl.BlockSpec((None, M, 2 * XA_WIDTH), lambda b, i: (b, 0, 0)),
                  whole(d2), whole(wglu), whole(wna), whole(ws5), whole(wxa), whole(wout), whole(nffn2),
                  whole(wrh), whole(wrl), whole(br)],
        out_specs=[tok(D), tok(D), tok(ROUTER_LANES)],
        out_shape=[jax.ShapeDtypeStruct((B, S, D), F32),
                   jax.ShapeDtypeStruct((B, S, D), BF16),
                   jax.ShapeDtypeStruct((B, S, ROUTER_LANES), F32)],
        scratch_shapes=[pltpu.VMEM((tm, D), BF16)],
        compiler_params=_cparams(("parallel", "parallel")),
        name="merge",
    )(x, ona, ys5, u, qx, gates, mkv, d2, wglu, wna, ws5, wxa, wout, nffn2, wrh, wrl, br)


def _moe_kernel(x_ref, h_ref, comb_ref, wg_ref, wu_ref, wd_ref, fn_ref, o_ref, acc_ref, *, final_norm):
    j = pl.program_id(2)
    h = h_ref[...]
    comb = comb_ref[...]
    lane = lax.broadcasted_iota(jnp.int32, comb.shape, 1)
    acts = []
    for ee in range(MOE_ECHUNK):
        c = jnp.sum(jnp.where(lane == j * MOE_ECHUNK + ee, comb, 0.0), axis=-1, keepdims=True)
        g = jnp.dot(h, wg_ref[ee], preferred_element_type=F32)
        u = jnp.dot(h, wu_ref[ee], preferred_element_type=F32)
        acts.append((g * jax.nn.sigmoid(g) * u * c).astype(BF16))
    a = jnp.concatenate(acts, axis=-1)
    part = jnp.dot(a, wd_ref[...].reshape(MOE_ECHUNK * MOE_FF, D_MODEL), preferred_element_type=F32)

    @pl.when(j == 0)
    def _():
        acc_ref[...] = x_ref[...] + part

    @pl.when(j > 0)
    def _():
        acc_ref[...] += part

    @pl.when(j == pl.num_programs(2) - 1)
    def _():
        xn = acc_ref[...]
        if final_norm:
            ms = jnp.mean(xn * xn, axis=-1, keepdims=True)
            xn = xn * lax.rsqrt(ms + RMS_EPS) * fn_ref[...]
        o_ref[...] = xn


def _moe(x, h2, comb, wg, wu, wd, fnorm, final_norm):
    B, S, D = x.shape
    tm = min(TOKEN_TILE, S)

    def tok(width):
        return pl.BlockSpec((None, tm, width), lambda b, i, j: (b, i, 0))

    return pl.pallas_call(
        functools.partial(_moe_kernel, final_norm=final_norm),
        grid=(B, S // tm, MOE_EXPERTS // MOE_ECHUNK),
        in_specs=[tok(D), tok(D), tok(ROUTER_LANES),
                  pl.BlockSpec((MOE_ECHUNK, D, MOE_FF), lambda b, i, j: (j, 0, 0)),
                  pl.BlockSpec((MOE_ECHUNK, D, MOE_FF), lambda b, i, j: (j, 0, 0)),
                  pl.BlockSpec((MOE_ECHUNK, MOE_FF, D), lambda b, i, j: (j, 0, 0)),
                  pl.BlockSpec((1, D), lambda b, i, j: (0, 0))],
        out_specs=tok(D),
        out_shape=jax.ShapeDtypeStruct((B, S, D), F32),
        scratch_shapes=[pltpu.VMEM((tm, D), F32)],
        compiler_params=_cparams(("parallel", "parallel", "arbitrary")),
        name="moe",
    )(x, h2, comb, wg, wu, wd, fnorm.reshape(1, D))


def kernel(x, mem, norm_mix, norm_ffn, norm_mem, w_in, b_gate, na_rpb, s5_lam_re, s5_lam_im, s5_log_dt, s5_b_re, s5_b_im, s5_c_re, s5_c_im, s5_d, s5_w_glu, w_mem_kv, w_br_na, w_br_s5, w_br_xa, w_out, moe_w_coarse, moe_b_coarse, moe_w_fine, moe_b_fine, moe_w_gate, moe_w_up, moe_w_down, final_norm):
    depth = w_in.shape[0]
    mkv_all = _memkv(mem, norm_mem, w_mem_kv.astype(BF16))
    for l in range(depth):
        q, k, v, u, qx, gates = _inproj(x, norm_mix[l], w_in[l].astype(BF16), b_gate[l])
        ona = _na(q, k, v, _na_bias_table(na_rpb[l]))
        ys5 = _s5(u, _s5_tables(s5_lam_re[l], s5_lam_im[l], s5_log_dt[l], s5_b_re[l], s5_b_im[l],
                                s5_c_re[l], s5_c_im[l]))
        wr = jnp.concatenate([moe_w_fine[l], moe_w_coarse[l]], axis=1).astype(F32)
        wr = jnp.pad(wr, ((0, 0), (0, ROUTER_LANES - wr.shape[1])))
        wrh = wr.astype(BF16)
        wrl = (wr - wrh.astype(F32)).astype(BF16)
        br = jnp.pad(jnp.concatenate([moe_b_fine[l], moe_b_coarse[l]]).astype(F32),
                     (0, ROUTER_LANES - MOE_EXPERTS - MOE_GROUPS)).reshape(1, ROUTER_LANES)
        x, h2, comb = _merge(x, ona, ys5, u, qx, gates, mkv_all[l], s5_d[l], s5_w_glu[l].astype(BF16),
                             w_br_na[l].astype(BF16), w_br_s5[l].astype(BF16), w_br_xa[l].astype(BF16),
                             w_out[l].astype(BF16), norm_ffn[l], wrh, wrl, br)
        x = _moe(x, h2, comb, moe_w_gate[l].astype(BF16), moe_w_up[l].astype(BF16),
                 moe_w_down[l].astype(BF16), final_norm, final_norm=(l == depth - 1))
    return x
```

```python
import functools
import math

import numpy as np
import jax
import jax.numpy as jnp
from jax import lax
from jax.experimental import pallas as pl
from jax.experimental.pallas import tpu as pltpu

F32 = jnp.float32
BF16 = jnp.bfloat16

D_MODEL = 1024
GRID_W = 64
RMS_EPS = 1e-6
NA_HEADS = 8
NA_HEAD_DIM = 64
NA_WIDTH = NA_HEADS * NA_HEAD_DIM
NA_WIN_R = 8
NA_WIN_C = 16
NA_KV_ROWS = 24
NA_Q_ROWS = 8
S5_GROUPS = 16
S5_GROUP_CH = 16
S5_WIDTH = S5_GROUPS * S5_GROUP_CH
S5_STATE = 64
S5_CHUNK = 16
S5_BPAD = 8
XA_HEADS = 4
XA_HEAD_DIM = 64
XA_WIDTH = XA_HEADS * XA_HEAD_DIM
N_BRANCHES = 3
MOE_GROUPS = 4
MOE_EPG = 4
MOE_EXPERTS = MOE_GROUPS * MOE_EPG
MOE_FF = 256
MOE_ECHUNK = 4
ROUTER_LANES = 128
NEG_BIG = -1e30

VMEM_LIMIT = 52 * 1024 * 1024
TOKEN_TILE = 512


def _cparams(sem):
    return pltpu.CompilerParams(dimension_semantics=sem, vmem_limit_bytes=VMEM_LIMIT)


def _lane_block_mask(rows, j):
    lane = lax.broadcasted_iota(jnp.int32, (rows, 128), 1)
    lo = S5_GROUP_CH * (j % 8)
    return (lane >= lo) & (lane < lo + S5_GROUP_CH)


def _time_to_group_major(src_ref, rows):
    dest = [[None, None] for _ in range(S5_GROUPS)]
    for s in range(S5_CHUNK):
        halves = tuple(src_ref[hf, pl.ds(s, rows, stride=S5_CHUNK), :] for hf in range(2))
        m = _lane_block_mask(rows, s)
        for g in range(S5_GROUPS):
            shift = (S5_GROUP_CH * ((s % 8) - (g % 8))) % 128
            r = pltpu.roll(halves[g // 8], shift, axis=1) if shift else halves[g // 8]
            prev = dest[g][s // 8]
            dest[g][s // 8] = jnp.where(m, r, 0.0 if prev is None else prev)
    return [jnp.concatenate(d, axis=1) for d in dest]


def _group_to_time_major(src_ref, dst_ref, rows):
    for t in range(S5_CHUNK):
        out = [None, None]
        for g in range(S5_GROUPS):
            src = src_ref[g, :, 128 * (t // 8):128 * (t // 8 + 1)]
            shift = (S5_GROUP_CH * ((g % 8) - (t % 8))) % 128
            r = pltpu.roll(src, shift, axis=1) if shift else src
            prev = out[g // 8]
            out[g // 8] = jnp.where(_lane_block_mask(rows, g), r, 0.0 if prev is None else prev)
        for hf in range(2):
            dst_ref[hf, pl.ds(t, rows, stride=S5_CHUNK), :] = out[hf]


def _inproj_kernel(x_ref, g_ref, w_ref, bg_ref, q_ref, k_ref, v_ref, u_ref, ug_ref, qx_ref, gate_ref, uh_ref):
    x = x_ref[...]
    ms = jnp.mean(x * x, axis=-1, keepdims=True)
    h = (x * lax.rsqrt(ms + RMS_EPS) * g_ref[...]).astype(BF16)

    def proj(a, b):
        return jnp.dot(h, w_ref[:, a:b], preferred_element_type=F32)

    c0 = NA_WIDTH
    q_ref[...] = (proj(0, c0) * (NA_HEAD_DIM ** -0.5)).astype(BF16)
    k_ref[...] = proj(c0, 2 * c0).astype(BF16)
    v_ref[...] = proj(2 * c0, 3 * c0).astype(BF16)
    c1 = 3 * c0
    u = proj(c1, c1 + S5_WIDTH)
    u_ref[...] = u
    uh_ref[0] = u[:, :128]
    uh_ref[1] = u[:, 128:]
    for g, ug in enumerate(_time_to_group_major(uh_ref, ug_ref.shape[1])):
        ug_ref[g] = ug.astype(BF16)
    c2 = c1 + S5_WIDTH
    qx_ref[...] = (proj(c2, c2 + XA_WIDTH) * (XA_HEAD_DIM ** -0.5)).astype(BF16)
    c3 = c2 + XA_WIDTH
    for j in range(N_BRANCHES):
        z = proj(c3 + D_MODEL * j, c3 + D_MODEL * (j + 1)) + bg_ref[:, D_MODEL * j:D_MODEL * (j + 1)]
        gate_ref[:, D_MODEL * j:D_MODEL * (j + 1)] = jax.nn.sigmoid(z).astype(BF16)


def _inproj(x, g, w_bf, bg):
    B, S, D = x.shape
    tm = min(TOKEN_TILE, S)
    ncols = w_bf.shape[1]

    def tok(width):
        return pl.BlockSpec((None, tm, width), lambda b, i: (b, i, 0))

    def whole(shape):
        return pl.BlockSpec(shape, lambda b, i: (0,) * len(shape))

    return pl.pallas_call(
        _inproj_kernel,
        grid=(B, S // tm),
        in_specs=[tok(D), whole((1, D)), whole((D, ncols)), whole((1, N_BRANCHES * D))],
        out_specs=[tok(NA_WIDTH), tok(NA_WIDTH), tok(NA_WIDTH), tok(S5_WIDTH),
                   pl.BlockSpec((S5_GROUPS, None, tm // S5_CHUNK, S5_CHUNK * S5_GROUP_CH), lambda b, i: (0, b, i, 0)),
                   tok(XA_WIDTH), tok(N_BRANCHES * D)],
        out_shape=[
            jax.ShapeDtypeStruct((B, S, NA_WIDTH), BF16),
            jax.ShapeDtypeStruct((B, S, NA_WIDTH), BF16),
            jax.ShapeDtypeStruct((B, S, NA_WIDTH), BF16),
            jax.ShapeDtypeStruct((B, S, S5_WIDTH), F32),
            jax.ShapeDtypeStruct((S5_GROUPS, B, S // S5_CHUNK, S5_CHUNK * S5_GROUP_CH), BF16),
            jax.ShapeDtypeStruct((B, S, XA_WIDTH), BF16),
            jax.ShapeDtypeStruct((B, S, N_BRANCHES * D), BF16),
        ],
        scratch_shapes=[pltpu.VMEM((2, tm, 128), F32)],
        compiler_params=_cparams(("parallel", "parallel")),
        name="inproj",
    )(x, g.reshape(1, D), w_bf, bg.reshape(1, N_BRANCHES * D))


def _na_bias_table(rpb):
    cols = np.arange(GRID_W)
    col_start = np.clip(cols - NA_WIN_C // 2, 0, GRID_W - NA_WIN_C)
    cj = np.arange(GRID_W)[None, :]
    valid = (cj >= col_start[:, None]) & (cj < col_start[:, None] + NA_WIN_C)
    col_idx = np.clip(cj - cols[:, None] + (NA_WIN_C - 1), 0, 2 * NA_WIN_C - 2)
    d = np.arange(NA_WIN_R)[:, None]
    ri = np.arange(NA_WIN_R)[None, :]
    row_idx = ri - d + (NA_WIN_R - 1)
    row_sel = (row_idx[:, :, None] == np.arange(2 * NA_WIN_R - 1)).astype(np.float32)
    col_sel = ((col_idx[:, :, None] == np.arange(2 * NA_WIN_C - 1)) & valid[:, :, None]).astype(np.float32)
    t = jnp.einsum('hrk,cjk->hrcj', rpb.astype(F32), col_sel, precision=lax.Precision.HIGHEST)
    t = jnp.einsum('hrcj,dir->dhcij', t, row_sel, precision=lax.Precision.HIGHEST)
    t = t + jnp.where(valid, 0.0, NEG_BIG)[None, None, :, None, :].astype(F32)
    return t.reshape(NA_WIN_R, NA_HEADS, GRID_W, NA_WIN_R * GRID_W)


def _na_kernel(q_ref, k_ref, v_ref, bias_ref, o_ref, *, rows_total):
    r0 = pl.program_id(1) * NA_Q_ROWS
    kstart = jnp.clip(r0 - NA_Q_ROWS, 0, rows_total - NA_KV_ROWS)
    lane = lax.broadcasted_iota(jnp.int32, (GRID_W, 128), 1)
    first_head = lane < NA_HEAD_DIM
    win = NA_WIN_R * GRID_W

    def body(i, carry):
        r = r0 + i
        rs = jnp.clip(r - NA_WIN_R // 2, 0, rows_total - NA_WIN_R)
        koff = pl.multiple_of((rs - kstart) * GRID_W, GRID_W)
        didx = r - rs
        qoff = pl.multiple_of(i * GRID_W, GRID_W)
        scores = []
        for h in range(NA_HEADS):
            cs = slice(128 * (h // 2), 128 * (h // 2 + 1))
            q2 = q_ref[pl.ds(qoff, GRID_W), cs]
            qm = jnp.where(first_head if h % 2 == 0 else jnp.logical_not(first_head), q2, jnp.zeros_like(q2))
            s = lax.dot_general(qm, k_ref[pl.ds(koff, win), cs], (((1,), (1,)), ((), ())),
                                preferred_element_type=F32)
            scores.append(s + bias_ref[didx, h])
        probs = []
        for s in scores:
            m = jnp.max(s, axis=-1, keepdims=True)
            e = jnp.exp(s - m)
            probs.append((e.astype(BF16), jnp.sum(e, axis=-1, keepdims=True)))
        outs = []
        for h, (e, l) in enumerate(probs):
            cs = slice(128 * (h // 2), 128 * (h // 2 + 1))
            outs.append(jnp.dot(e, v_ref[pl.ds(koff, win), cs], preferred_element_type=F32) / l)
        for p in range(NA_HEADS // 2):
            o_ref[pl.ds(qoff, GRID_W), 128 * p:128 * (p + 1)] = jnp.where(
                first_head, outs[2 * p], outs[2 * p + 1]).astype(BF16)
        return carry

    lax.fori_loop(0, NA_Q_ROWS, body, 0)


def _na(q, k, v, bias):
    B, S, _ = q.shape
    rows = S // GRID_W
    assert rows >= NA_KV_ROWS and rows % NA_Q_ROWS == 0
    qtok = NA_Q_ROWS * GRID_W
    kvtok = NA_KV_ROWS * GRID_W

    def kv_map(b, rb):
        return (b, jnp.clip(rb * NA_Q_ROWS - NA_Q_ROWS, 0, rows - NA_KV_ROWS) * GRID_W, 0)

    kv_spec = pl.BlockSpec((None, pl.Element(kvtok), pl.Element(NA_WIDTH)), kv_map)
    return pl.pallas_call(
        functools.partial(_na_kernel, rows_total=rows),
        grid=(B, rows // NA_Q_ROWS),
        in_specs=[
            pl.BlockSpec((None, qtok, NA_WIDTH), lambda b, rb: (b, rb, 0)),
            kv_spec,
            kv_spec,
            pl.BlockSpec(bias.shape, lambda b, rb: (0, 0, 0, 0)),
        ],
        out_specs=pl.BlockSpec((None, qtok, NA_WIDTH), lambda b, rb: (b, rb, 0)),
        out_shape=jax.ShapeDtypeStruct((B, S, NA_WIDTH), BF16),
        compiler_params=_cparams(("parallel", "parallel")),
        name="na",
    )(q, k, v, bias)


def _s5_tables(lam_re, lam_im, log_dt, b_re, b_im, c_re, c_im):
    L = S5_CHUNK
    lr = jnp.minimum(lam_re.astype(F32), -1e-4)
    li = lam_im.astype(F32)
    dt = jnp.exp(log_dt.astype(F32))[:, :, None]
    mag = jnp.exp(lr * dt)
    ar = mag * jnp.cos(li * dt)
    ai = mag * jnp.sin(li * dt)
    den = lr * lr + li * li
    fr = ((ar - 1.0) * lr + ai * li) / den
    fi = (ai * lr - (ar - 1.0) * li) / den
    br = b_re.astype(F32)
    bi = b_im.astype(F32)
    bbr = fr[..., None] * br - fi[..., None] * bi
    bbi = fr[..., None] * bi + fi[..., None] * br
    cr = c_re.astype(F32)
    ci = c_im.astype(F32)
    j = jnp.arange(L + 1, dtype=F32)[:, None, None, None]
    pm = jnp.exp(j * (lr * dt)[None])
    pr = pm * jnp.cos(j * (li * dt)[None])
    pi_ = pm * jnp.sin(j * (li * dt)[None])
    abr = pr[..., None] * bbr[None] - pi_[..., None] * bbi[None]
    abi = pr[..., None] * bbi[None] + pi_[..., None] * bbr[None]
    kern = jnp.einsum('dgcp,jdgpe->jdgce', cr, abr) - jnp.einsum('dgcp,jdgpe->jdgce', ci, abi)
    s_idx = np.arange(L)[:, None]
    t_idx = np.arange(L)[None, :]
    lag_f = np.clip(t_idx - s_idx, 0, L)
    lag_b = np.clip(s_idx - t_idx, 0, L)
    kf = jnp.where((t_idx >= s_idx)[:, :, None, None, None], kern[lag_f, 0], 0.0)
    kb = jnp.where((s_idx >= t_idx)[:, :, None, None, None], kern[lag_b, 1], 0.0)
    tmat = jnp.transpose(kf + kb, (2, 0, 4, 1, 3)).reshape(S5_GROUPS, L * S5_GROUP_CH, L * S5_GROUP_CH)
    ef_r = abr[L - 1 - np.arange(L), 0]
    ef_i = abi[L - 1 - np.arange(L), 0]
    eb_r = abr[np.arange(L), 1]
    eb_i = abi[np.arange(L), 1]

    def to_rows(a):
        return jnp.transpose(a, (1, 0, 3, 2)).reshape(S5_GROUPS, L * S5_GROUP_CH, S5_STATE)

    we = jnp.concatenate([to_rows(ef_r), to_rows(eb_r), to_rows(ef_i), to_rows(eb_i)], axis=-1)
    def readout(d, powers):
        p_r = pr[powers, d]
        p_i = pi_[powers, d]
        m_re = cr[d][None] * p_r[:, :, None, :] - ci[d][None] * p_i[:, :, None, :]
        m_im = cr[d][None] * p_i[:, :, None, :] + ci[d][None] * p_r[:, :, None, :]
        to_cols = lambda a: jnp.transpose(a, (1, 3, 0, 2)).reshape(S5_GROUPS, S5_STATE, L * S5_GROUP_CH)
        return to_cols(m_re), to_cols(-m_im)

    yf_re, yf_im = readout(0, np.arange(L) + 1)
    yb_re, yb_im = readout(1, L - np.arange(L))
    wy = jnp.concatenate([yf_re, yb_re, yf_im, yb_im], axis=1)
    al = jnp.stack([jnp.concatenate([pr[L, 0], pr[L, 1]], axis=-1),
                    jnp.concatenate([pi_[L, 0], pi_[L, 1]], axis=-1)], axis=1)
    return tmat.astype(BF16), we.astype(BF16), wy.astype(BF16), al


def _s5_kernel(u_ref, t_ref, we_ref, wy_ref, al_ref, y_ref, e_ref, xp_ref, *, n_chunks, batch):
    lc = S5_CHUNK * S5_GROUP_CH
    u = u_ref[...].reshape(batch * n_chunks, lc)
    y_ref[...] = jnp.dot(u, t_ref[...], preferred_element_type=F32).reshape(batch, n_chunks, lc)
    e = jnp.dot(u, we_ref[...], preferred_element_type=F32)
    e_ref[...] = jnp.zeros(e_ref.shape, F32)
    half = 2 * S5_STATE
    for b in range(batch):
        for hf in range(2):
            e_ref[hf, pl.ds(b, n_chunks, stride=S5_BPAD), :] = e[b * n_chunks:(b + 1) * n_chunks,
                                                                 hf * half:(hf + 1) * half]
    a_re = al_ref[0:1, :]
    a_im = al_ref[1:2, :]
    lane = lax.broadcasted_iota(jnp.int32, (S5_BPAD, half), 1)
    is_fwd = lane < S5_STATE

    def body(j, carry):
        xr, xi = carry
        rf = pl.multiple_of(j * S5_BPAD, S5_BPAD)
        rb = pl.multiple_of((n_chunks - 1 - j) * S5_BPAD, S5_BPAD)
        xp_ref[pl.ds(rf, S5_BPAD), 0:S5_STATE] = xr[:, 0:S5_STATE]
        xp_ref[pl.ds(rb, S5_BPAD), S5_STATE:half] = xr[:, S5_STATE:half]
        xp_ref[pl.ds(rf, S5_BPAD), half:half + S5_STATE] = xi[:, 0:S5_STATE]
        xp_ref[pl.ds(rb, S5_BPAD), half + S5_STATE:2 * half] = xi[:, S5_STATE:half]
        er = jnp.where(is_fwd, e_ref[0, pl.ds(rf, S5_BPAD), :], e_ref[0, pl.ds(rb, S5_BPAD), :])
        ei = jnp.where(is_fwd, e_ref[1, pl.ds(rf, S5_BPAD), :], e_ref[1, pl.ds(rb, S5_BPAD), :])
        nxr = a_re * xr - a_im * xi + er
        nxi = a_re * xi + a_im * xr + ei
        return nxr, nxi

    zero = jnp.zeros((S5_BPAD, half), F32)
    lax.fori_loop(0, n_chunks, body, (zero, zero))
    yi = jnp.dot(xp_ref[...].astype(BF16), wy_ref[...], preferred_element_type=F32)
    e_ref[0] = yi[:, :half]
    e_ref[1] = yi[:, half:]
    for b in range(batch):
        y_ref[b] += jnp.concatenate(
            [e_ref[hf, pl.ds(b, n_chunks, stride=S5_BPAD), :] for hf in range(2)], axis=1)


def _s5(ug, tabs):
    tmat, we, wy, al = tabs
    G, B, n, lc = ug.shape
    assert B <= S5_BPAD and lc == 4 * S5_STATE
    rows = n * S5_BPAD
    grp = lambda shape: pl.BlockSpec((None,) + shape, lambda g: (g,) + (0,) * len(shape))
    return pl.pallas_call(
        functools.partial(_s5_kernel, n_chunks=n, batch=B),
        grid=(G,),
        in_specs=[grp((B, n, lc)), grp((lc, lc)), grp((lc, 4 * S5_STATE)), grp((4 * S5_STATE, lc)),
                  grp((2, 2 * S5_STATE))],
        out_specs=grp((B, n, lc)),
        out_shape=jax.ShapeDtypeStruct((G, B, n, lc), F32),
        scratch_shapes=[pltpu.VMEM((2, rows, 2 * S5_STATE), F32), pltpu.VMEM((rows, 4 * S5_STATE), F32)],
        compiler_params=_cparams(("parallel",)),
        name="s5",
    )(ug, tmat, we, wy, al)


def _memkv_kernel(m_ref, g_ref, w_ref, o_ref):
    x = m_ref[...]
    ms = jnp.mean(x * x, axis=-1, keepdims=True)
    h = (x * lax.rsqrt(ms + RMS_EPS) * g_ref[...]).astype(BF16)
    o_ref[...] = jnp.dot(h, w_ref[...], preferred_element_type=F32).astype(BF16)


def _memkv(mem, norm_mem, w_bf):
    B, M, D = mem.shape
    L = w_bf.shape[0]
    return pl.pallas_call(
        _memkv_kernel,
        grid=(L, B),
        in_specs=[pl.BlockSpec((None, M, D), lambda l, b: (b, 0, 0)),
                  pl.BlockSpec((None, 1, D), lambda l, b: (l, 0, 0)),
                  pl.BlockSpec((None, D, 2 * XA_WIDTH), lambda l, b: (l, 0, 0))],
        out_specs=pl.BlockSpec((None, None, M, 2 * XA_WIDTH), lambda l, b: (l, b, 0, 0)),
        out_shape=jax.ShapeDtypeStruct((L, B, M, 2 * XA_WIDTH), BF16),
        compiler_params=_cparams(("parallel", "parallel")),
        name="memkv",
    )(mem, norm_mem.reshape(L, 1, D), w_bf)


def _gelu_tanh(x):
    return 0.5 * x * (1.0 + jnp.tanh(math.sqrt(2.0 / math.pi) * (x + 0.044715 * (x * x * x))))


def _router(z):
    lane = lax.broadcasted_iota(jnp.int32, z.shape, 1).astype(F32)
    ninf = jnp.full_like(z, -jnp.inf)
    far = jnp.full_like(z, 1e9)
    cm = (lane >= MOE_EXPERTS) & (lane < MOE_EXPERTS + MOE_GROUPS)
    cmax = jnp.max(jnp.where(cm, z, ninf), axis=-1, keepdims=True)
    glane = jnp.min(jnp.where(cm & (z == cmax), lane, far), axis=-1, keepdims=True)
    psum = jnp.sum(jnp.where(cm, jnp.exp(jnp.where(cm, z, cmax) - cmax), 0.0), axis=-1, keepdims=True)
    p_grp = 1.0 / psum
    f0 = (glane - MOE_EXPERTS) * MOE_EPG
    fm = (lane >= f0) & (lane < f0 + MOE_EPG)
    v1 = jnp.max(jnp.where(fm, z, ninf), axis=-1, keepdims=True)
    i1 = jnp.min(jnp.where(fm & (z == v1), lane, far), axis=-1, keepdims=True)
    fm2 = fm & (lane != i1)
    v2 = jnp.max(jnp.where(fm2, z, ninf), axis=-1, keepdims=True)
    i2 = jnp.min(jnp.where(fm2 & (z == v2), lane, far), axis=-1, keepdims=True)
    t = jnp.exp(v2 - v1)
    w1 = p_grp / (1.0 + t)
    w2 = p_grp * t / (1.0 + t)
    return jnp.where(lane == i1, w1, 0.0) + jnp.where(lane == i2, w2, 0.0)


def _merge_kernel(x_ref, ona_ref, ys5_ref, u_ref, qx_ref, gate_ref, mkv_ref,
                  d_ref, wglu_ref, wna_ref, ws5_ref, wxa_ref, wout_ref, nffn_ref, wrh_ref, wrl_ref, br_ref,
                  xo_ref, h2_ref, comb_ref, mrg_ref, ys_ref):
    tm = x_ref.shape[0]
    _group_to_time_major(ys5_ref, ys_ref, tm // S5_CHUNK)
    ys = jnp.concatenate([ys_ref[0], ys_ref[1]], axis=1)
    y = _gelu_tanh(ys + d_ref[...] * u_ref[...])
    g = jnp.dot(y.astype(BF16), wglu_ref[...], preferred_element_type=F32)
    os5 = (g[:, :S5_WIDTH] * jax.nn.sigmoid(g[:, S5_WIDTH:])).astype(BF16)
    lane = lax.broadcasted_iota(jnp.int32, (tm, 128), 1)
    first_head = lane < XA_HEAD_DIM
    oxa = []
    for p in range(XA_HEADS // 2):
        cs = slice(128 * p, 128 * (p + 1))
        q2 = qx_ref[:, cs]
        k2 = mkv_ref[:, cs]
        v2 = mkv_ref[:, XA_WIDTH + 128 * p:XA_WIDTH + 128 * (p + 1)]
        outs = []
        for hh in range(2):
            qm = jnp.where(first_head if hh == 0 else jnp.logical_not(first_head), q2, jnp.zeros_like(q2))
            s = lax.dot_general(qm, k2, (((1,), (1,)), ((), ())), preferred_element_type=F32)
            m = jnp.max(s, axis=-1, keepdims=True)
            e = jnp.exp(s - m)
            l = jnp.sum(e, axis=-1, keepdims=True)
            outs.append(jnp.dot(e.astype(BF16), v2, preferred_element_type=F32) / l)
        oxa.append(jnp.where(first_head, outs[0], outs[1]).astype(BF16))
    oxa = jnp.concatenate(oxa, axis=-1)
    ona = ona_ref[...]
    cw = 256
    for c in range(D_MODEL // cw):
        cs = slice(cw * c, cw * (c + 1))
        m = gate_ref[:, cw * c:cw * (c + 1)].astype(F32) * jnp.dot(ona, wna_ref[:, cs], preferred_element_type=F32)
        m += gate_ref[:, D_MODEL + cw * c:D_MODEL + cw * (c + 1)].astype(F32) * jnp.dot(
            os5, ws5_ref[:, cs], preferred_element_type=F32)
        m += gate_ref[:, 2 * D_MODEL + cw * c:2 * D_MODEL + cw * (c + 1)].astype(F32) * jnp.dot(
            oxa, wxa_ref[:, cs], preferred_element_type=F32)
        mrg_ref[:, cs] = m.astype(BF16)
    xn = x_ref[...] + jnp.dot(mrg_ref[...], wout_ref[...], preferred_element_type=F32)
    xo_ref[...] = xn
    ms = jnp.mean(xn * xn, axis=-1, keepdims=True)
    h2 = xn * lax.rsqrt(ms + RMS_EPS) * nffn_ref[...]
    hi = h2.astype(BF16)
    lo = (h2 - hi.astype(F32)).astype(BF16)
    h2_ref[...] = hi
    z = (jnp.dot(hi, wrh_ref[...], preferred_element_type=F32)
         + jnp.dot(lo, wrh_ref[...], preferred_element_type=F32)
         + jnp.dot(hi, wrl_ref[...], preferred_element_type=F32)) + br_ref[...]
    comb_ref[...] = _router(z)


def _merge(x, ona, ys5, u, qx, gates, mkv, d, wglu, wna, ws5, wxa, wout, nffn, wrh, wrl, br):
    B, S, D = x.shape
    tm = min(TOKEN_TILE, S)
    M = mkv.shape[1]

    def tok(width):
        return pl.BlockSpec((None, tm, width), lambda b, i: (b, i, 0))

    def whole(a):
        return pl.BlockSpec(a.shape, lambda b, i: (0,) * a.ndim)

    d2 = d.reshape(1, S5_WIDTH)
    nffn2 = nffn.reshape(1, D)
    return pl.pallas_call(
        _merge_kernel,
        grid=(B, S // tm),
        in_specs=[tok(D), tok(NA_WIDTH),
                  pl.BlockSpec((S5_GROUPS, None, tm // S5_CHUNK, S5_CHUNK * S5_GROUP_CH), lambda b, i: (0, b, i, 0)),
                  tok(S5_WIDTH), tok(XA_WIDTH), tok(N_BRANCHES * D),
                  pl.BlockSpec((None, M, 2 * XA_WIDTH), lambda b, i: (b, 0, 0)),
                  whole(d2), whole(wglu), whole(wna), whole(ws5), whole(wxa), whole(wout), whole(nffn2),
                  whole(wrh), whole(wrl), whole(br)],
        out_specs=[tok(D), tok(D), tok(ROUTER_LANES)],
        out_shape=[jax.ShapeDtypeStruct((B, S, D), F32),
                   jax.ShapeDtypeStruct((B, S, D), BF16),
                   jax.ShapeDtypeStruct((B, S, ROUTER_LANES), F32)],
        scratch_shapes=[pltpu.VMEM((tm, D), BF16), pltpu.VMEM((2, tm, 128), F32)],
        compiler_params=_cparams(("parallel", "parallel")),
        name="merge",
    )(x, ona, ys5, u, qx, gates, mkv, d2, wglu, wna, ws5, wxa, wout, nffn2, wrh, wrl, br)


def _moe_kernel(x_ref, h_ref, comb_ref, wg_ref, wu_ref, wd_ref, fn_ref, o_ref, acc_ref, *, final_norm):
    j = pl.program_id(2)
    h = h_ref[...]
    comb = comb_ref[...]
    lane = lax.broadcasted_iota(jnp.int32, comb.shape, 1)
    acts = []
    for ee in range(MOE_ECHUNK):
        c = jnp.sum(jnp.where(lane == j * MOE_ECHUNK + ee, comb, 0.0), axis=-1, keepdims=True)
        g = jnp.dot(h, wg_ref[ee], preferred_element_type=F32)
        u = jnp.dot(h, wu_ref[ee], preferred_element_type=F32)
        acts.append((g * jax.nn.sigmoid(g) * u * c).astype(BF16))
    a = jnp.concatenate(acts, axis=-1)
    part = jnp.dot(a, wd_ref[...].reshape(MOE_ECHUNK * MOE_FF, D_MODEL), preferred_element_type=F32)

    @pl.when(j == 0)
    def _():
        acc_ref[...] = x_ref[...] + part

    @pl.when(j > 0)
    def _():
        acc_ref[...] += part

    @pl.when(j == pl.num_programs(2) - 1)
    def _():
        xn = acc_ref[...]
        if final_norm:
            ms = jnp.mean(xn * xn, axis=-1, keepdims=True)
            xn = xn * lax.rsqrt(ms + RMS_EPS) * fn_ref[...]
        o_ref[...] = xn


def _moe(x, h2, comb, wg, wu, wd, fnorm, final_norm):
    B, S, D = x.shape
    tm = min(TOKEN_TILE, S)

    def tok(width):
        return pl.BlockSpec((None, tm, width), lambda b, i, j: (b, i, 0))

    return pl.pallas_call(
        functools.partial(_moe_kernel, final_norm=final_norm),
        grid=(B, S // tm, MOE_EXPERTS // MOE_ECHUNK),
        in_specs=[tok(D), tok(D), tok(ROUTER_LANES),
                  pl.BlockSpec((MOE_ECHUNK, D, MOE_FF), lambda b, i, j: (j, 0, 0)),
                  pl.BlockSpec((MOE_ECHUNK, D, MOE_FF), lambda b, i, j: (j, 0, 0)),
                  pl.BlockSpec((MOE_ECHUNK, MOE_FF, D), lambda b, i, j: (j, 0, 0)),
                  pl.BlockSpec((1, D), lambda b, i, j: (0, 0))],
        out_specs=tok(D),
        out_shape=jax.ShapeDtypeStruct((B, S, D), F32),
        scratch_shapes=[pltpu.VMEM((tm, D), F32)],
        compiler_params=_cparams(("parallel", "parallel", "arbitrary")),
        name="moe",
    )(x, h2, comb, wg, wu, wd, fnorm.reshape(1, D))


def kernel(x, mem, norm_mix, norm_ffn, norm_mem, w_in, b_gate, na_rpb, s5_lam_re, s5_lam_im, s5_log_dt, s5_b_re, s5_b_im, s5_c_re, s5_c_im, s5_d, s5_w_glu, w_mem_kv, w_br_na, w_br_s5, w_br_xa, w_out, moe_w_coarse, moe_b_coarse, moe_w_fine, moe_b_fine, moe_w_gate, moe_w_up, moe_w_down, final_norm):
    depth = w_in.shape[0]
    mkv_all = _memkv(mem, norm_mem, w_mem_kv.astype(BF16))
    for l in range(depth):
        q, k, v, u, ug, qx, gates = _inproj(x, norm_mix[l], w_in[l].astype(BF16), b_gate[l])
        ona = _na(q, k, v, _na_bias_table(na_rpb[l]))
        ys5 = _s5(ug, _s5_tables(s5_lam_re[l], s5_lam_im[l], s5_log_dt[l], s5_b_re[l], s5_b_im[l],
                                s5_c_re[l], s5_c_im[l]))
        wr = jnp.concatenate([moe_w_fine[l], moe_w_coarse[l]], axis=1).astype(F32)
        wr = jnp.pad(wr, ((0, 0), (0, ROUTER_LANES - wr.shape[1])))
        wrh = wr.astype(BF16)
        wrl = (wr - wrh.astype(F32)).astype(BF16)
        br = jnp.pad(jnp.concatenate([moe_b_fine[l], moe_b_coarse[l]]).astype(F32),
                     (0, ROUTER_LANES - MOE_EXPERTS - MOE_GROUPS)).reshape(1, ROUTER_LANES)
        x, h2, comb = _merge(x, ona, ys5, u, qx, gates, mkv_all[l], s5_d[l], s5_w_glu[l].astype(BF16),
                             w_br_na[l].astype(BF16), w_br_s5[l].astype(BF16), w_br_xa[l].astype(BF16),
                             w_out[l].astype(BF16), norm_ffn[l], wrh, wrl, br)
        x = _moe(x, h2, comb, moe_w_gate[l].astype(BF16), moe_w_up[l].astype(BF16),
                 moe_w_down[l].astype(BF16), final_norm, final_norm=(l == depth - 1))
    return x
```

```python
import functools
import math

import numpy as np
import jax
import jax.numpy as jnp
from jax import lax
from jax.experimental import pallas as pl
from jax.experimental.pallas import tpu as pltpu

F32 = jnp.float32
BF16 = jnp.bfloat16

D_MODEL = 1024
GRID_W = 64
RMS_EPS = 1e-6
NA_HEADS = 8
NA_HEAD_DIM = 64
NA_WIDTH = NA_HEADS * NA_HEAD_DIM
NA_WIN_R = 8
NA_WIN_C = 16
NA_KV_ROWS = 24
NA_Q_ROWS = 8
S5_GROUPS = 16
S5_GROUP_CH = 16
S5_WIDTH = S5_GROUPS * S5_GROUP_CH
S5_STATE = 64
S5_CHUNK = 16
S5_BPAD = 8
XA_HEADS = 4
XA_HEAD_DIM = 64
XA_WIDTH = XA_HEADS * XA_HEAD_DIM
N_BRANCHES = 3
MOE_GROUPS = 4
MOE_EPG = 4
MOE_EXPERTS = MOE_GROUPS * MOE_EPG
MOE_FF = 256
MOE_ECHUNK = 4
ROUTER_LANES = 128
TOKEN_REC_ROWS = 8
MOE_TILE = 512
NEG_BIG = -1e30

VMEM_LIMIT = 52 * 1024 * 1024
TOKEN_TILE = 512


def _cparams(sem):
    return pltpu.CompilerParams(dimension_semantics=sem, vmem_limit_bytes=VMEM_LIMIT)


def _lane_block_mask(rows, j):
    lane = lax.broadcasted_iota(jnp.int32, (rows, 128), 1)
    lo = S5_GROUP_CH * (j % 8)
    return (lane >= lo) & (lane < lo + S5_GROUP_CH)


def _time_to_group_major(src_ref, rows):
    dest = [[None, None] for _ in range(S5_GROUPS)]
    for s in range(S5_CHUNK):
        halves = tuple(src_ref[hf, pl.ds(s, rows, stride=S5_CHUNK), :] for hf in range(2))
        m = _lane_block_mask(rows, s)
        for g in range(S5_GROUPS):
            shift = (S5_GROUP_CH * ((s % 8) - (g % 8))) % 128
            r = pltpu.roll(halves[g // 8], shift, axis=1) if shift else halves[g // 8]
            prev = dest[g][s // 8]
            dest[g][s // 8] = jnp.where(m, r, 0.0 if prev is None else prev)
    return [jnp.concatenate(d, axis=1) for d in dest]


def _group_to_time_major(src_ref, dst_ref, rows):
    for t in range(S5_CHUNK):
        out = [None, None]
        for g in range(S5_GROUPS):
            src = src_ref[g, :, 128 * (t // 8):128 * (t // 8 + 1)]
            shift = (S5_GROUP_CH * ((g % 8) - (t % 8))) % 128
            r = pltpu.roll(src, shift, axis=1) if shift else src
            prev = out[g // 8]
            out[g // 8] = jnp.where(_lane_block_mask(rows, g), r, 0.0 if prev is None else prev)
        for hf in range(2):
            dst_ref[hf, pl.ds(t, rows, stride=S5_CHUNK), :] = out[hf]


def _inproj_kernel(x_ref, g_ref, w_ref, bg_ref, q_ref, k_ref, v_ref, u_ref, ug_ref, qx_ref, gate_ref, uh_ref):
    x = x_ref[...]
    ms = jnp.mean(x * x, axis=-1, keepdims=True)
    h = (x * lax.rsqrt(ms + RMS_EPS) * g_ref[...]).astype(BF16)

    def proj(a, b):
        return jnp.dot(h, w_ref[:, a:b], preferred_element_type=F32)

    c0 = NA_WIDTH
    q_ref[...] = (proj(0, c0) * (NA_HEAD_DIM ** -0.5)).astype(BF16)
    k_ref[...] = proj(c0, 2 * c0).astype(BF16)
    v_ref[...] = proj(2 * c0, 3 * c0).astype(BF16)
    c1 = 3 * c0
    u = proj(c1, c1 + S5_WIDTH)
    u_ref[...] = u
    uh_ref[0] = u[:, :128]
    uh_ref[1] = u[:, 128:]
    for g, ug in enumerate(_time_to_group_major(uh_ref, ug_ref.shape[1])):
        ug_ref[g] = ug.astype(BF16)
    c2 = c1 + S5_WIDTH
    qx_ref[...] = (proj(c2, c2 + XA_WIDTH) * (XA_HEAD_DIM ** -0.5)).astype(BF16)
    c3 = c2 + XA_WIDTH
    for j in range(N_BRANCHES):
        z = proj(c3 + D_MODEL * j, c3 + D_MODEL * (j + 1)) + bg_ref[:, D_MODEL * j:D_MODEL * (j + 1)]
        gate_ref[:, D_MODEL * j:D_MODEL * (j + 1)] = jax.nn.sigmoid(z).astype(BF16)


def _inproj(x, g, w_bf, bg):
    B, S, D = x.shape
    tm = min(TOKEN_TILE, S)
    ncols = w_bf.shape[1]

    def tok(width):
        return pl.BlockSpec((None, tm, width), lambda b, i: (b, i, 0))

    def whole(shape):
        return pl.BlockSpec(shape, lambda b, i: (0,) * len(shape))

    return pl.pallas_call(
        _inproj_kernel,
        grid=(B, S // tm),
        in_specs=[tok(D), whole((1, D)), whole((D, ncols)), whole((1, N_BRANCHES * D))],
        out_specs=[tok(NA_WIDTH), tok(NA_WIDTH), tok(NA_WIDTH), tok(S5_WIDTH),
                   pl.BlockSpec((S5_GROUPS, None, tm // S5_CHUNK, S5_CHUNK * S5_GROUP_CH), lambda b, i: (0, b, i, 0)),
                   tok(XA_WIDTH), tok(N_BRANCHES * D)],
        out_shape=[
            jax.ShapeDtypeStruct((B, S, NA_WIDTH), BF16),
            jax.ShapeDtypeStruct((B, S, NA_WIDTH), BF16),
            jax.ShapeDtypeStruct((B, S, NA_WIDTH), BF16),
            jax.ShapeDtypeStruct((B, S, S5_WIDTH), F32),
            jax.ShapeDtypeStruct((S5_GROUPS, B, S // S5_CHUNK, S5_CHUNK * S5_GROUP_CH), BF16),
            jax.ShapeDtypeStruct((B, S, XA_WIDTH), BF16),
            jax.ShapeDtypeStruct((B, S, N_BRANCHES * D), BF16),
        ],
        scratch_shapes=[pltpu.VMEM((2, tm, 128), F32)],
        compiler_params=_cparams(("parallel", "parallel")),
        name="inproj",
    )(x, g.reshape(1, D), w_bf, bg.reshape(1, N_BRANCHES * D))


def _na_bias_table(rpb):
    cols = np.arange(GRID_W)
    col_start = np.clip(cols - NA_WIN_C // 2, 0, GRID_W - NA_WIN_C)
    cj = np.arange(GRID_W)[None, :]
    valid = (cj >= col_start[:, None]) & (cj < col_start[:, None] + NA_WIN_C)
    col_idx = np.clip(cj - cols[:, None] + (NA_WIN_C - 1), 0, 2 * NA_WIN_C - 2)
    d = np.arange(NA_WIN_R)[:, None]
    ri = np.arange(NA_WIN_R)[None, :]
    row_idx = ri - d + (NA_WIN_R - 1)
    row_sel = (row_idx[:, :, None] == np.arange(2 * NA_WIN_R - 1)).astype(np.float32)
    col_sel = ((col_idx[:, :, None] == np.arange(2 * NA_WIN_C - 1)) & valid[:, :, None]).astype(np.float32)
    t = jnp.einsum('hrk,cjk->hrcj', rpb.astype(F32), col_sel, precision=lax.Precision.HIGHEST)
    t = jnp.einsum('hrcj,dir->dhcij', t, row_sel, precision=lax.Precision.HIGHEST)
    t = t + jnp.where(valid, 0.0, NEG_BIG)[None, None, :, None, :].astype(F32)
    return t.reshape(NA_WIN_R, NA_HEADS, GRID_W, NA_WIN_R * GRID_W)


def _na_kernel(q_ref, k_ref, v_ref, bias_ref, o_ref, *, rows_total):
    r0 = pl.program_id(1) * NA_Q_ROWS
    kstart = jnp.clip(r0 - NA_Q_ROWS, 0, rows_total - NA_KV_ROWS)
    lane = lax.broadcasted_iota(jnp.int32, (GRID_W, 128), 1)
    first_head = lane < NA_HEAD_DIM
    win = NA_WIN_R * GRID_W

    def body(i, carry):
        r = r0 + i
        rs = jnp.clip(r - NA_WIN_R // 2, 0, rows_total - NA_WIN_R)
        koff = pl.multiple_of((rs - kstart) * GRID_W, GRID_W)
        didx = r - rs
        qoff = pl.multiple_of(i * GRID_W, GRID_W)
        scores = []
        for h in range(NA_HEADS):
            cs = slice(128 * (h // 2), 128 * (h // 2 + 1))
            q2 = q_ref[pl.ds(qoff, GRID_W), cs]
            qm = jnp.where(first_head if h % 2 == 0 else jnp.logical_not(first_head), q2, jnp.zeros_like(q2))
            s = lax.dot_general(qm, k_ref[pl.ds(koff, win), cs], (((1,), (1,)), ((), ())),
                                preferred_element_type=F32)
            scores.append(s + bias_ref[didx, h])
        probs = []
        for s in scores:
            m = jnp.max(s, axis=-1, keepdims=True)
            e = jnp.exp(s - m)
            probs.append((e.astype(BF16), jnp.sum(e, axis=-1, keepdims=True)))
        outs = []
        for h, (e, l) in enumerate(probs):
            cs = slice(128 * (h // 2), 128 * (h // 2 + 1))
            outs.append(jnp.dot(e, v_ref[pl.ds(koff, win), cs], preferred_element_type=F32) / l)
        for p in range(NA_HEADS // 2):
            o_ref[pl.ds(qoff, GRID_W), 128 * p:128 * (p + 1)] = jnp.where(
                first_head, outs[2 * p], outs[2 * p + 1]).astype(BF16)
        return carry

    lax.fori_loop(0, NA_Q_ROWS, body, 0)


def _na(q, k, v, bias):
    B, S, _ = q.shape
    rows = S // GRID_W
    assert rows >= NA_KV_ROWS and rows % NA_Q_ROWS == 0
    qtok = NA_Q_ROWS * GRID_W
    kvtok = NA_KV_ROWS * GRID_W

    def kv_map(b, rb):
        return (b, jnp.clip(rb * NA_Q_ROWS - NA_Q_ROWS, 0, rows - NA_KV_ROWS) * GRID_W, 0)

    kv_spec = pl.BlockSpec((None, pl.Element(kvtok), pl.Element(NA_WIDTH)), kv_map)
    return pl.pallas_call(
        functools.partial(_na_kernel, rows_total=rows),
        grid=(B, rows // NA_Q_ROWS),
        in_specs=[
            pl.BlockSpec((None, qtok, NA_WIDTH), lambda b, rb: (b, rb, 0)),
            kv_spec,
            kv_spec,
            pl.BlockSpec(bias.shape, lambda b, rb: (0, 0, 0, 0)),
        ],
        out_specs=pl.BlockSpec((None, qtok, NA_WIDTH), lambda b, rb: (b, rb, 0)),
        out_shape=jax.ShapeDtypeStruct((B, S, NA_WIDTH), BF16),
        compiler_params=_cparams(("parallel", "parallel")),
        name="na",
    )(q, k, v, bias)


def _s5_tables(lam_re, lam_im, log_dt, b_re, b_im, c_re, c_im):
    L = S5_CHUNK
    lr = jnp.minimum(lam_re.astype(F32), -1e-4)
    li = lam_im.astype(F32)
    dt = jnp.exp(log_dt.astype(F32))[:, :, None]
    mag = jnp.exp(lr * dt)
    ar = mag * jnp.cos(li * dt)
    ai = mag * jnp.sin(li * dt)
    den = lr * lr + li * li
    fr = ((ar - 1.0) * lr + ai * li) / den
    fi = (ai * lr - (ar - 1.0) * li) / den
    br = b_re.astype(F32)
    bi = b_im.astype(F32)
    bbr = fr[..., None] * br - fi[..., None] * bi
    bbi = fr[..., None] * bi + fi[..., None] * br
    cr = c_re.astype(F32)
    ci = c_im.astype(F32)
    j = jnp.arange(L + 1, dtype=F32)[:, None, None, None]
    pm = jnp.exp(j * (lr * dt)[None])
    pr = pm * jnp.cos(j * (li * dt)[None])
    pi_ = pm * jnp.sin(j * (li * dt)[None])
    abr = pr[..., None] * bbr[None] - pi_[..., None] * bbi[None]
    abi = pr[..., None] * bbi[None] + pi_[..., None] * bbr[None]
    kern = jnp.einsum('dgcp,jdgpe->jdgce', cr, abr) - jnp.einsum('dgcp,jdgpe->jdgce', ci, abi)
    s_idx = np.arange(L)[:, None]
    t_idx = np.arange(L)[None, :]
    lag_f = np.clip(t_idx - s_idx, 0, L)
    lag_b = np.clip(s_idx - t_idx, 0, L)
    kf = jnp.where((t_idx >= s_idx)[:, :, None, None, None], kern[lag_f, 0], 0.0)
    kb = jnp.where((s_idx >= t_idx)[:, :, None, None, None], kern[lag_b, 1], 0.0)
    tmat = jnp.transpose(kf + kb, (2, 0, 4, 1, 3)).reshape(S5_GROUPS, L * S5_GROUP_CH, L * S5_GROUP_CH)
    ef_r = abr[L - 1 - np.arange(L), 0]
    ef_i = abi[L - 1 - np.arange(L), 0]
    eb_r = abr[np.arange(L), 1]
    eb_i = abi[np.arange(L), 1]

    def to_rows(a):
        return jnp.transpose(a, (1, 0, 3, 2)).reshape(S5_GROUPS, L * S5_GROUP_CH, S5_STATE)

    we = jnp.concatenate([to_rows(ef_r), to_rows(eb_r), to_rows(ef_i), to_rows(eb_i)], axis=-1)
    def readout(d, powers):
        p_r = pr[powers, d]
        p_i = pi_[powers, d]
        m_re = cr[d][None] * p_r[:, :, None, :] - ci[d][None] * p_i[:, :, None, :]
        m_im = cr[d][None] * p_i[:, :, None, :] + ci[d][None] * p_r[:, :, None, :]
        to_cols = lambda a: jnp.transpose(a, (1, 3, 0, 2)).reshape(S5_GROUPS, S5_STATE, L * S5_GROUP_CH)
        return to_cols(m_re), to_cols(-m_im)

    yf_re, yf_im = readout(0, np.arange(L) + 1)
    yb_re, yb_im = readout(1, L - np.arange(L))
    wy = jnp.concatenate([yf_re, yb_re, yf_im, yb_im], axis=1)
    al = jnp.stack([jnp.concatenate([pr[L, 0], pr[L, 1]], axis=-1),
                    jnp.concatenate([pi_[L, 0], pi_[L, 1]], axis=-1)], axis=1)
    return tmat.astype(BF16), we.astype(BF16), wy.astype(BF16), al


def _s5_kernel(u_ref, t_ref, we_ref, wy_ref, al_ref, y_ref, e_ref, xp_ref, *, n_chunks, batch):
    lc = S5_CHUNK * S5_GROUP_CH
    u = u_ref[...].reshape(batch * n_chunks, lc)
    y_ref[...] = jnp.dot(u, t_ref[...], preferred_element_type=F32).reshape(batch, n_chunks, lc)
    e = jnp.dot(u, we_ref[...], preferred_element_type=F32)
    e_ref[...] = jnp.zeros(e_ref.shape, F32)
    half = 2 * S5_STATE
    for b in range(batch):
        for hf in range(2):
            e_ref[hf, pl.ds(b, n_chunks, stride=S5_BPAD), :] = e[b * n_chunks:(b + 1) * n_chunks,
                                                                 hf * half:(hf + 1) * half]
    a_re = al_ref[0:1, :]
    a_im = al_ref[1:2, :]
    lane = lax.broadcasted_iota(jnp.int32, (S5_BPAD, half), 1)
    is_fwd = lane < S5_STATE

    def body(j, carry):
        xr, xi = carry
        rf = pl.multiple_of(j * S5_BPAD, S5_BPAD)
        rb = pl.multiple_of((n_chunks - 1 - j) * S5_BPAD, S5_BPAD)
        xp_ref[pl.ds(rf, S5_BPAD), 0:S5_STATE] = xr[:, 0:S5_STATE]
        xp_ref[pl.ds(rb, S5_BPAD), S5_STATE:half] = xr[:, S5_STATE:half]
        xp_ref[pl.ds(rf, S5_BPAD), half:half + S5_STATE] = xi[:, 0:S5_STATE]
        xp_ref[pl.ds(rb, S5_BPAD), half + S5_STATE:2 * half] = xi[:, S5_STATE:half]
        er = jnp.where(is_fwd, e_ref[0, pl.ds(rf, S5_BPAD), :], e_ref[0, pl.ds(rb, S5_BPAD), :])
        ei = jnp.where(is_fwd, e_ref[1, pl.ds(rf, S5_BPAD), :], e_ref[1, pl.ds(rb, S5_BPAD), :])
        nxr = a_re * xr - a_im * xi + er
        nxi = a_re * xi + a_im * xr + ei
        return nxr, nxi

    zero = jnp.zeros((S5_BPAD, half), F32)
    lax.fori_loop(0, n_chunks, body, (zero, zero))
    yi = jnp.dot(xp_ref[...].astype(BF16), wy_ref[...], preferred_element_type=F32)
    e_ref[0] = yi[:, :half]
    e_ref[1] = yi[:, half:]
    for b in range(batch):
        y_ref[b] += jnp.concatenate(
            [e_ref[hf, pl.ds(b, n_chunks, stride=S5_BPAD), :] for hf in range(2)], axis=1)


def _s5(ug, tabs):
    tmat, we, wy, al = tabs
    G, B, n, lc = ug.shape
    assert B <= S5_BPAD and lc == 4 * S5_STATE
    rows = n * S5_BPAD
    grp = lambda shape: pl.BlockSpec((None,) + shape, lambda g: (g,) + (0,) * len(shape))
    return pl.pallas_call(
        functools.partial(_s5_kernel, n_chunks=n, batch=B),
        grid=(G,),
        in_specs=[grp((B, n, lc)), grp((lc, lc)), grp((lc, 4 * S5_STATE)), grp((4 * S5_STATE, lc)),
                  grp((2, 2 * S5_STATE))],
        out_specs=grp((B, n, lc)),
        out_shape=jax.ShapeDtypeStruct((G, B, n, lc), F32),
        scratch_shapes=[pltpu.VMEM((2, rows, 2 * S5_STATE), F32), pltpu.VMEM((rows, 4 * S5_STATE), F32)],
        compiler_params=_cparams(("parallel",)),
        name="s5",
    )(ug, tmat, we, wy, al)


def _memkv_kernel(m_ref, g_ref, w_ref, o_ref):
    x = m_ref[...]
    ms = jnp.mean(x * x, axis=-1, keepdims=True)
    h = (x * lax.rsqrt(ms + RMS_EPS) * g_ref[...]).astype(BF16)
    o_ref[...] = jnp.dot(h, w_ref[...], preferred_element_type=F32).astype(BF16)


def _memkv(mem, norm_mem, w_bf):
    B, M, D = mem.shape
    L = w_bf.shape[0]
    return pl.pallas_call(
        _memkv_kernel,
        grid=(L, B),
        in_specs=[pl.BlockSpec((None, M, D), lambda l, b: (b, 0, 0)),
                  pl.BlockSpec((None, 1, D), lambda l, b: (l, 0, 0)),
                  pl.BlockSpec((None, D, 2 * XA_WIDTH), lambda l, b: (l, 0, 0))],
        out_specs=pl.BlockSpec((None, None, M, 2 * XA_WIDTH), lambda l, b: (l, b, 0, 0)),
        out_shape=jax.ShapeDtypeStruct((L, B, M, 2 * XA_WIDTH), BF16),
        compiler_params=_cparams(("parallel", "parallel")),
        name="memkv",
    )(mem, norm_mem.reshape(L, 1, D), w_bf)


def _gelu_tanh(x):
    return 0.5 * x * (1.0 + jnp.tanh(math.sqrt(2.0 / math.pi) * (x + 0.044715 * (x * x * x))))


def _router(z):
    lane = lax.broadcasted_iota(jnp.int32, z.shape, 1).astype(F32)
    ninf = jnp.full_like(z, -jnp.inf)
    far = jnp.full_like(z, 1e9)
    cm = (lane >= MOE_EXPERTS) & (lane < MOE_EXPERTS + MOE_GROUPS)
    cmax = jnp.max(jnp.where(cm, z, ninf), axis=-1, keepdims=True)
    glane = jnp.min(jnp.where(cm & (z == cmax), lane, far), axis=-1, keepdims=True)
    psum = jnp.sum(jnp.where(cm, jnp.exp(jnp.where(cm, z, cmax) - cmax), 0.0), axis=-1, keepdims=True)
    p_grp = 1.0 / psum
    f0 = (glane - MOE_EXPERTS) * MOE_EPG
    fm = (lane >= f0) & (lane < f0 + MOE_EPG)
    v1 = jnp.max(jnp.where(fm, z, ninf), axis=-1, keepdims=True)
    i1 = jnp.min(jnp.where(fm & (z == v1), lane, far), axis=-1, keepdims=True)
    fm2 = fm & (lane != i1)
    v2 = jnp.max(jnp.where(fm2, z, ninf), axis=-1, keepdims=True)
    i2 = jnp.min(jnp.where(fm2 & (z == v2), lane, far), axis=-1, keepdims=True)
    t = jnp.exp(v2 - v1)
    w1 = p_grp / (1.0 + t)
    w2 = p_grp * t / (1.0 + t)
    comb = jnp.where(lane == i1, w1, 0.0) + jnp.where(lane == i2, w2, 0.0)
    return jnp.where(lane == MOE_EXPERTS, glane - MOE_EXPERTS, comb)


def _merge_kernel(x_ref, ona_ref, ys5_ref, u_ref, qx_ref, gate_ref, mkv_ref,
                  d_ref, wglu_ref, wna_ref, ws5_ref, wxa_ref, wout_ref, nffn_ref, wrh_ref, wrl_ref, br_ref,
                  xo_ref, hp_ref, mrg_ref, ys_ref):
    tm = x_ref.shape[0]
    _group_to_time_major(ys5_ref, ys_ref, tm // S5_CHUNK)
    ys = jnp.concatenate([ys_ref[0], ys_ref[1]], axis=1)
    y = _gelu_tanh(ys + d_ref[...] * u_ref[...])
    g = jnp.dot(y.astype(BF16), wglu_ref[...], preferred_element_type=F32)
    os5 = (g[:, :S5_WIDTH] * jax.nn.sigmoid(g[:, S5_WIDTH:])).astype(BF16)
    lane = lax.broadcasted_iota(jnp.int32, (tm, 128), 1)
    first_head = lane < XA_HEAD_DIM
    oxa = []
    for p in range(XA_HEADS // 2):
        cs = slice(128 * p, 128 * (p + 1))
        q2 = qx_ref[:, cs]
        k2 = mkv_ref[:, cs]
        v2 = mkv_ref[:, XA_WIDTH + 128 * p:XA_WIDTH + 128 * (p + 1)]
        outs = []
        for hh in range(2):
            qm = jnp.where(first_head if hh == 0 else jnp.logical_not(first_head), q2, jnp.zeros_like(q2))
            s = lax.dot_general(qm, k2, (((1,), (1,)), ((), ())), preferred_element_type=F32)
            m = jnp.max(s, axis=-1, keepdims=True)
            e = jnp.exp(s - m)
            l = jnp.sum(e, axis=-1, keepdims=True)
            outs.append(jnp.dot(e.astype(BF16), v2, preferred_element_type=F32) / l)
        oxa.append(jnp.where(first_head, outs[0], outs[1]).astype(BF16))
    oxa = jnp.concatenate(oxa, axis=-1)
    ona = ona_ref[...]
    cw = 256
    for c in range(D_MODEL // cw):
        cs = slice(cw * c, cw * (c + 1))
        m = gate_ref[:, cw * c:cw * (c + 1)].astype(F32) * jnp.dot(ona, wna_ref[:, cs], preferred_element_type=F32)
        m += gate_ref[:, D_MODEL + cw * c:D_MODEL + cw * (c + 1)].astype(F32) * jnp.dot(
            os5, ws5_ref[:, cs], preferred_element_type=F32)
        m += gate_ref[:, 2 * D_MODEL + cw * c:2 * D_MODEL + cw * (c + 1)].astype(F32) * jnp.dot(
            oxa, wxa_ref[:, cs], preferred_element_type=F32)
        mrg_ref[:, cs] = m.astype(BF16)
    xn = x_ref[...] + jnp.dot(mrg_ref[...], wout_ref[...], preferred_element_type=F32)
    xo_ref[...] = xn
    ms = jnp.mean(xn * xn, axis=-1, keepdims=True)
    h2 = xn * lax.rsqrt(ms + RMS_EPS) * nffn_ref[...]
    hi = h2.astype(BF16)
    hi32 = hi.astype(F32)
    lo = (h2 - hi32).astype(BF16)
    z = (jnp.dot(hi, wrh_ref[...], preferred_element_type=F32)
         + jnp.dot(lo, wrh_ref[...], preferred_element_type=F32)
         + jnp.dot(hi, wrl_ref[...], preferred_element_type=F32)) + br_ref[...]
    hw = D_MODEL // 2
    packed = pltpu.bitcast(hi32[:, :hw], jnp.uint32) | (pltpu.bitcast(hi32[:, hw:], jnp.uint32) >> 16)
    for j in range(hw // 128):
        hp_ref[pl.ds(j, tm, stride=TOKEN_REC_ROWS), :] = packed[:, 128 * j:128 * (j + 1)]
    hp_ref[pl.ds(hw // 128, tm, stride=TOKEN_REC_ROWS), :] = pltpu.bitcast(_router(z), jnp.uint32)
    for j in range(hw // 128 + 1, TOKEN_REC_ROWS):
        hp_ref[pl.ds(j, tm, stride=TOKEN_REC_ROWS), :] = jnp.zeros((tm, 128), jnp.uint32)


def _merge(x, ona, ys5, u, qx, gates, mkv, d, wglu, wna, ws5, wxa, wout, nffn, wrh, wrl, br):
    B, S, D = x.shape
    tm = min(TOKEN_TILE, S)
    M = mkv.shape[1]

    def tok(width):
        return pl.BlockSpec((None, tm, width), lambda b, i: (b, i, 0))

    def whole(a):
        return pl.BlockSpec(a.shape, lambda b, i: (0,) * a.ndim)

    d2 = d.reshape(1, S5_WIDTH)
    nffn2 = nffn.reshape(1, D)
    return pl.pallas_call(
        _merge_kernel,
        grid=(B, S // tm),
        in_specs=[tok(D), tok(NA_WIDTH),
                  pl.BlockSpec((S5_GROUPS, None, tm // S5_CHUNK, S5_CHUNK * S5_GROUP_CH), lambda b, i: (0, b, i, 0)),
                  tok(S5_WIDTH), tok(XA_WIDTH), tok(N_BRANCHES * D),
                  pl.BlockSpec((None, M, 2 * XA_WIDTH), lambda b, i: (b, 0, 0)),
                  whole(d2), whole(wglu), whole(wna), whole(ws5), whole(wxa), whole(wout), whole(nffn2),
                  whole(wrh), whole(wrl), whole(br)],
        out_specs=[tok(D), pl.BlockSpec((None, tm * TOKEN_REC_ROWS, 128), lambda b, i: (b, i, 0))],
        out_shape=[jax.ShapeDtypeStruct((B, S, D), F32),
                   jax.ShapeDtypeStruct((B, S * TOKEN_REC_ROWS, 128), jnp.uint32)],
        scratch_shapes=[pltpu.VMEM((tm, D), BF16), pltpu.VMEM((2, tm, 128), F32)],
        compiler_params=_cparams(("parallel", "parallel")),
        name="merge",
    )(x, ona, ys5, u, qx, gates, mkv, d2, wglu, wna, ws5, wxa, wout, nffn2, wrh, wrl, br)


def _moe_plan(hp, n_tokens):
    gid_words = hp.reshape(n_tokens, TOKEN_REC_ROWS, 128)[:, D_MODEL // 2 // 128, MOE_EXPERTS]
    gid = lax.bitcast_convert_type(gid_words, F32).astype(jnp.int32)
    onehot = (gid[:, None] == jnp.arange(MOE_GROUPS, dtype=jnp.int32)[None, :]).astype(jnp.int32)
    csum = jnp.cumsum(onehot, axis=0)
    counts = csum[-1]
    rank = jnp.sum((csum - onehot) * onehot, axis=1)
    padded = ((counts + MOE_TILE - 1) // MOE_TILE) * MOE_TILE
    ends = jnp.cumsum(padded)
    pos = jnp.sum(onehot * (ends - padded)[None, :], axis=1) + rank
    n_tiles = n_tokens // MOE_TILE + MOE_GROUPS
    starts = jnp.arange(n_tiles, dtype=jnp.int32) * MOE_TILE
    tile_grp = jnp.minimum(jnp.sum(starts[:, None] >= ends[None, :], axis=1), MOE_GROUPS - 1).astype(jnp.int32)
    return pos.astype(jnp.int32), ends.astype(jnp.int32), padded.astype(jnp.int32), tile_grp


def _rec(ref, token, count=1):
    return ref.at[pl.ds(pl.multiple_of(token * TOKEN_REC_ROWS, TOKEN_REC_ROWS), count * TOKEN_REC_ROWS), :]


def _rec_dma_wait_all(src_ref, dst_ref, sem, count):
    def body(r, c):
        pltpu.make_async_copy(_rec(src_ref, 0), _rec(dst_ref, 0), sem).wait()
        return c
    lax.fori_loop(0, count, body, 0)


def _permute_kernel(ends_ref, padded_ref, pos_ref, h_ref, hs_ref, zero_ref, sem):
    tm = h_ref.shape[0] // TOKEN_REC_ROWS

    @pl.when(pl.program_id(0) == 0)
    def _():
        zero_ref[...] = jnp.zeros(zero_ref.shape, zero_ref.dtype)
        n_rows = hs_ref.shape[0] // TOKEN_REC_ROWS
        for g in range(MOE_GROUPS):
            tail = ends_ref[MOE_GROUPS - 1] + g * MOE_TILE
            for cond, start in ((padded_ref[g] > 0, ends_ref[g] - MOE_TILE), (tail < n_rows, tail)):
                @pl.when(cond)
                def _():
                    cp = pltpu.make_async_copy(zero_ref, _rec(hs_ref, start, MOE_TILE), sem)
                    cp.start()
                    cp.wait()

    def issue(r, c):
        pltpu.make_async_copy(_rec(h_ref, r), _rec(hs_ref, pos_ref[0, 0, r]), sem).start()
        return c

    lax.fori_loop(0, tm, issue, 0, unroll=8)
    _rec_dma_wait_all(h_ref, hs_ref, sem, tm)


def _moe_permute(hp2, pos3, ends, padded, n_rows):
    T = hp2.shape[0] // TOKEN_REC_ROWS
    tm = min(TOKEN_TILE, T)
    return pl.pallas_call(
        _permute_kernel,
        grid_spec=pltpu.PrefetchScalarGridSpec(
            num_scalar_prefetch=2,
            grid=(T // tm,),
            in_specs=[pl.BlockSpec((1, 1, tm), lambda i, e, p: (i, 0, 0), memory_space=pltpu.SMEM),
                      pl.BlockSpec((tm * TOKEN_REC_ROWS, 128), lambda i, e, p: (i, 0))],
            out_specs=pl.BlockSpec(memory_space=pl.ANY),
            scratch_shapes=[pltpu.VMEM((MOE_TILE * TOKEN_REC_ROWS, 128), jnp.uint32), pltpu.SemaphoreType.DMA(())],
        ),
        out_shape=jax.ShapeDtypeStruct((n_rows * TOKEN_REC_ROWS, 128), jnp.uint32),
        compiler_params=_cparams(("arbitrary",)),
        name="moe_permute",
    )(ends, padded, pos3, hp2)


def _moe_kernel(grp_ref, h_ref, wg_ref, wu_ref, wd_ref, o_ref):
    hw = D_MODEL // 2

    def field(j):
        return h_ref[pl.ds(j, MOE_TILE, stride=TOKEN_REC_ROWS), :]

    w = jnp.concatenate([field(j) for j in range(hw // 128)], axis=1)
    h = jnp.concatenate([pltpu.bitcast(w & jnp.uint32(0xFFFF0000), F32).astype(BF16),
                         pltpu.bitcast(w << 16, F32).astype(BF16)], axis=1)
    comb = pltpu.bitcast(field(hw // 128), F32)
    lane = lax.broadcasted_iota(jnp.int32, comb.shape, 1)
    first = grp_ref[pl.program_id(0)] * MOE_EPG
    acts = []
    for ee in range(MOE_EPG):
        c = jnp.sum(jnp.where(lane == first + ee, comb, 0.0), axis=-1, keepdims=True)
        g = jnp.dot(h, wg_ref[ee], preferred_element_type=F32)
        u = jnp.dot(h, wu_ref[ee], preferred_element_type=F32)
        acts.append((g * jax.nn.sigmoid(g) * u * c).astype(BF16))
    a = jnp.concatenate(acts, axis=-1)
    out = jnp.dot(a, wd_ref[...].reshape(MOE_EPG * MOE_FF, D_MODEL), preferred_element_type=F32)
    for j in range(TOKEN_REC_ROWS):
        o_ref[pl.ds(j, MOE_TILE, stride=TOKEN_REC_ROWS), :] = out[:, 128 * j:128 * (j + 1)]


def _moe_experts(hs, tile_grp, wg, wu, wd):
    n_rows = hs.shape[0] // TOKEN_REC_ROWS
    n_tiles = n_rows // MOE_TILE
    rec_tile = pl.BlockSpec((MOE_TILE * TOKEN_REC_ROWS, 128), lambda i, g: (i, 0))
    return pl.pallas_call(
        _moe_kernel,
        grid_spec=pltpu.PrefetchScalarGridSpec(
            num_scalar_prefetch=1,
            grid=(n_tiles,),
            in_specs=[rec_tile,
                      pl.BlockSpec((MOE_EPG, D_MODEL, MOE_FF), lambda i, g: (g[i], 0, 0)),
                      pl.BlockSpec((MOE_EPG, D_MODEL, MOE_FF), lambda i, g: (g[i], 0, 0)),
                      pl.BlockSpec((MOE_EPG, MOE_FF, D_MODEL), lambda i, g: (g[i], 0, 0))],
            out_specs=rec_tile,
        ),
        out_shape=jax.ShapeDtypeStruct((n_rows * TOKEN_REC_ROWS, 128), F32),
        compiler_params=_cparams(("arbitrary",)),
        name="moe_experts",
    )(tile_grp, hs, wg, wu, wd)


def _unpermute_kernel(pos_ref, x_ref, ys_ref, fn_ref, o_ref, buf_ref, sem, *, final_norm):
    tm = x_ref.shape[0]

    def issue(r, c):
        pltpu.make_async_copy(_rec(ys_ref, pos_ref[0, 0, r]), _rec(buf_ref, r), sem).start()
        return c

    lax.fori_loop(0, tm, issue, 0, unroll=8)
    _rec_dma_wait_all(ys_ref, buf_ref, sem, tm)
    moe = jnp.concatenate([buf_ref[pl.ds(j, tm, stride=TOKEN_REC_ROWS), :] for j in range(TOKEN_REC_ROWS)], axis=1)
    xn = x_ref[...] + moe
    if final_norm:
        ms = jnp.mean(xn * xn, axis=-1, keepdims=True)
        xn = xn * lax.rsqrt(ms + RMS_EPS) * fn_ref[...]
    o_ref[...] = xn


def _moe_unpermute(x2, ys, pos3, fnorm, final_norm):
    T, D = x2.shape
    tm = min(TOKEN_TILE, T)
    return pl.pallas_call(
        functools.partial(_unpermute_kernel, final_norm=final_norm),
        grid=(T // tm,),
        in_specs=[pl.BlockSpec((1, 1, tm), lambda i: (i, 0, 0), memory_space=pltpu.SMEM),
                  pl.BlockSpec((tm, D), lambda i: (i, 0)),
                  pl.BlockSpec(memory_space=pl.ANY),
                  pl.BlockSpec((1, D), lambda i: (0, 0))],
        out_specs=pl.BlockSpec((tm, D), lambda i: (i, 0)),
        out_shape=jax.ShapeDtypeStruct((T, D), F32),
        scratch_shapes=[pltpu.VMEM((tm * TOKEN_REC_ROWS, 128), F32), pltpu.SemaphoreType.DMA(())],
        compiler_params=_cparams(("arbitrary",)),
        name="moe_unpermute",
    )(pos3, x2, ys, fnorm.reshape(1, D))


def _moe(x, hp, wg, wu, wd, fnorm, final_norm):
    B, S, D = x.shape
    T = B * S
    tm = min(TOKEN_TILE, T)
    pos, ends, padded, tile_grp = _moe_plan(hp, T)
    pos3 = pos.reshape(T // tm, 1, tm)
    n_rows = T + MOE_GROUPS * MOE_TILE
    hs = _moe_permute(hp.reshape(T * TOKEN_REC_ROWS, 128), pos3, ends, padded, n_rows)
    ys = _moe_experts(hs, tile_grp, wg, wu, wd)
    return _moe_unpermute(x.reshape(T, D), ys, pos3, fnorm, final_norm).reshape(B, S, D)


def kernel(x, mem, norm_mix, norm_ffn, norm_mem, w_in, b_gate, na_rpb, s5_lam_re, s5_lam_im, s5_log_dt, s5_b_re, s5_b_im, s5_c_re, s5_c_im, s5_d, s5_w_glu, w_mem_kv, w_br_na, w_br_s5, w_br_xa, w_out, moe_w_coarse, moe_b_coarse, moe_w_fine, moe_b_fine, moe_w_gate, moe_w_up, moe_w_down, final_norm):
    depth = w_in.shape[0]
    mkv_all = _memkv(mem, norm_mem, w_mem_kv.astype(BF16))
    for l in range(depth):
        q, k, v, u, ug, qx, gates = _inproj(x, norm_mix[l], w_in[l].astype(BF16), b_gate[l])
        ona = _na(q, k, v, _na_bias_table(na_rpb[l]))
        ys5 = _s5(ug, _s5_tables(s5_lam_re[l], s5_lam_im[l], s5_log_dt[l], s5_b_re[l], s5_b_im[l],
                                s5_c_re[l], s5_c_im[l]))
        wr = jnp.concatenate([moe_w_fine[l], moe_w_coarse[l]], axis=1).astype(F32)
        wr = jnp.pad(wr, ((0, 0), (0, ROUTER_LANES - wr.shape[1])))
        wrh = wr.astype(BF16)
        wrl = (wr - wrh.astype(F32)).astype(BF16)
        br = jnp.pad(jnp.concatenate([moe_b_fine[l], moe_b_coarse[l]]).astype(F32),
                     (0, ROUTER_LANES - MOE_EXPERTS - MOE_GROUPS)).reshape(1, ROUTER_LANES)
        x, hp = _merge(x, ona, ys5, u, qx, gates, mkv_all[l], s5_d[l], s5_w_glu[l].astype(BF16),
                       w_br_na[l].astype(BF16), w_br_s5[l].astype(BF16), w_br_xa[l].astype(BF16),
                       w_out[l].astype(BF16), norm_ffn[l], wrh, wrl, br)
        x = _moe(x, hp, moe_w_gate[l].astype(BF16), moe_w_up[l].astype(BF16),
                 moe_w_down[l].astype(BF16), final_norm, final_norm=(l == depth - 1))
    return x
```

```python
import functools
import math

import numpy as np
import jax
import jax.numpy as jnp
from jax import lax
from jax.experimental import pallas as pl
from jax.experimental.pallas import tpu as pltpu

F32 = jnp.float32
BF16 = jnp.bfloat16

D_MODEL = 1024
GRID_W = 64
RMS_EPS = 1e-6
NA_HEADS = 8
NA_HEAD_DIM = 64
NA_WIDTH = NA_HEADS * NA_HEAD_DIM
NA_WIN_R = 8
NA_WIN_C = 16
NA_KV_ROWS = 24
NA_Q_ROWS = 8
NA_ROWS_PER_STEP = 4
S5_GROUPS = 16
S5_GROUP_CH = 16
S5_WIDTH = S5_GROUPS * S5_GROUP_CH
S5_STATE = 64
S5_CHUNK = 16
S5_BPAD = 8
XA_HEADS = 4
XA_HEAD_DIM = 64
XA_WIDTH = XA_HEADS * XA_HEAD_DIM
N_BRANCHES = 3
MOE_GROUPS = 4
MOE_EPG = 4
MOE_EXPERTS = MOE_GROUPS * MOE_EPG
MOE_FF = 256
MOE_ECHUNK = 4
ROUTER_LANES = 128
TOKEN_REC_ROWS = 8
MOE_TILE = 512
NEG_BIG = -1e30

VMEM_LIMIT = 52 * 1024 * 1024
TOKEN_TILE = 512


def _cparams(sem):
    return pltpu.CompilerParams(dimension_semantics=sem, vmem_limit_bytes=VMEM_LIMIT)


def _lane_block_mask(rows, j):
    lane = lax.broadcasted_iota(jnp.int32, (rows, 128), 1)
    lo = S5_GROUP_CH * (j % 8)
    return (lane >= lo) & (lane < lo + S5_GROUP_CH)


def _time_to_group_major(src_ref, rows):
    dest = [[None, None] for _ in range(S5_GROUPS)]
    for s in range(S5_CHUNK):
        halves = tuple(src_ref[hf, pl.ds(s, rows, stride=S5_CHUNK), :] for hf in range(2))
        m = _lane_block_mask(rows, s)
        for g in range(S5_GROUPS):
            shift = (S5_GROUP_CH * ((s % 8) - (g % 8))) % 128
            r = pltpu.roll(halves[g // 8], shift, axis=1) if shift else halves[g // 8]
            prev = dest[g][s // 8]
            dest[g][s // 8] = jnp.where(m, r, 0.0 if prev is None else prev)
    return [jnp.concatenate(d, axis=1) for d in dest]


def _group_to_time_major(src_ref, dst_ref, rows):
    for t in range(S5_CHUNK):
        out = [None, None]
        for g in range(S5_GROUPS):
            src = src_ref[g, :, 128 * (t // 8):128 * (t // 8 + 1)]
            shift = (S5_GROUP_CH * ((g % 8) - (t % 8))) % 128
            r = pltpu.roll(src, shift, axis=1) if shift else src
            prev = out[g // 8]
            out[g // 8] = jnp.where(_lane_block_mask(rows, g), r, 0.0 if prev is None else prev)
        for hf in range(2):
            dst_ref[hf, pl.ds(t, rows, stride=S5_CHUNK), :] = out[hf]


def _inproj_kernel(x_ref, g_ref, w_ref, bg_ref, q_ref, k_ref, v_ref, u_ref, ug_ref, qx_ref, gate_ref, uh_ref):
    x = x_ref[...]
    ms = jnp.mean(x * x, axis=-1, keepdims=True)
    h = (x * lax.rsqrt(ms + RMS_EPS) * g_ref[...]).astype(BF16)

    def proj(a, b):
        return jnp.dot(h, w_ref[:, a:b], preferred_element_type=F32)

    c0 = NA_WIDTH
    q_ref[...] = (proj(0, c0) * (NA_HEAD_DIM ** -0.5)).astype(BF16)
    k_ref[...] = proj(c0, 2 * c0).astype(BF16)
    v_ref[...] = proj(2 * c0, 3 * c0).astype(BF16)
    c1 = 3 * c0
    u = proj(c1, c1 + S5_WIDTH)
    u_ref[...] = u
    uh_ref[0] = u[:, :128]
    uh_ref[1] = u[:, 128:]
    for g, ug in enumerate(_time_to_group_major(uh_ref, ug_ref.shape[1])):
        ug_ref[g] = ug.astype(BF16)
    c2 = c1 + S5_WIDTH
    qx_ref[...] = (proj(c2, c2 + XA_WIDTH) * (XA_HEAD_DIM ** -0.5)).astype(BF16)
    c3 = c2 + XA_WIDTH
    for j in range(N_BRANCHES):
        z = proj(c3 + D_MODEL * j, c3 + D_MODEL * (j + 1)) + bg_ref[:, D_MODEL * j:D_MODEL * (j + 1)]
        gate_ref[:, D_MODEL * j:D_MODEL * (j + 1)] = jax.nn.sigmoid(z).astype(BF16)


def _layer_spec(a, l, n_grid):
    zeros = (0,) * (a.ndim - 1)
    if n_grid == 1:
        index_map = lambda i: (l,) + zeros
    elif n_grid == 2:
        index_map = lambda b, i: (l,) + zeros
    else:
        raise ValueError(n_grid)
    return pl.BlockSpec((None,) + a.shape[1:], index_map, pipeline_mode=pl.Buffered(1))


def _inproj(x, g, w_bf, bg, l):
    B, S, D = x.shape
    tm = min(TOKEN_TILE, S)

    def tok(width):
        return pl.BlockSpec((None, tm, width), lambda b, i: (b, i, 0))

    return pl.pallas_call(
        _inproj_kernel,
        grid=(B, S // tm),
        in_specs=[tok(D), _layer_spec(g, l, 2), _layer_spec(w_bf, l, 2), _layer_spec(bg, l, 2)],
        out_specs=[tok(NA_WIDTH), tok(NA_WIDTH), tok(NA_WIDTH), tok(S5_WIDTH),
                   pl.BlockSpec((S5_GROUPS, None, tm // S5_CHUNK, S5_CHUNK * S5_GROUP_CH), lambda b, i: (0, b, i, 0)),
                   tok(XA_WIDTH), tok(N_BRANCHES * D)],
        out_shape=[
            jax.ShapeDtypeStruct((B, S, NA_WIDTH), BF16),
            jax.ShapeDtypeStruct((B, S, NA_WIDTH), BF16),
            jax.ShapeDtypeStruct((B, S, NA_WIDTH), BF16),
            jax.ShapeDtypeStruct((B, S, S5_WIDTH), F32),
            jax.ShapeDtypeStruct((S5_GROUPS, B, S // S5_CHUNK, S5_CHUNK * S5_GROUP_CH), BF16),
            jax.ShapeDtypeStruct((B, S, XA_WIDTH), BF16),
            jax.ShapeDtypeStruct((B, S, N_BRANCHES * D), BF16),
        ],
        scratch_shapes=[pltpu.VMEM((2, tm, 128), F32)],
        compiler_params=_cparams(("parallel", "parallel")),
        name="inproj",
    )(x, g, w_bf, bg)


def _na_bias_table(rpb):
    cols = np.arange(GRID_W)
    col_start = np.clip(cols - NA_WIN_C // 2, 0, GRID_W - NA_WIN_C)
    cj = np.arange(GRID_W)[None, :]
    valid = (cj >= col_start[:, None]) & (cj < col_start[:, None] + NA_WIN_C)
    col_idx = np.clip(cj - cols[:, None] + (NA_WIN_C - 1), 0, 2 * NA_WIN_C - 2)
    d = np.arange(NA_WIN_R)[:, None]
    ri = np.arange(NA_WIN_R)[None, :]
    row_idx = ri - d + (NA_WIN_R - 1)
    row_sel = (row_idx[:, :, None] == np.arange(2 * NA_WIN_R - 1)).astype(np.float32)
    col_sel = ((col_idx[:, :, None] == np.arange(2 * NA_WIN_C - 1)) & valid[:, :, None]).astype(np.float32)
    t = jnp.einsum('hrk,cjk->hrcj', rpb.astype(F32), col_sel, precision=lax.Precision.HIGHEST)
    t = jnp.einsum('hrcj,dir->dhcij', t, row_sel, precision=lax.Precision.HIGHEST)
    t = t + jnp.where(valid, 0.0, NEG_BIG)[None, None, :, None, :].astype(F32)
    return t.reshape(NA_WIN_R, NA_HEADS, GRID_W, NA_WIN_R * GRID_W)


def _na_kernel(q_ref, k_ref, v_ref, bias_ref, o_ref, *, rows_total):
    r0 = pl.program_id(1) * NA_Q_ROWS
    kstart = jnp.clip(r0 - NA_Q_ROWS, 0, rows_total - NA_KV_ROWS)
    lane = lax.broadcasted_iota(jnp.int32, (GRID_W, 128), 1)
    first_head = lane < NA_HEAD_DIM
    win = NA_WIN_R * GRID_W

    def body(it, carry):
        rows = []
        for sub in range(NA_ROWS_PER_STEP):
            i = it * NA_ROWS_PER_STEP + sub
            r = r0 + i
            rs = jnp.clip(r - NA_WIN_R // 2, 0, rows_total - NA_WIN_R)
            rows.append((pl.multiple_of(i * GRID_W, GRID_W), pl.multiple_of((rs - kstart) * GRID_W, GRID_W), r - rs))
        scores = []
        for qoff, koff, didx in rows:
            for h in range(NA_HEADS):
                cs = slice(128 * (h // 2), 128 * (h // 2 + 1))
                q2 = q_ref[pl.ds(qoff, GRID_W), cs]
                qm = jnp.where(first_head if h % 2 == 0 else jnp.logical_not(first_head), q2, jnp.zeros_like(q2))
                s = lax.dot_general(qm, k_ref[pl.ds(koff, win), cs], (((1,), (1,)), ((), ())),
                                    preferred_element_type=F32)
                scores.append(s + bias_ref[didx, h])
        probs = []
        for s in scores:
            m = jnp.max(s, axis=-1, keepdims=True)
            e = jnp.exp(s - m)
            probs.append((e.astype(BF16), jnp.sum(e, axis=-1, keepdims=True)))
        for n, (qoff, koff, didx) in enumerate(rows):
            outs = []
            for h in range(NA_HEADS):
                e, l = probs[n * NA_HEADS + h]
                cs = slice(128 * (h // 2), 128 * (h // 2 + 1))
                outs.append(jnp.dot(e, v_ref[pl.ds(koff, win), cs], preferred_element_type=F32) / l)
            for p in range(NA_HEADS // 2):
                o_ref[pl.ds(qoff, GRID_W), 128 * p:128 * (p + 1)] = jnp.where(
                    first_head, outs[2 * p], outs[2 * p + 1]).astype(BF16)
        return carry

    lax.fori_loop(0, NA_Q_ROWS // NA_ROWS_PER_STEP, body, 0)


def _na(q, k, v, bias, l):
    B, S, _ = q.shape
    rows = S // GRID_W
    assert rows >= NA_KV_ROWS and rows % NA_Q_ROWS == 0
    qtok = NA_Q_ROWS * GRID_W
    kvtok = NA_KV_ROWS * GRID_W

    def kv_map(b, rb):
        return (b, jnp.clip(rb * NA_Q_ROWS - NA_Q_ROWS, 0, rows - NA_KV_ROWS) * GRID_W, 0)

    kv_spec = pl.BlockSpec((None, pl.Element(kvtok), pl.Element(NA_WIDTH)), kv_map)
    return pl.pallas_call(
        functools.partial(_na_kernel, rows_total=rows),
        grid=(B, rows // NA_Q_ROWS),
        in_specs=[
            pl.BlockSpec((None, qtok, NA_WIDTH), lambda b, rb: (b, rb, 0)),
            kv_spec,
            kv_spec,
            _layer_spec(bias, l, 2),
        ],
        out_specs=pl.BlockSpec((None, qtok, NA_WIDTH), lambda b, rb: (b, rb, 0)),
        out_shape=jax.ShapeDtypeStruct((B, S, NA_WIDTH), BF16),
        compiler_params=_cparams(("parallel", "parallel")),
        name="na",
    )(q, k, v, bias)


def _s5_tables(lam_re, lam_im, log_dt, b_re, b_im, c_re, c_im):
    L = S5_CHUNK
    lr = jnp.minimum(lam_re.astype(F32), -1e-4)
    li = lam_im.astype(F32)
    dt = jnp.exp(log_dt.astype(F32))[:, :, None]
    mag = jnp.exp(lr * dt)
    ar = mag * jnp.cos(li * dt)
    ai = mag * jnp.sin(li * dt)
    den = lr * lr + li * li
    fr = ((ar - 1.0) * lr + ai * li) / den
    fi = (ai * lr - (ar - 1.0) * li) / den
    br = b_re.astype(F32)
    bi = b_im.astype(F32)
    bbr = fr[..., None] * br - fi[..., None] * bi
    bbi = fr[..., None] * bi + fi[..., None] * br
    cr = c_re.astype(F32)
    ci = c_im.astype(F32)
    j = jnp.arange(L + 1, dtype=F32)[:, None, None, None]
    pm = jnp.exp(j * (lr * dt)[None])
    pr = pm * jnp.cos(j * (li * dt)[None])
    pi_ = pm * jnp.sin(j * (li * dt)[None])
    abr = pr[..., None] * bbr[None] - pi_[..., None] * bbi[None]
    abi = pr[..., None] * bbi[None] + pi_[..., None] * bbr[None]
    kern = jnp.einsum('dgcp,jdgpe->jdgce', cr, abr) - jnp.einsum('dgcp,jdgpe->jdgce', ci, abi)
    s_idx = np.arange(L)[:, None]
    t_idx = np.arange(L)[None, :]
    lag_f = np.clip(t_idx - s_idx, 0, L)
    lag_b = np.clip(s_idx - t_idx, 0, L)
    kf = jnp.where((t_idx >= s_idx)[:, :, None, None, None], kern[lag_f, 0], 0.0)
    kb = jnp.where((s_idx >= t_idx)[:, :, None, None, None], kern[lag_b, 1], 0.0)
    tmat = jnp.transpose(kf + kb, (2, 0, 4, 1, 3)).reshape(S5_GROUPS, L * S5_GROUP_CH, L * S5_GROUP_CH)
    ef_r = abr[L - 1 - np.arange(L), 0]
    ef_i = abi[L - 1 - np.arange(L), 0]
    eb_r = abr[np.arange(L), 1]
    eb_i = abi[np.arange(L), 1]

    def to_rows(a):
        return jnp.transpose(a, (1, 0, 3, 2)).reshape(S5_GROUPS, L * S5_GROUP_CH, S5_STATE)

    we = jnp.concatenate([to_rows(ef_r), to_rows(eb_r), to_rows(ef_i), to_rows(eb_i)], axis=-1)
    def readout(d, powers):
        p_r = pr[powers, d]
        p_i = pi_[powers, d]
        m_re = cr[d][None] * p_r[:, :, None, :] - ci[d][None] * p_i[:, :, None, :]
        m_im = cr[d][None] * p_i[:, :, None, :] + ci[d][None] * p_r[:, :, None, :]
        to_cols = lambda a: jnp.transpose(a, (1, 3, 0, 2)).reshape(S5_GROUPS, S5_STATE, L * S5_GROUP_CH)
        return to_cols(m_re), to_cols(-m_im)

    yf_re, yf_im = readout(0, np.arange(L) + 1)
    yb_re, yb_im = readout(1, L - np.arange(L))
    wy = jnp.concatenate([yf_re, yb_re, yf_im, yb_im], axis=1)
    al = jnp.stack([jnp.concatenate([pr[L, 0], pr[L, 1]], axis=-1),
                    jnp.concatenate([pi_[L, 0], pi_[L, 1]], axis=-1)], axis=1)
    return tmat.astype(BF16), we.astype(BF16), wy.astype(BF16), al


def _s5_kernel(u_ref, t_ref, we_ref, wy_ref, al_ref, y_ref, e_ref, xp_ref, *, n_chunks, batch):
    lc = S5_CHUNK * S5_GROUP_CH
    u = u_ref[...].reshape(batch * n_chunks, lc)
    y_ref[...] = jnp.dot(u, t_ref[...], preferred_element_type=F32).reshape(batch, n_chunks, lc)
    e = jnp.dot(u, we_ref[...], preferred_element_type=F32)
    e_ref[...] = jnp.zeros(e_ref.shape, F32)
    half = 2 * S5_STATE
    for b in range(batch):
        for hf in range(2):
            e_ref[hf, pl.ds(b, n_chunks, stride=S5_BPAD), :] = e[b * n_chunks:(b + 1) * n_chunks,
                                                                 hf * half:(hf + 1) * half]
    a_re = al_ref[0:1, :]
    a_im = al_ref[1:2, :]
    lane = lax.broadcasted_iota(jnp.int32, (S5_BPAD, half), 1)
    is_fwd = lane < S5_STATE

    def body(j, carry):
        xr, xi = carry
        rf = pl.multiple_of(j * S5_BPAD, S5_BPAD)
        rb = pl.multiple_of((n_chunks - 1 - j) * S5_BPAD, S5_BPAD)
        xp_ref[pl.ds(rf, S5_BPAD), 0:S5_STATE] = xr[:, 0:S5_STATE]
        xp_ref[pl.ds(rb, S5_BPAD), S5_STATE:half] = xr[:, S5_STATE:half]
        xp_ref[pl.ds(rf, S5_BPAD), half:half + S5_STATE] = xi[:, 0:S5_STATE]
        xp_ref[pl.ds(rb, S5_BPAD), half + S5_STATE:2 * half] = xi[:, S5_STATE:half]
        er = jnp.where(is_fwd, e_ref[0, pl.ds(rf, S5_BPAD), :], e_ref[0, pl.ds(rb, S5_BPAD), :])
        ei = jnp.where(is_fwd, e_ref[1, pl.ds(rf, S5_BPAD), :], e_ref[1, pl.ds(rb, S5_BPAD), :])
        nxr = a_re * xr - a_im * xi + er
        nxi = a_re * xi + a_im * xr + ei
        return nxr, nxi

    zero = jnp.zeros((S5_BPAD, half), F32)
    lax.fori_loop(0, n_chunks, body, (zero, zero))
    yi = jnp.dot(xp_ref[...].astype(BF16), wy_ref[...], preferred_element_type=F32)
    e_ref[0] = yi[:, :half]
    e_ref[1] = yi[:, half:]
    for b in range(batch):
        y_ref[b] += jnp.concatenate(
            [e_ref[hf, pl.ds(b, n_chunks, stride=S5_BPAD), :] for hf in range(2)], axis=1)


def _s5(ug, tabs, l):
    tmat, we, wy, al = tabs
    G, B, n, lc = ug.shape
    assert B <= S5_BPAD and lc == 4 * S5_STATE
    rows = n * S5_BPAD
    grp = lambda shape: pl.BlockSpec((None,) + shape, lambda g: (g,) + (0,) * len(shape))
    tab = lambda shape: pl.BlockSpec((None, None) + shape, lambda g: (l, g) + (0,) * len(shape))
    return pl.pallas_call(
        functools.partial(_s5_kernel, n_chunks=n, batch=B),
        grid=(G,),
        in_specs=[grp((B, n, lc)), tab((lc, lc)), tab((lc, 4 * S5_STATE)), tab((4 * S5_STATE, lc)),
                  tab((2, 2 * S5_STATE))],
        out_specs=grp((B, n, lc)),
        out_shape=jax.ShapeDtypeStruct((G, B, n, lc), F32),
        scratch_shapes=[pltpu.VMEM((2, rows, 2 * S5_STATE), F32), pltpu.VMEM((rows, 4 * S5_STATE), F32)],
        compiler_params=_cparams(("parallel",)),
        name="s5",
    )(ug, tmat, we, wy, al)


def _memkv_kernel(m_ref, g_ref, w_ref, o_ref):
    x = m_ref[...]
    ms = jnp.mean(x * x, axis=-1, keepdims=True)
    h = (x * lax.rsqrt(ms + RMS_EPS) * g_ref[...]).astype(BF16)
    o_ref[...] = jnp.dot(h, w_ref[...], preferred_element_type=F32).astype(BF16)


def _memkv(mem, norm_mem, w_bf):
    B, M, D = mem.shape
    L = w_bf.shape[0]
    return pl.pallas_call(
        _memkv_kernel,
        grid=(L, B),
        in_specs=[pl.BlockSpec((None, M, D), lambda l, b: (b, 0, 0)),
                  pl.BlockSpec((None, 1, D), lambda l, b: (l, 0, 0)),
                  pl.BlockSpec((None, D, 2 * XA_WIDTH), lambda l, b: (l, 0, 0))],
        out_specs=pl.BlockSpec((None, None, M, 2 * XA_WIDTH), lambda l, b: (l, b, 0, 0)),
        out_shape=jax.ShapeDtypeStruct((L, B, M, 2 * XA_WIDTH), BF16),
        compiler_params=_cparams(("parallel", "parallel")),
        name="memkv",
    )(mem, norm_mem.reshape(L, 1, D), w_bf)


def _gelu_tanh(x):
    return 0.5 * x * (1.0 + jnp.tanh(math.sqrt(2.0 / math.pi) * (x + 0.044715 * (x * x * x))))


def _router(z):
    lane = lax.broadcasted_iota(jnp.int32, z.shape, 1).astype(F32)
    ninf = jnp.full_like(z, -jnp.inf)
    far = jnp.full_like(z, 1e9)
    cm = (lane >= MOE_EXPERTS) & (lane < MOE_EXPERTS + MOE_GROUPS)
    cmax = jnp.max(jnp.where(cm, z, ninf), axis=-1, keepdims=True)
    glane = jnp.min(jnp.where(cm & (z == cmax), lane, far), axis=-1, keepdims=True)
    psum = jnp.sum(jnp.where(cm, jnp.exp(jnp.where(cm, z, cmax) - cmax), 0.0), axis=-1, keepdims=True)
    p_grp = 1.0 / psum
    f0 = (glane - MOE_EXPERTS) * MOE_EPG
    fm = (lane >= f0) & (lane < f0 + MOE_EPG)
    v1 = jnp.max(jnp.where(fm, z, ninf), axis=-1, keepdims=True)
    i1 = jnp.min(jnp.where(fm & (z == v1), lane, far), axis=-1, keepdims=True)
    fm2 = fm & (lane != i1)
    v2 = jnp.max(jnp.where(fm2, z, ninf), axis=-1, keepdims=True)
    i2 = jnp.min(jnp.where(fm2 & (z == v2), lane, far), axis=-1, keepdims=True)
    t = jnp.exp(v2 - v1)
    w1 = p_grp / (1.0 + t)
    w2 = p_grp * t / (1.0 + t)
    comb = jnp.where(lane == i1, w1, 0.0) + jnp.where(lane == i2, w2, 0.0)
    return comb, glane - MOE_EXPERTS


def _merge_kernel(x_ref, ona_ref, ys5_ref, u_ref, qx_ref, gate_ref, mkv_ref,
                  d_ref, wglu_ref, wna_ref, ws5_ref, wxa_ref, wout_ref, nffn_ref, wrh_ref, wrl_ref, br_ref,
                  xo_ref, hp_ref, meta_ref, cnt_ref, mrg_ref, ys_ref):
    tm = x_ref.shape[0]
    _group_to_time_major(ys5_ref, ys_ref, tm // S5_CHUNK)
    ys = jnp.concatenate([ys_ref[0], ys_ref[1]], axis=1)
    y = _gelu_tanh(ys + d_ref[...] * u_ref[...])
    g = jnp.dot(y.astype(BF16), wglu_ref[...], preferred_element_type=F32)
    os5 = (g[:, :S5_WIDTH] * jax.nn.sigmoid(g[:, S5_WIDTH:])).astype(BF16)
    lane = lax.broadcasted_iota(jnp.int32, (tm, 128), 1)
    first_head = lane < XA_HEAD_DIM
    oxa = []
    for p in range(XA_HEADS // 2):
        cs = slice(128 * p, 128 * (p + 1))
        q2 = qx_ref[:, cs]
        k2 = mkv_ref[:, cs]
        v2 = mkv_ref[:, XA_WIDTH + 128 * p:XA_WIDTH + 128 * (p + 1)]
        outs = []
        for hh in range(2):
            qm = jnp.where(first_head if hh == 0 else jnp.logical_not(first_head), q2, jnp.zeros_like(q2))
            s = lax.dot_general(qm, k2, (((1,), (1,)), ((), ())), preferred_element_type=F32)
            m = jnp.max(s, axis=-1, keepdims=True)
            e = jnp.exp(s - m)
            l = jnp.sum(e, axis=-1, keepdims=True)
            outs.append(jnp.dot(e.astype(BF16), v2, preferred_element_type=F32) / l)
        oxa.append(jnp.where(first_head, outs[0], outs[1]).astype(BF16))
    oxa = jnp.concatenate(oxa, axis=-1)
    ona = ona_ref[...]
    cw = 256
    for c in range(D_MODEL // cw):
        cs = slice(cw * c, cw * (c + 1))
        m = gate_ref[:, cw * c:cw * (c + 1)].astype(F32) * jnp.dot(ona, wna_ref[:, cs], preferred_element_type=F32)
        m += gate_ref[:, D_MODEL + cw * c:D_MODEL + cw * (c + 1)].astype(F32) * jnp.dot(
            os5, ws5_ref[:, cs], preferred_element_type=F32)
        m += gate_ref[:, 2 * D_MODEL + cw * c:2 * D_MODEL + cw * (c + 1)].astype(F32) * jnp.dot(
            oxa, wxa_ref[:, cs], preferred_element_type=F32)
        mrg_ref[:, cs] = m.astype(BF16)
    xn = x_ref[...] + jnp.dot(mrg_ref[...], wout_ref[...], preferred_element_type=F32)
    xo_ref[...] = xn
    ms = jnp.mean(xn * xn, axis=-1, keepdims=True)
    h2 = xn * lax.rsqrt(ms + RMS_EPS) * nffn_ref[...]
    hi = h2.astype(BF16)
    hi32 = hi.astype(F32)
    lo = (h2 - hi32).astype(BF16)
    z = (jnp.dot(hi, wrh_ref[...], preferred_element_type=F32)
         + jnp.dot(lo, wrh_ref[...], preferred_element_type=F32)
         + jnp.dot(hi, wrl_ref[...], preferred_element_type=F32)) + br_ref[...]
    hw = D_MODEL // 2
    packed = pltpu.bitcast(hi32[:, :hw], jnp.uint32) | (pltpu.bitcast(hi32[:, hw:], jnp.uint32) >> 16)
    for j in range(hw // 128):
        hp_ref[pl.ds(j, tm, stride=TOKEN_REC_ROWS), :] = packed[:, 128 * j:128 * (j + 1)]
    comb, gid = _router(z)
    hp_ref[pl.ds(hw // 128, tm, stride=TOKEN_REC_ROWS), :] = pltpu.bitcast(comb, jnp.uint32)
    for j in range(hw // 128 + 1, TOKEN_REC_ROWS):
        hp_ref[pl.ds(j, tm, stride=TOKEN_REC_ROWS), :] = jnp.zeros((tm, 128), jnp.uint32)
    lanef = lane.astype(F32)
    onehot = lanef == gid
    row = lax.broadcasted_iota(jnp.int32, (tm, tm), 0)
    col = lax.broadcasted_iota(jnp.int32, (tm, tm), 1)
    tri = jnp.where(col <= row, 1.0, 0.0).astype(BF16)
    csum = jnp.dot(tri, jnp.where(onehot, 1.0, 0.0).astype(BF16), preferred_element_type=F32)
    rank = jnp.sum(jnp.where(onehot, csum, 0.0), axis=-1, keepdims=True) - 1.0
    meta = jnp.where(lane == 0, gid, jnp.where(lane == 1, rank, 0.0))
    meta_ref[...] = jnp.transpose(meta)[:8, :]
    cnt_ref[...] = jnp.broadcast_to(csum[tm - 1:tm, :], (8, 128))


def _merge(x, ona, ys5, u, qx, gates, mkv, params, l):
    B, S, D = x.shape
    tm = min(TOKEN_TILE, S)
    M = mkv.shape[2]
    n_i = S // tm

    def tok(width):
        return pl.BlockSpec((None, tm, width), lambda b, i: (b, i, 0))

    return pl.pallas_call(
        _merge_kernel,
        grid=(B, n_i),
        in_specs=[tok(D), tok(NA_WIDTH),
                  pl.BlockSpec((S5_GROUPS, None, tm // S5_CHUNK, S5_CHUNK * S5_GROUP_CH), lambda b, i: (0, b, i, 0)),
                  tok(S5_WIDTH), tok(XA_WIDTH), tok(N_BRANCHES * D),
                  pl.BlockSpec((None, None, M, 2 * XA_WIDTH), lambda b, i: (l, b, 0, 0))]
                 + [_layer_spec(p, l, 2) for p in params],
        out_specs=[tok(D), pl.BlockSpec((None, tm * TOKEN_REC_ROWS, 128), lambda b, i: (b, i, 0)),
                   pl.BlockSpec((None, None, 8, tm), lambda b, i: (b, i, 0, 0)),
                   pl.BlockSpec((None, None, 8, 128), lambda b, i: (b, i, 0, 0))],
        out_shape=[jax.ShapeDtypeStruct((B, S, D), F32),
                   jax.ShapeDtypeStruct((B, S * TOKEN_REC_ROWS, 128), jnp.uint32),
                   jax.ShapeDtypeStruct((B, n_i, 8, tm), F32),
                   jax.ShapeDtypeStruct((B, n_i, 8, 128), F32)],
        scratch_shapes=[pltpu.VMEM((tm, D), BF16), pltpu.VMEM((2, tm, 128), F32)],
        compiler_params=_cparams(("parallel", "parallel")),
        name="merge",
    )(x, ona, ys5, u, qx, gates, mkv, *params)


def _moe_plan(meta, cnt):
    B, n_i, _, tm = meta.shape
    n_tokens = B * n_i * tm
    gid = meta[:, :, 0, :].reshape(B * n_i, tm).astype(jnp.int32)
    rank = meta[:, :, 1, :].reshape(B * n_i, tm).astype(jnp.int32)
    tile_counts = cnt[:, :, 0, :MOE_GROUPS].reshape(B * n_i, MOE_GROUPS).astype(jnp.int32)
    counts = jnp.sum(tile_counts, axis=0)
    padded = ((counts + MOE_TILE - 1) // MOE_TILE) * MOE_TILE
    ends = jnp.cumsum(padded)
    tile_base = (ends - padded)[None, :] + jnp.cumsum(tile_counts, axis=0) - tile_counts
    pos = rank
    for g in range(MOE_GROUPS):
        pos = pos + jnp.where(gid == g, tile_base[:, g:g + 1], 0)
    pos = pos.reshape(n_tokens)
    n_tiles = n_tokens // MOE_TILE + MOE_GROUPS
    starts = jnp.arange(n_tiles, dtype=jnp.int32) * MOE_TILE
    tile_grp = jnp.minimum(jnp.sum(starts[:, None] >= ends[None, :], axis=1), MOE_GROUPS - 1).astype(jnp.int32)
    return pos.astype(jnp.int32), ends.astype(jnp.int32), padded.astype(jnp.int32), tile_grp


def _rec(ref, token, count=1):
    return ref.at[pl.ds(pl.multiple_of(token * TOKEN_REC_ROWS, TOKEN_REC_ROWS), count * TOKEN_REC_ROWS), :]


DMA_UNROLL = 8


def _issue_record_dmas(count, copy_of):
    def body(blk, c):
        for k in range(DMA_UNROLL):
            copy_of(blk * DMA_UNROLL + k).start(priority=k % 2)
        return c
    lax.fori_loop(0, count // DMA_UNROLL, body, 0)


def _permute_kernel(ends_ref, padded_ref, pos_ref, h_ref, hs_ref, zero_ref, sem):
    tm = h_ref.shape[0] // TOKEN_REC_ROWS

    @pl.when(pl.program_id(0) == 0)
    def _():
        zero_ref[...] = jnp.zeros(zero_ref.shape, zero_ref.dtype)
        n_rows = hs_ref.shape[0] // TOKEN_REC_ROWS
        for g in range(MOE_GROUPS):
            tail = ends_ref[MOE_GROUPS - 1] + g * MOE_TILE
            for cond, start in ((padded_ref[g] > 0, ends_ref[g] - MOE_TILE), (tail < n_rows, tail)):
                @pl.when(cond)
                def _():
                    cp = pltpu.make_async_copy(zero_ref, _rec(hs_ref, start, MOE_TILE), sem)
                    cp.start()
                    cp.wait()

    _issue_record_dmas(tm, lambda r: pltpu.make_async_copy(_rec(h_ref, r), _rec(hs_ref, pos_ref[0, 0, r]), sem))
    pltpu.make_async_copy(h_ref, _rec(hs_ref, 0, tm), sem).wait()


def _moe_permute(hp2, pos3, ends, padded, n_rows):
    T = hp2.shape[0] // TOKEN_REC_ROWS
    tm = min(TOKEN_TILE, T)
    return pl.pallas_call(
        _permute_kernel,
        grid_spec=pltpu.PrefetchScalarGridSpec(
            num_scalar_prefetch=2,
            grid=(T // tm,),
            in_specs=[pl.BlockSpec((1, 1, tm), lambda i, e, p: (i, 0, 0), memory_space=pltpu.SMEM),
                      pl.BlockSpec((tm * TOKEN_REC_ROWS, 128), lambda i, e, p: (i, 0))],
            out_specs=pl.BlockSpec(memory_space=pl.ANY),
            scratch_shapes=[pltpu.VMEM((MOE_TILE * TOKEN_REC_ROWS, 128), jnp.uint32), pltpu.SemaphoreType.DMA(())],
        ),
        out_shape=jax.ShapeDtypeStruct((n_rows * TOKEN_REC_ROWS, 128), jnp.uint32),
        compiler_params=_cparams(("arbitrary",)),
        name="moe_permute",
    )(ends, padded, pos3, hp2)


def _moe_kernel(grp_ref, h_ref, wg_ref, wu_ref, wd_ref, o_ref):
    hw = D_MODEL // 2

    def field(j):
        return h_ref[pl.ds(j, MOE_TILE, stride=TOKEN_REC_ROWS), :]

    w = jnp.concatenate([field(j) for j in range(hw // 128)], axis=1)
    h = jnp.concatenate([pltpu.bitcast(w & jnp.uint32(0xFFFF0000), F32).astype(BF16),
                         pltpu.bitcast(w << 16, F32).astype(BF16)], axis=1)
    comb = pltpu.bitcast(field(hw // 128), F32)
    lane = lax.broadcasted_iota(jnp.int32, comb.shape, 1)
    first = grp_ref[pl.program_id(0)] * MOE_EPG
    acts = []
    for ee in range(MOE_EPG):
        c = jnp.sum(jnp.where(lane == first + ee, comb, 0.0), axis=-1, keepdims=True)
        g = jnp.dot(h, wg_ref[ee], preferred_element_type=F32)
        u = jnp.dot(h, wu_ref[ee], preferred_element_type=F32)
        acts.append((g * jax.nn.sigmoid(g) * u * c).astype(BF16))
    a = jnp.concatenate(acts, axis=-1)
    out = jnp.dot(a, wd_ref[...].reshape(MOE_EPG * MOE_FF, D_MODEL), preferred_element_type=F32)
    for j in range(TOKEN_REC_ROWS):
        o_ref[pl.ds(j, MOE_TILE, stride=TOKEN_REC_ROWS), :] = out[:, 128 * j:128 * (j + 1)]


def _moe_experts(hs, tile_grp, wg, wu, wd, l):
    n_rows = hs.shape[0] // TOKEN_REC_ROWS
    n_tiles = n_rows // MOE_TILE
    rec_tile = pl.BlockSpec((MOE_TILE * TOKEN_REC_ROWS, 128), lambda i, g: (i, 0))
    return pl.pallas_call(
        _moe_kernel,
        grid_spec=pltpu.PrefetchScalarGridSpec(
            num_scalar_prefetch=1,
            grid=(n_tiles,),
            in_specs=[rec_tile,
                      pl.BlockSpec((None, MOE_EPG, D_MODEL, MOE_FF), lambda i, g: (l, g[i], 0, 0)),
                      pl.BlockSpec((None, MOE_EPG, D_MODEL, MOE_FF), lambda i, g: (l, g[i], 0, 0)),
                      pl.BlockSpec((None, MOE_EPG, MOE_FF, D_MODEL), lambda i, g: (l, g[i], 0, 0))],
            out_specs=rec_tile,
        ),
        out_shape=jax.ShapeDtypeStruct((n_rows * TOKEN_REC_ROWS, 128), F32),
        compiler_params=_cparams(("arbitrary",)),
        name="moe_experts",
    )(tile_grp, hs, wg, wu, wd)


def _unpermute_kernel(pos_ref, x_ref, ys_ref, fn_ref, o_ref, buf_ref, sem, *, final_norm):
    tm = x_ref.shape[0]

    _issue_record_dmas(tm, lambda r: pltpu.make_async_copy(_rec(ys_ref, pos_ref[0, 0, r]), _rec(buf_ref, r), sem))
    pltpu.make_async_copy(_rec(ys_ref, 0, tm), buf_ref, sem).wait()
    moe = jnp.concatenate([buf_ref[pl.ds(j, tm, stride=TOKEN_REC_ROWS), :] for j in range(TOKEN_REC_ROWS)], axis=1)
    xn = x_ref[...] + moe
    if final_norm:
        ms = jnp.mean(xn * xn, axis=-1, keepdims=True)
        xn = xn * lax.rsqrt(ms + RMS_EPS) * fn_ref[...]
    o_ref[...] = xn


def _moe_unpermute(x2, ys, pos3, fnorm, final_norm):
    T, D = x2.shape
    tm = min(TOKEN_TILE, T)
    return pl.pallas_call(
        functools.partial(_unpermute_kernel, final_norm=final_norm),
        grid=(T // tm,),
        in_specs=[pl.BlockSpec((1, 1, tm), lambda i: (i, 0, 0), memory_space=pltpu.SMEM),
                  pl.BlockSpec((tm, D), lambda i: (i, 0)),
                  pl.BlockSpec(memory_space=pl.ANY),
                  pl.BlockSpec((1, D), lambda i: (0, 0))],
        out_specs=pl.BlockSpec((tm, D), lambda i: (i, 0)),
        out_shape=jax.ShapeDtypeStruct((T, D), F32),
        scratch_shapes=[pltpu.VMEM((tm * TOKEN_REC_ROWS, 128), F32), pltpu.SemaphoreType.DMA(())],
        compiler_params=_cparams(("arbitrary",)),
        name="moe_unpermute",
    )(pos3, x2, ys, fnorm.reshape(1, D))


def _moe(x, hp, meta, cnt, wg, wu, wd, l, fnorm, final_norm):
    B, S, D = x.shape
    T = B * S
    tm = min(TOKEN_TILE, T)
    pos, ends, padded, tile_grp = _moe_plan(meta, cnt)
    pos3 = pos.reshape(T // tm, 1, tm)
    n_rows = T + MOE_GROUPS * MOE_TILE
    hs = _moe_permute(hp.reshape(T * TOKEN_REC_ROWS, 128), pos3, ends, padded, n_rows)
    ys = _moe_experts(hs, tile_grp, wg, wu, wd, l)
    return _moe_unpermute(x.reshape(T, D), ys, pos3, fnorm, final_norm).reshape(B, S, D)


def kernel(x, mem, norm_mix, norm_ffn, norm_mem, w_in, b_gate, na_rpb, s5_lam_re, s5_lam_im, s5_log_dt, s5_b_re, s5_b_im, s5_c_re, s5_c_im, s5_d, s5_w_glu, w_mem_kv, w_br_na, w_br_s5, w_br_xa, w_out, moe_w_coarse, moe_b_coarse, moe_w_fine, moe_b_fine, moe_w_gate, moe_w_up, moe_w_down, final_norm):
    depth, D = w_in.shape[0], w_in.shape[1]
    mkv_all = _memkv(mem, norm_mem, w_mem_kv.astype(BF16))
    w_in_bf = w_in.astype(BF16)
    norm_mix3 = norm_mix.reshape(depth, 1, D)
    b_gate3 = b_gate.reshape(depth, 1, N_BRANCHES * D)
    na_bias = jax.vmap(_na_bias_table)(na_rpb)
    s5_tabs = jax.vmap(_s5_tables)(s5_lam_re, s5_lam_im, s5_log_dt, s5_b_re, s5_b_im, s5_c_re, s5_c_im)
    wr = jnp.concatenate([moe_w_fine, moe_w_coarse], axis=2).astype(F32)
    wr = jnp.pad(wr, ((0, 0), (0, 0), (0, ROUTER_LANES - wr.shape[2])))
    wrh = wr.astype(BF16)
    wrl = (wr - wrh.astype(F32)).astype(BF16)
    br = jnp.pad(jnp.concatenate([moe_b_fine, moe_b_coarse], axis=1).astype(F32),
                 ((0, 0), (0, ROUTER_LANES - MOE_EXPERTS - MOE_GROUPS))).reshape(depth, 1, ROUTER_LANES)
    merge_params = (s5_d.reshape(depth, 1, S5_WIDTH), s5_w_glu.astype(BF16), w_br_na.astype(BF16),
                    w_br_s5.astype(BF16), w_br_xa.astype(BF16), w_out.astype(BF16),
                    norm_ffn.reshape(depth, 1, D), wrh, wrl, br)
    wg, wu, wd = moe_w_gate.astype(BF16), moe_w_up.astype(BF16), moe_w_down.astype(BF16)
    for l in range(depth):
        q, k, v, u, ug, qx, gates = _inproj(x, norm_mix3, w_in_bf, b_gate3, l)
        ona = _na(q, k, v, na_bias, l)
        ys5 = _s5(ug, s5_tabs, l)
        x, hp, meta, cnt = _merge(x, ona, ys5, u, qx, gates, mkv_all, merge_params, l)
        x = _moe(x, hp, meta, cnt, wg, wu, wd, l, final_norm, final_norm=(l == depth - 1))
    return x
```

```python
import functools
import math

import numpy as np
import jax
import jax.numpy as jnp
from jax import lax
from jax.experimental import pallas as pl
from jax.experimental.pallas import tpu as pltpu

F32 = jnp.float32
BF16 = jnp.bfloat16

D_MODEL = 1024
GRID_W = 64
RMS_EPS = 1e-6
NA_HEADS = 8
NA_HEAD_DIM = 64
NA_WIDTH = NA_HEADS * NA_HEAD_DIM
NA_WIN_R = 8
NA_WIN_C = 16
NA_KV_ROWS = 24
NA_Q_ROWS = 8
NA_ROWS_PER_STEP = 4
S5_GROUPS = 16
S5_GROUP_CH = 16
S5_WIDTH = S5_GROUPS * S5_GROUP_CH
S5_STATE = 64
S5_CHUNK = 16
S5_BPAD = 8
XA_HEADS = 4
XA_HEAD_DIM = 64
XA_WIDTH = XA_HEADS * XA_HEAD_DIM
N_BRANCHES = 3
MOE_GROUPS = 4
MOE_EPG = 4
MOE_EXPERTS = MOE_GROUPS * MOE_EPG
MOE_FF = 256
MOE_ECHUNK = 4
ROUTER_LANES = 128
TOKEN_REC_ROWS = 8
MOE_TILE = 512
NEG_BIG = -1e30

VMEM_LIMIT = 52 * 1024 * 1024
TOKEN_TILE = 512


def _cparams(sem):
    return pltpu.CompilerParams(dimension_semantics=sem, vmem_limit_bytes=VMEM_LIMIT)


def _lane_block_mask(rows, j):
    lane = lax.broadcasted_iota(jnp.int32, (rows, 128), 1)
    lo = S5_GROUP_CH * (j % 8)
    return (lane >= lo) & (lane < lo + S5_GROUP_CH)


def _time_to_group_major(src_ref, rows):
    dest = [[None, None] for _ in range(S5_GROUPS)]
    for s in range(S5_CHUNK):
        halves = tuple(src_ref[hf, pl.ds(s, rows, stride=S5_CHUNK), :] for hf in range(2))
        m = _lane_block_mask(rows, s)
        for g in range(S5_GROUPS):
            shift = (S5_GROUP_CH * ((s % 8) - (g % 8))) % 128
            r = pltpu.roll(halves[g // 8], shift, axis=1) if shift else halves[g // 8]
            prev = dest[g][s // 8]
            dest[g][s // 8] = jnp.where(m, r, 0.0 if prev is None else prev)
    return [jnp.concatenate(d, axis=1) for d in dest]


def _group_to_time_major(src_ref, dst_ref, rows):
    for t in range(S5_CHUNK):
        out = [None, None]
        for g in range(S5_GROUPS):
            src = src_ref[g, :, 128 * (t // 8):128 * (t // 8 + 1)]
            shift = (S5_GROUP_CH * ((g % 8) - (t % 8))) % 128
            r = pltpu.roll(src, shift, axis=1) if shift else src
            prev = out[g // 8]
            out[g // 8] = jnp.where(_lane_block_mask(rows, g), r, 0.0 if prev is None else prev)
        for hf in range(2):
            dst_ref[hf, pl.ds(t, rows, stride=S5_CHUNK), :] = out[hf]


def _inproj_kernel(*refs, with_moe):
    if with_moe:
        (pos_ref, pos_next_ref, x_ref, ys_ref, g_ref, w_ref, bg_ref,
         q_ref, k_ref, v_ref, u_ref, ug_ref, qx_ref, gate_ref, xo_ref, uh_ref, buf_ref, sem) = refs
        tm = x_ref.shape[0]
        step = pl.program_id(0) * pl.num_programs(1) + pl.program_id(1)
        n_steps = pl.num_programs(0) * pl.num_programs(1)
        slot = lax.rem(step, 2)

        def fetch(p_ref, s):
            _issue_record_dmas(tm, lambda r: pltpu.make_async_copy(
                _rec(ys_ref, p_ref[0, 0, r]), _rec(buf_ref.at[s], r), sem.at[s]))

        @pl.when(step == 0)
        def _():
            fetch(pos_ref, 0)

        @pl.when(step + 1 < n_steps)
        def _():
            fetch(pos_next_ref, 1 - slot)

        pltpu.make_async_copy(_rec(ys_ref, 0, tm), buf_ref.at[slot], sem.at[slot]).wait()
        moe = jnp.concatenate([buf_ref[slot, pl.ds(j, tm, stride=TOKEN_REC_ROWS), :]
                               for j in range(TOKEN_REC_ROWS)], axis=1)
        x = x_ref[...] + moe
        xo_ref[...] = x
    else:
        x_ref, g_ref, w_ref, bg_ref, q_ref, k_ref, v_ref, u_ref, ug_ref, qx_ref, gate_ref, uh_ref = refs
        x = x_ref[...]
    ms = jnp.mean(x * x, axis=-1, keepdims=True)
    h = (x * lax.rsqrt(ms + RMS_EPS) * g_ref[...]).astype(BF16)

    def proj(a, b):
        return jnp.dot(h, w_ref[:, a:b], preferred_element_type=F32)

    c0 = NA_WIDTH
    q_ref[...] = (proj(0, c0) * (NA_HEAD_DIM ** -0.5)).astype(BF16)
    k_ref[...] = proj(c0, 2 * c0).astype(BF16)
    v_ref[...] = proj(2 * c0, 3 * c0).astype(BF16)
    c1 = 3 * c0
    u = proj(c1, c1 + S5_WIDTH)
    u_ref[...] = u
    uh_ref[0] = u[:, :128]
    uh_ref[1] = u[:, 128:]
    for g, ug in enumerate(_time_to_group_major(uh_ref, ug_ref.shape[1])):
        ug_ref[g] = ug.astype(BF16)
    c2 = c1 + S5_WIDTH
    qx_ref[...] = (proj(c2, c2 + XA_WIDTH) * (XA_HEAD_DIM ** -0.5)).astype(BF16)
    c3 = c2 + XA_WIDTH
    for j in range(N_BRANCHES):
        z = proj(c3 + D_MODEL * j, c3 + D_MODEL * (j + 1)) + bg_ref[:, D_MODEL * j:D_MODEL * (j + 1)]
        gate_ref[:, D_MODEL * j:D_MODEL * (j + 1)] = jax.nn.sigmoid(z).astype(BF16)


def _layer_spec(a, l, n_grid):
    zeros = (0,) * (a.ndim - 1)
    if n_grid == 1:
        index_map = lambda i: (l,) + zeros
    elif n_grid == 2:
        index_map = lambda b, i: (l,) + zeros
    else:
        raise ValueError(n_grid)
    return pl.BlockSpec((None,) + a.shape[1:], index_map, pipeline_mode=pl.Buffered(1))


def _inproj(x, g, w_bf, bg, l, moe=None):
    B, S, D = x.shape
    tm = min(TOKEN_TILE, S)
    n_i = S // tm

    def tok(width):
        return pl.BlockSpec((None, tm, width), lambda b, i: (b, i, 0))

    in_specs = [tok(D), _layer_spec(g, l, 2), _layer_spec(w_bf, l, 2), _layer_spec(bg, l, 2)]
    operands = [x, g, w_bf, bg]
    out_specs = [tok(NA_WIDTH), tok(NA_WIDTH), tok(NA_WIDTH), tok(S5_WIDTH),
                 pl.BlockSpec((S5_GROUPS, None, tm // S5_CHUNK, S5_CHUNK * S5_GROUP_CH), lambda b, i: (0, b, i, 0)),
                 tok(XA_WIDTH), tok(N_BRANCHES * D)]
    out_shape = [
        jax.ShapeDtypeStruct((B, S, NA_WIDTH), BF16),
        jax.ShapeDtypeStruct((B, S, NA_WIDTH), BF16),
        jax.ShapeDtypeStruct((B, S, NA_WIDTH), BF16),
        jax.ShapeDtypeStruct((B, S, S5_WIDTH), F32),
        jax.ShapeDtypeStruct((S5_GROUPS, B, S // S5_CHUNK, S5_CHUNK * S5_GROUP_CH), BF16),
        jax.ShapeDtypeStruct((B, S, XA_WIDTH), BF16),
        jax.ShapeDtypeStruct((B, S, N_BRANCHES * D), BF16),
    ]
    scratch = [pltpu.VMEM((2, tm, 128), F32)]
    if moe is not None:
        ys, pos3 = moe
        last = B * n_i - 1
        pos_spec = lambda nxt: pl.BlockSpec(
            (1, 1, tm), lambda b, i: (jnp.minimum(b * n_i + i + nxt, last), 0, 0), memory_space=pltpu.SMEM)
        in_specs = [pos_spec(0), pos_spec(1), in_specs[0], pl.BlockSpec(memory_space=pl.ANY)] + in_specs[1:]
        operands = [pos3, pos3, x, ys] + operands[1:]
        out_specs.append(tok(D))
        out_shape.append(jax.ShapeDtypeStruct((B, S, D), F32))
        scratch += [pltpu.VMEM((2, tm * TOKEN_REC_ROWS, 128), F32), pltpu.SemaphoreType.DMA((2,))]
    return pl.pallas_call(
        functools.partial(_inproj_kernel, with_moe=moe is not None),
        grid=(B, n_i),
        in_specs=in_specs,
        out_specs=out_specs,
        out_shape=out_shape,
        scratch_shapes=scratch,
        compiler_params=_cparams(("arbitrary", "arbitrary") if moe is not None else ("parallel", "parallel")),
        name="inproj",
    )(*operands)


def _na_bias_table(rpb):
    cols = np.arange(GRID_W)
    col_start = np.clip(cols - NA_WIN_C // 2, 0, GRID_W - NA_WIN_C)
    cj = np.arange(GRID_W)[None, :]
    valid = (cj >= col_start[:, None]) & (cj < col_start[:, None] + NA_WIN_C)
    col_idx = np.clip(cj - cols[:, None] + (NA_WIN_C - 1), 0, 2 * NA_WIN_C - 2)
    col_sel = ((col_idx[:, :, None] == np.arange(2 * NA_WIN_C - 1)) & valid[:, :, None]).astype(np.float32)
    t = jnp.einsum('hrk,cjk->hrcj', rpb.astype(F32), col_sel, precision=lax.Precision.HIGHEST)
    t = t + jnp.where(valid, 0.0, NEG_BIG)[None, None].astype(F32)
    return jnp.concatenate([t[:, :-1], t[:, 1:]], axis=-1)


def _na_kernel(q_ref, k_ref, v_ref, bias_ref, o_ref, *, rows_total):
    r0 = pl.program_id(1) * NA_Q_ROWS
    kstart = jnp.clip(r0 - NA_Q_ROWS, 0, rows_total - NA_KV_ROWS)
    lane = lax.broadcasted_iota(jnp.int32, (GRID_W, 128), 1)
    first_head = lane < NA_HEAD_DIM
    win = NA_WIN_R * GRID_W

    def body(it, carry):
        rows = []
        for sub in range(NA_ROWS_PER_STEP):
            i = it * NA_ROWS_PER_STEP + sub
            r = r0 + i
            rs = jnp.clip(r - NA_WIN_R // 2, 0, rows_total - NA_WIN_R)
            rows.append((pl.multiple_of(i * GRID_W, GRID_W), pl.multiple_of((rs - kstart) * GRID_W, GRID_W), r - rs))
        scores = []
        for qoff, koff, didx in rows:
            for h in range(NA_HEADS):
                cs = slice(128 * (h // 2), 128 * (h // 2 + 1))
                q2 = q_ref[pl.ds(qoff, GRID_W), cs]
                qm = jnp.where(first_head if h % 2 == 0 else jnp.logical_not(first_head), q2, jnp.zeros_like(q2))
                s = lax.dot_general(qm, k_ref[pl.ds(koff, win), cs], (((1,), (1,)), ((), ())),
                                    preferred_element_type=F32)
                bias = [bias_ref[h, 2 * kk + (NA_WIN_R - 1) - didx] for kk in range(NA_WIN_R // 2)]
                scores.append(s + jnp.concatenate(bias, axis=1))
        probs = []
        for s in scores:
            m = jnp.max(s, axis=-1, keepdims=True)
            e = jnp.exp(s - m)
            probs.append((e.astype(BF16), jnp.sum(e, axis=-1, keepdims=True)))
        for n, (qoff, koff, didx) in enumerate(rows):
            outs = []
            for h in range(NA_HEADS):
                e, l = probs[n * NA_HEADS + h]
                cs = slice(128 * (h // 2), 128 * (h // 2 + 1))
                outs.append(jnp.dot(e, v_ref[pl.ds(koff, win), cs], preferred_element_type=F32) / l)
            for p in range(NA_HEADS // 2):
                o_ref[pl.ds(qoff, GRID_W), 128 * p:128 * (p + 1)] = jnp.where(
                    first_head, outs[2 * p], outs[2 * p + 1]).astype(BF16)
        return carry

    lax.fori_loop(0, NA_Q_ROWS // NA_ROWS_PER_STEP, body, 0)


def _na(q, k, v, bias, l):
    B, S, _ = q.shape
    rows = S // GRID_W
    assert rows >= NA_KV_ROWS and rows % NA_Q_ROWS == 0
    qtok = NA_Q_ROWS * GRID_W
    kvtok = NA_KV_ROWS * GRID_W

    def kv_map(b, rb):
        return (b, jnp.clip(rb * NA_Q_ROWS - NA_Q_ROWS, 0, rows - NA_KV_ROWS) * GRID_W, 0)

    kv_spec = pl.BlockSpec((None, pl.Element(kvtok), pl.Element(NA_WIDTH)), kv_map)
    return pl.pallas_call(
        functools.partial(_na_kernel, rows_total=rows),
        grid=(B, rows // NA_Q_ROWS),
        in_specs=[
            pl.BlockSpec((None, qtok, NA_WIDTH), lambda b, rb: (b, rb, 0)),
            kv_spec,
            kv_spec,
            _layer_spec(bias, l, 2),
        ],
        out_specs=pl.BlockSpec((None, qtok, NA_WIDTH), lambda b, rb: (b, rb, 0)),
        out_shape=jax.ShapeDtypeStruct((B, S, NA_WIDTH), BF16),
        compiler_params=_cparams(("parallel", "parallel")),
        name="na",
    )(q, k, v, bias)


def _s5_tables(lam_re, lam_im, log_dt, b_re, b_im, c_re, c_im):
    L = S5_CHUNK
    lr = jnp.minimum(lam_re.astype(F32), -1e-4)
    li = lam_im.astype(F32)
    dt = jnp.exp(log_dt.astype(F32))[:, :, None]
    mag = jnp.exp(lr * dt)
    ar = mag * jnp.cos(li * dt)
    ai = mag * jnp.sin(li * dt)
    den = lr * lr + li * li
    fr = ((ar - 1.0) * lr + ai * li) / den
    fi = (ai * lr - (ar - 1.0) * li) / den
    br = b_re.astype(F32)
    bi = b_im.astype(F32)
    bbr = fr[..., None] * br - fi[..., None] * bi
    bbi = fr[..., None] * bi + fi[..., None] * br
    cr = c_re.astype(F32)
    ci = c_im.astype(F32)
    j = jnp.arange(L + 1, dtype=F32)[:, None, None, None]
    pm = jnp.exp(j * (lr * dt)[None])
    pr = pm * jnp.cos(j * (li * dt)[None])
    pi_ = pm * jnp.sin(j * (li * dt)[None])
    abr = pr[..., None] * bbr[None] - pi_[..., None] * bbi[None]
    abi = pr[..., None] * bbi[None] + pi_[..., None] * bbr[None]
    kern = jnp.einsum('dgcp,jdgpe->jdgce', cr, abr) - jnp.einsum('dgcp,jdgpe->jdgce', ci, abi)
    s_idx = np.arange(L)[:, None]
    t_idx = np.arange(L)[None, :]
    lag_f = np.clip(t_idx - s_idx, 0, L)
    lag_b = np.clip(s_idx - t_idx, 0, L)
    kf = jnp.where((t_idx >= s_idx)[:, :, None, None, None], kern[lag_f, 0], 0.0)
    kb = jnp.where((s_idx >= t_idx)[:, :, None, None, None], kern[lag_b, 1], 0.0)
    tmat = jnp.transpose(kf + kb, (2, 0, 4, 1, 3)).reshape(S5_GROUPS, L * S5_GROUP_CH, L * S5_GROUP_CH)
    ef_r = abr[L - 1 - np.arange(L), 0]
    ef_i = abi[L - 1 - np.arange(L), 0]
    eb_r = abr[np.arange(L), 1]
    eb_i = abi[np.arange(L), 1]

    def to_rows(a):
        return jnp.transpose(a, (1, 0, 3, 2)).reshape(S5_GROUPS, L * S5_GROUP_CH, S5_STATE)

    we = jnp.concatenate([to_rows(ef_r), to_rows(eb_r), to_rows(ef_i), to_rows(eb_i)], axis=-1)
    def readout(d, powers):
        p_r = pr[powers, d]
        p_i = pi_[powers, d]
        m_re = cr[d][None] * p_r[:, :, None, :] - ci[d][None] * p_i[:, :, None, :]
        m_im = cr[d][None] * p_i[:, :, None, :] + ci[d][None] * p_r[:, :, None, :]
        to_cols = lambda a: jnp.transpose(a, (1, 3, 0, 2)).reshape(S5_GROUPS, S5_STATE, L * S5_GROUP_CH)
        return to_cols(m_re), to_cols(-m_im)

    yf_re, yf_im = readout(0, np.arange(L) + 1)
    yb_re, yb_im = readout(1, L - np.arange(L))
    wy = jnp.concatenate([yf_re, yb_re, yf_im, yb_im], axis=1)
    al = jnp.stack([jnp.concatenate([pr[L, 0], pr[L, 1]], axis=-1),
                    jnp.concatenate([pi_[L, 0], pi_[L, 1]], axis=-1)], axis=1)
    return tmat.astype(BF16), we.astype(BF16), wy.astype(BF16), al


def _s5_kernel(u_ref, t_ref, we_ref, wy_ref, al_ref, y_ref, e_ref, xp_ref, *, n_chunks, batch):
    lc = S5_CHUNK * S5_GROUP_CH
    u = u_ref[...].reshape(batch * n_chunks, lc)
    y_ref[...] = jnp.dot(u, t_ref[...], preferred_element_type=F32).reshape(batch, n_chunks, lc)
    e = jnp.dot(u, we_ref[...], preferred_element_type=F32)
    e_ref[...] = jnp.zeros(e_ref.shape, F32)
    half = 2 * S5_STATE
    for b in range(batch):
        for hf in range(2):
            e_ref[hf, pl.ds(b, n_chunks, stride=S5_BPAD), :] = e[b * n_chunks:(b + 1) * n_chunks,
                                                                 hf * half:(hf + 1) * half]
    a_re = al_ref[0:1, :]
    a_im = al_ref[1:2, :]
    lane = lax.broadcasted_iota(jnp.int32, (S5_BPAD, half), 1)
    is_fwd = lane < S5_STATE

    def body(j, carry):
        xr, xi = carry
        rf = pl.multiple_of(j * S5_BPAD, S5_BPAD)
        rb = pl.multiple_of((n_chunks - 1 - j) * S5_BPAD, S5_BPAD)
        xp_ref[pl.ds(rf, S5_BPAD), 0:S5_STATE] = xr[:, 0:S5_STATE]
        xp_ref[pl.ds(rb, S5_BPAD), S5_STATE:half] = xr[:, S5_STATE:half]
        xp_ref[pl.ds(rf, S5_BPAD), half:half + S5_STATE] = xi[:, 0:S5_STATE]
        xp_ref[pl.ds(rb, S5_BPAD), half + S5_STATE:2 * half] = xi[:, S5_STATE:half]
        er = jnp.where(is_fwd, e_ref[0, pl.ds(rf, S5_BPAD), :], e_ref[0, pl.ds(rb, S5_BPAD), :])
        ei = jnp.where(is_fwd, e_ref[1, pl.ds(rf, S5_BPAD), :], e_ref[1, pl.ds(rb, S5_BPAD), :])
        nxr = a_re * xr - a_im * xi + er
        nxi = a_re * xi + a_im * xr + ei
        return nxr, nxi

    zero = jnp.zeros((S5_BPAD, half), F32)
    lax.fori_loop(0, n_chunks, body, (zero, zero))
    yi = jnp.dot(xp_ref[...].astype(BF16), wy_ref[...], preferred_element_type=F32)
    e_ref[0] = yi[:, :half]
    e_ref[1] = yi[:, half:]
    for b in range(batch):
        y_ref[b] += jnp.concatenate(
            [e_ref[hf, pl.ds(b, n_chunks, stride=S5_BPAD), :] for hf in range(2)], axis=1)


def _s5(ug, tabs, l):
    tmat, we, wy, al = tabs
    G, B, n, lc = ug.shape
    assert B <= S5_BPAD and lc == 4 * S5_STATE
    rows = n * S5_BPAD
    grp = lambda shape: pl.BlockSpec((None,) + shape, lambda g: (g,) + (0,) * len(shape))
    tab = lambda shape: pl.BlockSpec((None, None) + shape, lambda g: (l, g) + (0,) * len(shape))
    return pl.pallas_call(
        functools.partial(_s5_kernel, n_chunks=n, batch=B),
        grid=(G,),
        in_specs=[grp((B, n, lc)), tab((lc, lc)), tab((lc, 4 * S5_STATE)), tab((4 * S5_STATE, lc)),
                  tab((2, 2 * S5_STATE))],
        out_specs=grp((B, n, lc)),
        out_shape=jax.ShapeDtypeStruct((G, B, n, lc), F32),
        scratch_shapes=[pltpu.VMEM((2, rows, 2 * S5_STATE), F32), pltpu.VMEM((rows, 4 * S5_STATE), F32)],
        compiler_params=_cparams(("parallel",)),
        name="s5",
    )(ug, tmat, we, wy, al)


def _memkv_kernel(m_ref, g_ref, w_ref, o_ref):
    x = m_ref[...]
    ms = jnp.mean(x * x, axis=-1, keepdims=True)
    h = (x * lax.rsqrt(ms + RMS_EPS) * g_ref[...]).astype(BF16)
    o_ref[...] = jnp.dot(h, w_ref[...], preferred_element_type=F32).astype(BF16)


def _memkv(mem, norm_mem, w_bf):
    B, M, D = mem.shape
    L = w_bf.shape[0]
    return pl.pallas_call(
        _memkv_kernel,
        grid=(L, B),
        in_specs=[pl.BlockSpec((None, M, D), lambda l, b: (b, 0, 0)),
                  pl.BlockSpec((None, 1, D), lambda l, b: (l, 0, 0)),
                  pl.BlockSpec((None, D, 2 * XA_WIDTH), lambda l, b: (l, 0, 0))],
        out_specs=pl.BlockSpec((None, None, M, 2 * XA_WIDTH), lambda l, b: (l, b, 0, 0)),
        out_shape=jax.ShapeDtypeStruct((L, B, M, 2 * XA_WIDTH), BF16),
        compiler_params=_cparams(("parallel", "parallel")),
        name="memkv",
    )(mem, norm_mem.reshape(L, 1, D), w_bf)


def _gelu_tanh(x):
    return 0.5 * x * (1.0 + jnp.tanh(math.sqrt(2.0 / math.pi) * (x + 0.044715 * (x * x * x))))


def _router(z):
    lane = lax.broadcasted_iota(jnp.int32, z.shape, 1).astype(F32)
    ninf = jnp.full_like(z, -jnp.inf)
    far = jnp.full_like(z, 1e9)
    cm = (lane >= MOE_EXPERTS) & (lane < MOE_EXPERTS + MOE_GROUPS)
    cmax = jnp.max(jnp.where(cm, z, ninf), axis=-1, keepdims=True)
    glane = jnp.min(jnp.where(cm & (z == cmax), lane, far), axis=-1, keepdims=True)
    psum = jnp.sum(jnp.where(cm, jnp.exp(jnp.where(cm, z, cmax) - cmax), 0.0), axis=-1, keepdims=True)
    p_grp = 1.0 / psum
    f0 = (glane - MOE_EXPERTS) * MOE_EPG
    fm = (lane >= f0) & (lane < f0 + MOE_EPG)
    v1 = jnp.max(jnp.where(fm, z, ninf), axis=-1, keepdims=True)
    i1 = jnp.min(jnp.where(fm & (z == v1), lane, far), axis=-1, keepdims=True)
    fm2 = fm & (lane != i1)
    v2 = jnp.max(jnp.where(fm2, z, ninf), axis=-1, keepdims=True)
    i2 = jnp.min(jnp.where(fm2 & (z == v2), lane, far), axis=-1, keepdims=True)
    t = jnp.exp(v2 - v1)
    w1 = p_grp / (1.0 + t)
    w2 = p_grp * t / (1.0 + t)
    comb = jnp.where(lane == i1, w1, 0.0) + jnp.where(lane == i2, w2, 0.0)
    return comb, glane - MOE_EXPERTS


def _merge_kernel(x_ref, ona_ref, ys5_ref, u_ref, qx_ref, gate_ref, mkv_ref,
                  d_ref, wglu_ref, wna_ref, ws5_ref, wxa_ref, wout_ref, nffn_ref, wrh_ref, wrl_ref, br_ref,
                  xo_ref, hp_ref, meta_ref, cnt_ref, mrg_ref, ys_ref):
    tm = x_ref.shape[0]
    _group_to_time_major(ys5_ref, ys_ref, tm // S5_CHUNK)
    ys = jnp.concatenate([ys_ref[0], ys_ref[1]], axis=1)
    y = _gelu_tanh(ys + d_ref[...] * u_ref[...])
    g = jnp.dot(y.astype(BF16), wglu_ref[...], preferred_element_type=F32)
    os5 = (g[:, :S5_WIDTH] * jax.nn.sigmoid(g[:, S5_WIDTH:])).astype(BF16)
    lane = lax.broadcasted_iota(jnp.int32, (tm, 128), 1)
    first_head = lane < XA_HEAD_DIM
    oxa = []
    for p in range(XA_HEADS // 2):
        cs = slice(128 * p, 128 * (p + 1))
        q2 = qx_ref[:, cs]
        k2 = mkv_ref[:, cs]
        v2 = mkv_ref[:, XA_WIDTH + 128 * p:XA_WIDTH + 128 * (p + 1)]
        outs = []
        for hh in range(2):
            qm = jnp.where(first_head if hh == 0 else jnp.logical_not(first_head), q2, jnp.zeros_like(q2))
            s = lax.dot_general(qm, k2, (((1,), (1,)), ((), ())), preferred_element_type=F32)
            m = jnp.max(s, axis=-1, keepdims=True)
            e = jnp.exp(s - m)
            l = jnp.sum(e, axis=-1, keepdims=True)
            outs.append(jnp.dot(e.astype(BF16), v2, preferred_element_type=F32) / l)
        oxa.append(jnp.where(first_head, outs[0], outs[1]).astype(BF16))
    oxa = jnp.concatenate(oxa, axis=-1)
    ona = ona_ref[...]
    cw = 256
    for c in range(D_MODEL // cw):
        cs = slice(cw * c, cw * (c + 1))
        m = gate_ref[:, cw * c:cw * (c + 1)].astype(F32) * jnp.dot(ona, wna_ref[:, cs], preferred_element_type=F32)
        m += gate_ref[:, D_MODEL + cw * c:D_MODEL + cw * (c + 1)].astype(F32) * jnp.dot(
            os5, ws5_ref[:, cs], preferred_element_type=F32)
        m += gate_ref[:, 2 * D_MODEL + cw * c:2 * D_MODEL + cw * (c + 1)].astype(F32) * jnp.dot(
            oxa, wxa_ref[:, cs], preferred_element_type=F32)
        mrg_ref[:, cs] = m.astype(BF16)
    xn = x_ref[...] + jnp.dot(mrg_ref[...], wout_ref[...], preferred_element_type=F32)
    xo_ref[...] = xn
    ms = jnp.mean(xn * xn, axis=-1, keepdims=True)
    h2 = xn * lax.rsqrt(ms + RMS_EPS) * nffn_ref[...]
    hi = h2.astype(BF16)
    hi32 = hi.astype(F32)
    lo = (h2 - hi32).astype(BF16)
    z = (jnp.dot(hi, wrh_ref[...], preferred_element_type=F32)
         + jnp.dot(lo, wrh_ref[...], preferred_element_type=F32)
         + jnp.dot(hi, wrl_ref[...], preferred_element_type=F32)) + br_ref[...]
    hw = D_MODEL // 2
    packed = pltpu.bitcast(hi32[:, :hw], jnp.uint32) | (pltpu.bitcast(hi32[:, hw:], jnp.uint32) >> 16)
    for j in range(hw // 128):
        hp_ref[pl.ds(j, tm, stride=TOKEN_REC_ROWS), :] = packed[:, 128 * j:128 * (j + 1)]
    comb, gid = _router(z)
    hp_ref[pl.ds(hw // 128, tm, stride=TOKEN_REC_ROWS), :] = pltpu.bitcast(comb, jnp.uint32)
    for j in range(hw // 128 + 1, TOKEN_REC_ROWS):
        hp_ref[pl.ds(j, tm, stride=TOKEN_REC_ROWS), :] = jnp.zeros((tm, 128), jnp.uint32)
    lanef = lane.astype(F32)
    onehot = lanef == gid
    row = lax.broadcasted_iota(jnp.int32, (tm, tm), 0)
    col = lax.broadcasted_iota(jnp.int32, (tm, tm), 1)
    tri = jnp.where(col <= row, 1.0, 0.0).astype(BF16)
    csum = jnp.dot(tri, jnp.where(onehot, 1.0, 0.0).astype(BF16), preferred_element_type=F32)
    rank = jnp.sum(jnp.where(onehot, csum, 0.0), axis=-1, keepdims=True) - 1.0
    meta = jnp.where(lane == 0, gid, jnp.where(lane == 1, rank, 0.0))
    meta_ref[...] = jnp.transpose(meta)[:8, :]
    cnt_ref[...] = jnp.broadcast_to(csum[tm - 1:tm, :], (8, 128))


def _merge(x, ona, ys5, u, qx, gates, mkv, params, l):
    B, S, D = x.shape
    tm = min(TOKEN_TILE, S)
    M = mkv.shape[2]
    n_i = S // tm

    def tok(width):
        return pl.BlockSpec((None, tm, width), lambda b, i: (b, i, 0))

    return pl.pallas_call(
        _merge_kernel,
        grid=(B, n_i),
        in_specs=[tok(D), tok(NA_WIDTH),
                  pl.BlockSpec((S5_GROUPS, None, tm // S5_CHUNK, S5_CHUNK * S5_GROUP_CH), lambda b, i: (0, b, i, 0)),
                  tok(S5_WIDTH), tok(XA_WIDTH), tok(N_BRANCHES * D),
                  pl.BlockSpec((None, None, M, 2 * XA_WIDTH), lambda b, i: (l, b, 0, 0))]
                 + [_layer_spec(p, l, 2) for p in params],
        out_specs=[tok(D), pl.BlockSpec((None, tm * TOKEN_REC_ROWS, 128), lambda b, i: (b, i, 0)),
                   pl.BlockSpec((None, None, 8, tm), lambda b, i: (b, i, 0, 0)),
                   pl.BlockSpec((None, None, 8, 128), lambda b, i: (b, i, 0, 0))],
        out_shape=[jax.ShapeDtypeStruct((B, S, D), F32),
                   jax.ShapeDtypeStruct((B, S * TOKEN_REC_ROWS, 128), jnp.uint32),
                   jax.ShapeDtypeStruct((B, n_i, 8, tm), F32),
                   jax.ShapeDtypeStruct((B, n_i, 8, 128), F32)],
        scratch_shapes=[pltpu.VMEM((tm, D), BF16), pltpu.VMEM((2, tm, 128), F32)],
        compiler_params=_cparams(("parallel", "parallel")),
        name="merge",
    )(x, ona, ys5, u, qx, gates, mkv, *params)


def _moe_plan(meta, cnt):
    B, n_i, _, tm = meta.shape
    n_tokens = B * n_i * tm
    gid = meta[:, :, 0, :].reshape(B * n_i, tm).astype(jnp.int32)
    rank = meta[:, :, 1, :].reshape(B * n_i, tm).astype(jnp.int32)
    tile_counts = cnt[:, :, 0, :MOE_GROUPS].reshape(B * n_i, MOE_GROUPS).astype(jnp.int32)
    counts = jnp.sum(tile_counts, axis=0)
    padded = ((counts + MOE_TILE - 1) // MOE_TILE) * MOE_TILE
    ends = jnp.cumsum(padded)
    tile_base = (ends - padded)[None, :] + jnp.cumsum(tile_counts, axis=0) - tile_counts
    pos = rank
    for g in range(MOE_GROUPS):
        pos = pos + jnp.where(gid == g, tile_base[:, g:g + 1], 0)
    pos = pos.reshape(n_tokens)
    n_tiles = n_tokens // MOE_TILE + MOE_GROUPS
    starts = jnp.arange(n_tiles, dtype=jnp.int32) * MOE_TILE
    tile_grp = jnp.minimum(jnp.sum(starts[:, None] >= ends[None, :], axis=1), MOE_GROUPS - 1).astype(jnp.int32)
    return pos.astype(jnp.int32), ends.astype(jnp.int32), padded.astype(jnp.int32), tile_grp


def _rec(ref, token, count=1):
    return ref.at[pl.ds(pl.multiple_of(token * TOKEN_REC_ROWS, TOKEN_REC_ROWS), count * TOKEN_REC_ROWS), :]


DMA_UNROLL = 8


def _issue_record_dmas(count, copy_of):
    def body(blk, c):
        for k in range(DMA_UNROLL):
            copy_of(blk * DMA_UNROLL + k).start(priority=k % 2)
        return c
    lax.fori_loop(0, count // DMA_UNROLL, body, 0)


def _permute_kernel(ends_ref, padded_ref, pos_ref, h_ref, hs_ref, zero_ref, sem):
    tm = h_ref.shape[0] // TOKEN_REC_ROWS

    @pl.when(pl.program_id(0) == 0)
    def _():
        zero_ref[...] = jnp.zeros(zero_ref.shape, zero_ref.dtype)
        n_rows = hs_ref.shape[0] // TOKEN_REC_ROWS
        for g in range(MOE_GROUPS):
            tail = ends_ref[MOE_GROUPS - 1] + g * MOE_TILE
            for cond, start in ((padded_ref[g] > 0, ends_ref[g] - MOE_TILE), (tail < n_rows, tail)):
                @pl.when(cond)
                def _():
                    cp = pltpu.make_async_copy(zero_ref, _rec(hs_ref, start, MOE_TILE), sem)
                    cp.start()
                    cp.wait()

    _issue_record_dmas(tm, lambda r: pltpu.make_async_copy(_rec(h_ref, r), _rec(hs_ref, pos_ref[0, 0, r]), sem))
    pltpu.make_async_copy(h_ref, _rec(hs_ref, 0, tm), sem).wait()


def _moe_permute(hp2, pos3, ends, padded, n_rows):
    T = hp2.shape[0] // TOKEN_REC_ROWS
    tm = min(TOKEN_TILE, T)
    return pl.pallas_call(
        _permute_kernel,
        grid_spec=pltpu.PrefetchScalarGridSpec(
            num_scalar_prefetch=2,
            grid=(T // tm,),
            in_specs=[pl.BlockSpec((1, 1, tm), lambda i, e, p: (i, 0, 0), memory_space=pltpu.SMEM),
                      pl.BlockSpec((tm * TOKEN_REC_ROWS, 128), lambda i, e, p: (i, 0))],
            out_specs=pl.BlockSpec(memory_space=pl.ANY),
            scratch_shapes=[pltpu.VMEM((MOE_TILE * TOKEN_REC_ROWS, 128), jnp.uint32), pltpu.SemaphoreType.DMA(())],
        ),
        out_shape=jax.ShapeDtypeStruct((n_rows * TOKEN_REC_ROWS, 128), jnp.uint32),
        compiler_params=_cparams(("arbitrary",)),
        name="moe_permute",
    )(ends, padded, pos3, hp2)


def _moe_kernel(grp_ref, h_ref, wg_ref, wu_ref, wd_ref, o_ref):
    hw = D_MODEL // 2

    def field(j):
        return h_ref[pl.ds(j, MOE_TILE, stride=TOKEN_REC_ROWS), :]

    w = jnp.concatenate([field(j) for j in range(hw // 128)], axis=1)
    h = jnp.concatenate([pltpu.bitcast(w & jnp.uint32(0xFFFF0000), F32).astype(BF16),
                         pltpu.bitcast(w << 16, F32).astype(BF16)], axis=1)
    comb = pltpu.bitcast(field(hw // 128), F32)
    lane = lax.broadcasted_iota(jnp.int32, comb.shape, 1)
    first = grp_ref[pl.program_id(0)] * MOE_EPG
    acts = []
    for ee in range(MOE_EPG):
        c = jnp.sum(jnp.where(lane == first + ee, comb, 0.0), axis=-1, keepdims=True)
        g = jnp.dot(h, wg_ref[ee], preferred_element_type=F32)
        u = jnp.dot(h, wu_ref[ee], preferred_element_type=F32)
        acts.append((g * jax.nn.sigmoid(g) * u * c).astype(BF16))
    a = jnp.concatenate(acts, axis=-1)
    out = jnp.dot(a, wd_ref[...].reshape(MOE_EPG * MOE_FF, D_MODEL), preferred_element_type=F32)
    for j in range(TOKEN_REC_ROWS):
        o_ref[pl.ds(j, MOE_TILE, stride=TOKEN_REC_ROWS), :] = out[:, 128 * j:128 * (j + 1)]


def _moe_experts(hs, tile_grp, wg, wu, wd, l):
    n_rows = hs.shape[0] // TOKEN_REC_ROWS
    n_tiles = n_rows // MOE_TILE
    rec_tile = pl.BlockSpec((MOE_TILE * TOKEN_REC_ROWS, 128), lambda i, g: (i, 0))
    return pl.pallas_call(
        _moe_kernel,
        grid_spec=pltpu.PrefetchScalarGridSpec(
            num_scalar_prefetch=1,
            grid=(n_tiles,),
            in_specs=[rec_tile,
                      pl.BlockSpec((None, MOE_EPG, D_MODEL, MOE_FF), lambda i, g: (l, g[i], 0, 0)),
                      pl.BlockSpec((None, MOE_EPG, D_MODEL, MOE_FF), lambda i, g: (l, g[i], 0, 0)),
                      pl.BlockSpec((None, MOE_EPG, MOE_FF, D_MODEL), lambda i, g: (l, g[i], 0, 0))],
            out_specs=rec_tile,
        ),
        out_shape=jax.ShapeDtypeStruct((n_rows * TOKEN_REC_ROWS, 128), F32),
        compiler_params=_cparams(("arbitrary",)),
        name="moe_experts",
    )(tile_grp, hs, wg, wu, wd)


def _final_kernel(pos_ref, x_ref, ys_ref, fn_ref, o_ref, buf_ref, sem):
    tm = x_ref.shape[0]
    _issue_record_dmas(tm, lambda r: pltpu.make_async_copy(_rec(ys_ref, pos_ref[0, 0, r]), _rec(buf_ref, r), sem))
    pltpu.make_async_copy(_rec(ys_ref, 0, tm), buf_ref, sem).wait()
    moe = jnp.concatenate([buf_ref[pl.ds(j, tm, stride=TOKEN_REC_ROWS), :] for j in range(TOKEN_REC_ROWS)], axis=1)
    xn = x_ref[...] + moe
    ms = jnp.mean(xn * xn, axis=-1, keepdims=True)
    o_ref[...] = xn * lax.rsqrt(ms + RMS_EPS) * fn_ref[...]


def _final(x2, ys, pos3, fnorm):
    T, D = x2.shape
    tm = min(TOKEN_TILE, T)
    return pl.pallas_call(
        _final_kernel,
        grid=(T // tm,),
        in_specs=[pl.BlockSpec((1, 1, tm), lambda i: (i, 0, 0), memory_space=pltpu.SMEM),
                  pl.BlockSpec((tm, D), lambda i: (i, 0)),
                  pl.BlockSpec(memory_space=pl.ANY),
                  pl.BlockSpec((1, D), lambda i: (0, 0))],
        out_specs=pl.BlockSpec((tm, D), lambda i: (i, 0)),
        out_shape=jax.ShapeDtypeStruct((T, D), F32),
        scratch_shapes=[pltpu.VMEM((tm * TOKEN_REC_ROWS, 128), F32), pltpu.SemaphoreType.DMA(())],
        compiler_params=_cparams(("arbitrary",)),
        name="final_unpermute_norm",
    )(pos3, x2, ys, fnorm.reshape(1, D))


def _moe(hp, meta, cnt, wg, wu, wd, l):
    T = hp.shape[0] * hp.shape[1] // TOKEN_REC_ROWS
    tm = min(TOKEN_TILE, T)
    pos, ends, padded, tile_grp = _moe_plan(meta, cnt)
    pos3 = pos.reshape(T // tm, 1, tm)
    n_rows = T + MOE_GROUPS * MOE_TILE
    hs = _moe_permute(hp.reshape(T * TOKEN_REC_ROWS, 128), pos3, ends, padded, n_rows)
    return _moe_experts(hs, tile_grp, wg, wu, wd, l), pos3


def kernel(x, mem, norm_mix, norm_ffn, norm_mem, w_in, b_gate, na_rpb, s5_lam_re, s5_lam_im, s5_log_dt, s5_b_re, s5_b_im, s5_c_re, s5_c_im, s5_d, s5_w_glu, w_mem_kv, w_br_na, w_br_s5, w_br_xa, w_out, moe_w_coarse, moe_b_coarse, moe_w_fine, moe_b_fine, moe_w_gate, moe_w_up, moe_w_down, final_norm):
    depth, D = w_in.shape[0], w_in.shape[1]
    mkv_all = _memkv(mem, norm_mem, w_mem_kv.astype(BF16))
    w_in_bf = w_in.astype(BF16)
    norm_mix3 = norm_mix.reshape(depth, 1, D)
    b_gate3 = b_gate.reshape(depth, 1, N_BRANCHES * D)
    na_bias = jax.vmap(_na_bias_table)(na_rpb)
    s5_tabs = jax.vmap(_s5_tables)(s5_lam_re, s5_lam_im, s5_log_dt, s5_b_re, s5_b_im, s5_c_re, s5_c_im)
    wr = jnp.concatenate([moe_w_fine, moe_w_coarse], axis=2).astype(F32)
    wr = jnp.pad(wr, ((0, 0), (0, 0), (0, ROUTER_LANES - wr.shape[2])))
    wrh = wr.astype(BF16)
    wrl = (wr - wrh.astype(F32)).astype(BF16)
    br = jnp.pad(jnp.concatenate([moe_b_fine, moe_b_coarse], axis=1).astype(F32),
                 ((0, 0), (0, ROUTER_LANES - MOE_EXPERTS - MOE_GROUPS))).reshape(depth, 1, ROUTER_LANES)
    merge_params = (s5_d.reshape(depth, 1, S5_WIDTH), s5_w_glu.astype(BF16), w_br_na.astype(BF16),
                    w_br_s5.astype(BF16), w_br_xa.astype(BF16), w_out.astype(BF16),
                    norm_ffn.reshape(depth, 1, D), wrh, wrl, br)
    wg, wu, wd = moe_w_gate.astype(BF16), moe_w_up.astype(BF16), moe_w_down.astype(BF16)
    moe = None
    for l in range(depth):
        if moe is None:
            q, k, v, u, ug, qx, gates = _inproj(x, norm_mix3, w_in_bf, b_gate3, l)
        else:
            q, k, v, u, ug, qx, gates, x = _inproj(x, norm_mix3, w_in_bf, b_gate3, l, moe)
        ona = _na(q, k, v, na_bias, l)
        ys5 = _s5(ug, s5_tabs, l)
        x, hp, meta, cnt = _merge(x, ona, ys5, u, qx, gates, mkv_all, merge_params, l)
        moe = _moe(hp, meta, cnt, wg, wu, wd, l)
    B, S, _ = x.shape
    ys, pos3 = moe
    return _final(x.reshape(B * S, D), ys, pos3, final_norm).reshape(B, S, D)
```

```python
import functools
import math

import numpy as np
import jax
import jax.numpy as jnp
from jax import lax
from jax.experimental import pallas as pl
from jax.experimental.pallas import tpu as pltpu

F32 = jnp.float32
BF16 = jnp.bfloat16

D_MODEL = 1024
GRID_W = 64
RMS_EPS = 1e-6
NA_HEADS = 8
NA_HEAD_DIM = 64
NA_WIDTH = NA_HEADS * NA_HEAD_DIM
NA_WIN_R = 8
NA_WIN_C = 16
NA_KV_ROWS = 24
NA_Q_ROWS = 8
NA_ROWS_PER_STEP = 4
S5_GROUPS = 16
S5_GROUP_CH = 16
S5_WIDTH = S5_GROUPS * S5_GROUP_CH
S5_STATE = 64
S5_CHUNK = 16
S5_BPAD = 8
S5_GROUPS_PER_STEP = 2
XA_HEADS = 4
XA_HEAD_DIM = 64
XA_WIDTH = XA_HEADS * XA_HEAD_DIM
N_BRANCHES = 3
MOE_GROUPS = 4
MOE_EPG = 4
MOE_EXPERTS = MOE_GROUPS * MOE_EPG
MOE_FF = 256
MOE_ECHUNK = 4
ROUTER_LANES = 128
TOKEN_REC_ROWS = 8
MOE_PAIR_A = (0, 0, 0, 1, 1, 2)
MOE_PAIR_B = (1, 2, 3, 2, 3, 3)
MOE_PAIRS = len(MOE_PAIR_A)
MOE_BINS = MOE_GROUPS * MOE_PAIRS
MOE_TILE = 256
NEG_BIG = -1e30

VMEM_LIMIT = 52 * 1024 * 1024
TOKEN_TILE = 512


def _cparams(sem):
    return pltpu.CompilerParams(dimension_semantics=sem, vmem_limit_bytes=VMEM_LIMIT)


def _lane_block_mask(rows, j):
    lane = lax.broadcasted_iota(jnp.int32, (rows, 128), 1)
    lo = S5_GROUP_CH * (j % 8)
    return (lane >= lo) & (lane < lo + S5_GROUP_CH)


def _time_to_group_major(src_ref, rows):
    dest = [[None, None] for _ in range(S5_GROUPS)]
    for s in range(S5_CHUNK):
        halves = tuple(src_ref[hf, pl.ds(s, rows, stride=S5_CHUNK), :] for hf in range(2))
        m = _lane_block_mask(rows, s)
        for g in range(S5_GROUPS):
            shift = (S5_GROUP_CH * ((s % 8) - (g % 8))) % 128
            r = pltpu.roll(halves[g // 8], shift, axis=1) if shift else halves[g // 8]
            prev = dest[g][s // 8]
            dest[g][s // 8] = jnp.where(m, r, 0.0 if prev is None else prev)
    return [jnp.concatenate(d, axis=1) for d in dest]


def _group_to_time_major(src_ref, dst_ref, rows):
    for t in range(S5_CHUNK):
        out = [None, None]
        for g in range(S5_GROUPS):
            src = src_ref[g, :, 128 * (t // 8):128 * (t // 8 + 1)]
            shift = (S5_GROUP_CH * ((g % 8) - (t % 8))) % 128
            r = pltpu.roll(src, shift, axis=1) if shift else src
            prev = out[g // 8]
            out[g // 8] = jnp.where(_lane_block_mask(rows, g), r, 0.0 if prev is None else prev)
        for hf in range(2):
            dst_ref[hf, pl.ds(t, rows, stride=S5_CHUNK), :] = out[hf]


def _inproj_kernel(*refs, with_moe):
    if with_moe:
        (pos_ref, pos_next_ref, x_ref, ys_ref, g_ref, w_ref, bg_ref,
         q_ref, k_ref, v_ref, u_ref, ug_ref, qx_ref, gate_ref, xo_ref, uh_ref, buf_ref, sem) = refs
        tm = x_ref.shape[0]
        step = pl.program_id(0) * pl.num_programs(1) + pl.program_id(1)
        n_steps = pl.num_programs(0) * pl.num_programs(1)
        slot = lax.rem(step, 2)

        def fetch(p_ref, s):
            _issue_record_dmas(tm, lambda r: pltpu.make_async_copy(
                _rec(ys_ref, p_ref[0, 0, r]), _rec(buf_ref.at[s], r), sem.at[s]))

        @pl.when(step == 0)
        def _():
            fetch(pos_ref, 0)

        @pl.when(step + 1 < n_steps)
        def _():
            fetch(pos_next_ref, 1 - slot)

        pltpu.make_async_copy(_rec(ys_ref, 0, tm), buf_ref.at[slot], sem.at[slot]).wait()
        moe = jnp.concatenate([buf_ref[slot, pl.ds(j, tm, stride=TOKEN_REC_ROWS), :]
                               for j in range(TOKEN_REC_ROWS)], axis=1)
        x = x_ref[...] + moe
        xo_ref[...] = x
    else:
        x_ref, g_ref, w_ref, bg_ref, q_ref, k_ref, v_ref, u_ref, ug_ref, qx_ref, gate_ref, uh_ref = refs
        x = x_ref[...]
    ms = jnp.mean(x * x, axis=-1, keepdims=True)
    h = (x * lax.rsqrt(ms + RMS_EPS) * g_ref[...]).astype(BF16)

    def proj(a, b):
        return jnp.dot(h, w_ref[:, a:b], preferred_element_type=F32)

    c0 = NA_WIDTH
    q_ref[...] = (proj(0, c0) * (NA_HEAD_DIM ** -0.5)).astype(BF16)
    k_ref[...] = proj(c0, 2 * c0).astype(BF16)
    v_ref[...] = proj(2 * c0, 3 * c0).astype(BF16)
    c1 = 3 * c0
    u = proj(c1, c1 + S5_WIDTH)
    u_ref[...] = u
    uh_ref[0] = u[:, :128]
    uh_ref[1] = u[:, 128:]
    for g, ug in enumerate(_time_to_group_major(uh_ref, ug_ref.shape[1])):
        ug_ref[g] = ug.astype(BF16)
    c2 = c1 + S5_WIDTH
    qx_ref[...] = (proj(c2, c2 + XA_WIDTH) * (XA_HEAD_DIM ** -0.5)).astype(BF16)
    c3 = c2 + XA_WIDTH
    for j in range(N_BRANCHES):
        z = proj(c3 + D_MODEL * j, c3 + D_MODEL * (j + 1)) + bg_ref[:, D_MODEL * j:D_MODEL * (j + 1)]
        gate_ref[:, D_MODEL * j:D_MODEL * (j + 1)] = jax.nn.sigmoid(z).astype(BF16)


def _layer_spec(a, l, n_grid):
    zeros = (0,) * (a.ndim - 1)
    if n_grid == 1:
        index_map = lambda i: (l,) + zeros
    elif n_grid == 2:
        index_map = lambda b, i: (l,) + zeros
    else:
        raise ValueError(n_grid)
    return pl.BlockSpec((None,) + a.shape[1:], index_map, pipeline_mode=pl.Buffered(1))


def _inproj(x, g, w_bf, bg, l, moe=None):
    B, S, D = x.shape
    tm = min(TOKEN_TILE, S)
    n_i = S // tm

    def tok(width):
        return pl.BlockSpec((None, tm, width), lambda b, i: (b, i, 0))

    in_specs = [tok(D), _layer_spec(g, l, 2), _layer_spec(w_bf, l, 2), _layer_spec(bg, l, 2)]
    operands = [x, g, w_bf, bg]
    out_specs = [tok(NA_WIDTH), tok(NA_WIDTH), tok(NA_WIDTH), tok(S5_WIDTH),
                 pl.BlockSpec((S5_GROUPS, None, tm // S5_CHUNK, S5_CHUNK * S5_GROUP_CH), lambda b, i: (0, b, i, 0)),
                 tok(XA_WIDTH), tok(N_BRANCHES * D)]
    out_shape = [
        jax.ShapeDtypeStruct((B, S, NA_WIDTH), BF16),
        jax.ShapeDtypeStruct((B, S, NA_WIDTH), BF16),
        jax.ShapeDtypeStruct((B, S, NA_WIDTH), BF16),
        jax.ShapeDtypeStruct((B, S, S5_WIDTH), F32),
        jax.ShapeDtypeStruct((S5_GROUPS, B, S // S5_CHUNK, S5_CHUNK * S5_GROUP_CH), BF16),
        jax.ShapeDtypeStruct((B, S, XA_WIDTH), BF16),
        jax.ShapeDtypeStruct((B, S, N_BRANCHES * D), BF16),
    ]
    scratch = [pltpu.VMEM((2, tm, 128), F32)]
    if moe is not None:
        ys, pos3 = moe
        last = B * n_i - 1
        pos_spec = lambda nxt: pl.BlockSpec(
            (1, 1, tm), lambda b, i: (jnp.minimum(b * n_i + i + nxt, last), 0, 0), memory_space=pltpu.SMEM)
        in_specs = [pos_spec(0), pos_spec(1), in_specs[0], pl.BlockSpec(memory_space=pl.ANY)] + in_specs[1:]
        operands = [pos3, pos3, x, ys] + operands[1:]
        out_specs.append(tok(D))
        out_shape.append(jax.ShapeDtypeStruct((B, S, D), F32))
        scratch += [pltpu.VMEM((2, tm * TOKEN_REC_ROWS, 128), F32), pltpu.SemaphoreType.DMA((2,))]
    return pl.pallas_call(
        functools.partial(_inproj_kernel, with_moe=moe is not None),
        grid=(B, n_i),
        in_specs=in_specs,
        out_specs=out_specs,
        out_shape=out_shape,
        scratch_shapes=scratch,
        compiler_params=_cparams(("arbitrary", "arbitrary") if moe is not None else ("parallel", "parallel")),
        name="inproj",
    )(*operands)


def _na_bias_table(rpb):
    cols = np.arange(GRID_W)
    col_start = np.clip(cols - NA_WIN_C // 2, 0, GRID_W - NA_WIN_C)
    cj = np.arange(GRID_W)[None, :]
    valid = (cj >= col_start[:, None]) & (cj < col_start[:, None] + NA_WIN_C)
    col_idx = np.clip(cj - cols[:, None] + (NA_WIN_C - 1), 0, 2 * NA_WIN_C - 2)
    col_sel = ((col_idx[:, :, None] == np.arange(2 * NA_WIN_C - 1)) & valid[:, :, None]).astype(np.float32)
    t = jnp.einsum('hrk,cjk->hrcj', rpb.astype(F32), col_sel, precision=lax.Precision.HIGHEST)
    t = t + jnp.where(valid, 0.0, NEG_BIG)[None, None].astype(F32)
    return jnp.concatenate([t[:, :-1], t[:, 1:]], axis=-1)


def _na_kernel(q_ref, k_ref, v_ref, bias_ref, o_ref, *, rows_total):
    r0 = pl.program_id(1) * NA_Q_ROWS
    kstart = jnp.clip(r0 - NA_Q_ROWS, 0, rows_total - NA_KV_ROWS)
    lane = lax.broadcasted_iota(jnp.int32, (GRID_W, 128), 1)
    first_head = lane < NA_HEAD_DIM
    win = NA_WIN_R * GRID_W

    def body(it, carry):
        rows = []
        for sub in range(NA_ROWS_PER_STEP):
            i = it * NA_ROWS_PER_STEP + sub
            r = r0 + i
            rs = jnp.clip(r - NA_WIN_R // 2, 0, rows_total - NA_WIN_R)
            rows.append((pl.multiple_of(i * GRID_W, GRID_W), pl.multiple_of((rs - kstart) * GRID_W, GRID_W), r - rs))
        scores = []
        for qoff, koff, didx in rows:
            for h in range(NA_HEADS):
                cs = slice(128 * (h // 2), 128 * (h // 2 + 1))
                q2 = q_ref[pl.ds(qoff, GRID_W), cs]
                qm = jnp.where(first_head if h % 2 == 0 else jnp.logical_not(first_head), q2, jnp.zeros_like(q2))
                s = lax.dot_general(qm, k_ref[pl.ds(koff, win), cs], (((1,), (1,)), ((), ())),
                                    preferred_element_type=F32)
                bias = [bias_ref[h, 2 * kk + (NA_WIN_R - 1) - didx] for kk in range(NA_WIN_R // 2)]
                scores.append(s + jnp.concatenate(bias, axis=1))
        probs = []
        for s in scores:
            m = jnp.max(s, axis=-1, keepdims=True)
            e = jnp.exp(s - m)
            probs.append((e.astype(BF16), jnp.sum(e, axis=-1, keepdims=True)))
        for n, (qoff, koff, didx) in enumerate(rows):
            outs = []
            for h in range(NA_HEADS):
                e, l = probs[n * NA_HEADS + h]
                cs = slice(128 * (h // 2), 128 * (h // 2 + 1))
                outs.append(jnp.dot(e, v_ref[pl.ds(koff, win), cs], preferred_element_type=F32) / l)
            for p in range(NA_HEADS // 2):
                o_ref[pl.ds(qoff, GRID_W), 128 * p:128 * (p + 1)] = jnp.where(
                    first_head, outs[2 * p], outs[2 * p + 1]).astype(BF16)
        return carry

    lax.fori_loop(0, NA_Q_ROWS // NA_ROWS_PER_STEP, body, 0)


def _na(q, k, v, bias, l):
    B, S, _ = q.shape
    rows = S // GRID_W
    assert rows >= NA_KV_ROWS and rows % NA_Q_ROWS == 0
    qtok = NA_Q_ROWS * GRID_W
    kvtok = NA_KV_ROWS * GRID_W

    def kv_map(b, rb):
        return (b, jnp.clip(rb * NA_Q_ROWS - NA_Q_ROWS, 0, rows - NA_KV_ROWS) * GRID_W, 0)

    kv_spec = pl.BlockSpec((None, pl.Element(kvtok), pl.Element(NA_WIDTH)), kv_map)
    return pl.pallas_call(
        functools.partial(_na_kernel, rows_total=rows),
        grid=(B, rows // NA_Q_ROWS),
        in_specs=[
            pl.BlockSpec((None, qtok, NA_WIDTH), lambda b, rb: (b, rb, 0)),
            kv_spec,
            kv_spec,
            _layer_spec(bias, l, 2),
        ],
        out_specs=pl.BlockSpec((None, qtok, NA_WIDTH), lambda b, rb: (b, rb, 0)),
        out_shape=jax.ShapeDtypeStruct((B, S, NA_WIDTH), BF16),
        compiler_params=_cparams(("parallel", "parallel")),
        name="na",
    )(q, k, v, bias)


def _s5_tables(lam_re, lam_im, log_dt, b_re, b_im, c_re, c_im):
    L = S5_CHUNK
    lr = jnp.minimum(lam_re.astype(F32), -1e-4)
    li = lam_im.astype(F32)
    dt = jnp.exp(log_dt.astype(F32))[:, :, None]
    mag = jnp.exp(lr * dt)
    ar = mag * jnp.cos(li * dt)
    ai = mag * jnp.sin(li * dt)
    den = lr * lr + li * li
    fr = ((ar - 1.0) * lr + ai * li) / den
    fi = (ai * lr - (ar - 1.0) * li) / den
    br = b_re.astype(F32)
    bi = b_im.astype(F32)
    bbr = fr[..., None] * br - fi[..., None] * bi
    bbi = fr[..., None] * bi + fi[..., None] * br
    cr = c_re.astype(F32)
    ci = c_im.astype(F32)
    j = jnp.arange(L + 1, dtype=F32)[:, None, None, None]
    pm = jnp.exp(j * (lr * dt)[None])
    pr = pm * jnp.cos(j * (li * dt)[None])
    pi_ = pm * jnp.sin(j * (li * dt)[None])
    abr = pr[..., None] * bbr[None] - pi_[..., None] * bbi[None]
    abi = pr[..., None] * bbi[None] + pi_[..., None] * bbr[None]
    kern = jnp.einsum('dgcp,jdgpe->jdgce', cr, abr) - jnp.einsum('dgcp,jdgpe->jdgce', ci, abi)
    s_idx = np.arange(L)[:, None]
    t_idx = np.arange(L)[None, :]
    lags = np.arange(L + 1)[:, None, None]
    sel = np.stack([(t_idx - s_idx == lags), (s_idx - t_idx == lags)], axis=1).astype(np.float32)
    tk = jnp.einsum('jdst,jdgce->gsetc', sel, kern, precision=lax.Precision.HIGHEST)
    tmat = tk.reshape(S5_GROUPS, L * S5_GROUP_CH, L * S5_GROUP_CH)
    ef_r = abr[:L, 0][::-1]
    ef_i = abi[:L, 0][::-1]
    eb_r = abr[:L, 1]
    eb_i = abi[:L, 1]

    def to_rows(a):
        return jnp.transpose(a, (1, 0, 3, 2)).reshape(S5_GROUPS, L * S5_GROUP_CH, S5_STATE)

    we = jnp.concatenate([to_rows(ef_r), to_rows(eb_r), to_rows(ef_i), to_rows(eb_i)], axis=-1)
    def readout(d, descending):
        p_r = pr[1:L + 1, d]
        p_i = pi_[1:L + 1, d]
        if descending:
            p_r, p_i = p_r[::-1], p_i[::-1]
        m_re = cr[d][None] * p_r[:, :, None, :] - ci[d][None] * p_i[:, :, None, :]
        m_im = cr[d][None] * p_i[:, :, None, :] + ci[d][None] * p_r[:, :, None, :]
        to_cols = lambda a: jnp.transpose(a, (1, 3, 0, 2)).reshape(S5_GROUPS, S5_STATE, L * S5_GROUP_CH)
        return to_cols(m_re), to_cols(-m_im)

    yf_re, yf_im = readout(0, False)
    yb_re, yb_im = readout(1, True)
    wy = jnp.concatenate([yf_re, yb_re, yf_im, yb_im], axis=1)
    al = jnp.stack([jnp.concatenate([pr[L, 0], pr[L, 1]], axis=-1),
                    jnp.concatenate([pi_[L, 0], pi_[L, 1]], axis=-1)], axis=1)
    return tmat.astype(BF16), we.astype(BF16), wy.astype(BF16), al


def _s5_kernel(u_ref, t_ref, we_ref, wy_ref, al_ref, y_ref, e_ref, xp_ref, *, n_chunks, batch):
    lc = S5_CHUNK * S5_GROUP_CH
    half = 2 * S5_STATE
    slots = S5_BPAD // S5_GROUPS_PER_STEP
    if batch < slots:
        e_ref[...] = jnp.zeros(e_ref.shape, F32)
    for gg in range(S5_GROUPS_PER_STEP):
        u = u_ref[gg].reshape(batch * n_chunks, lc)
        y_ref[gg] = jnp.dot(u, t_ref[gg], preferred_element_type=F32).reshape(batch, n_chunks, lc)
        e = jnp.dot(u, we_ref[gg], preferred_element_type=F32)
        for b in range(batch):
            for hf in range(2):
                e_ref[hf, pl.ds(gg * slots + b, n_chunks, stride=S5_BPAD), :] = e[
                    b * n_chunks:(b + 1) * n_chunks, hf * half:(hf + 1) * half]
    lane = lax.broadcasted_iota(jnp.int32, (S5_BPAD, half), 1)
    sub = lax.broadcasted_iota(jnp.int32, (S5_BPAD, half), 0)
    is_fwd = lane < S5_STATE
    a_re = jnp.broadcast_to(al_ref[0, 0:1, :], (S5_BPAD, half))
    a_im = jnp.broadcast_to(al_ref[0, 1:2, :], (S5_BPAD, half))
    for gg in range(1, S5_GROUPS_PER_STEP):
        a_re = jnp.where(sub >= gg * slots, al_ref[gg, 0:1, :], a_re)
        a_im = jnp.where(sub >= gg * slots, al_ref[gg, 1:2, :], a_im)

    def body(j, carry):
        xr, xi = carry
        rf = pl.multiple_of(j * S5_BPAD, S5_BPAD)
        rb = pl.multiple_of((n_chunks - 1 - j) * S5_BPAD, S5_BPAD)
        xp_ref[pl.ds(rf, S5_BPAD), 0:S5_STATE] = xr[:, 0:S5_STATE]
        xp_ref[pl.ds(rb, S5_BPAD), S5_STATE:half] = xr[:, S5_STATE:half]
        xp_ref[pl.ds(rf, S5_BPAD), half:half + S5_STATE] = xi[:, 0:S5_STATE]
        xp_ref[pl.ds(rb, S5_BPAD), half + S5_STATE:2 * half] = xi[:, S5_STATE:half]
        er = jnp.where(is_fwd, e_ref[0, pl.ds(rf, S5_BPAD), :], e_ref[0, pl.ds(rb, S5_BPAD), :])
        ei = jnp.where(is_fwd, e_ref[1, pl.ds(rf, S5_BPAD), :], e_ref[1, pl.ds(rb, S5_BPAD), :])
        nxr = a_re * xr - a_im * xi + er
        nxi = a_re * xi + a_im * xr + ei
        return nxr, nxi

    zero = jnp.zeros((S5_BPAD, half), F32)
    lax.fori_loop(0, n_chunks, body, (zero, zero))
    xp = xp_ref[...].astype(BF16)
    for gg in range(S5_GROUPS_PER_STEP):
        yi = jnp.dot(xp, wy_ref[gg], preferred_element_type=F32)
        e_ref[0] = yi[:, :half]
        e_ref[1] = yi[:, half:]
        for b in range(batch):
            y_ref[gg, b] += jnp.concatenate(
                [e_ref[hf, pl.ds(gg * slots + b, n_chunks, stride=S5_BPAD), :] for hf in range(2)], axis=1)


def _s5(ug, tabs, l):
    tmat, we, wy, al = tabs
    G, B, n, lc = ug.shape
    gs = S5_GROUPS_PER_STEP
    assert B * gs <= S5_BPAD and lc == 4 * S5_STATE and G % gs == 0
    rows = n * S5_BPAD
    grp = lambda shape: pl.BlockSpec((gs,) + shape, lambda g: (g,) + (0,) * len(shape))
    tab = lambda shape: pl.BlockSpec((None, gs) + shape, lambda g: (l, g) + (0,) * len(shape))
    return pl.pallas_call(
        functools.partial(_s5_kernel, n_chunks=n, batch=B),
        grid=(G // gs,),
        in_specs=[grp((B, n, lc)), tab((lc, lc)), tab((lc, 4 * S5_STATE)), tab((4 * S5_STATE, lc)),
                  tab((2, 2 * S5_STATE))],
        out_specs=grp((B, n, lc)),
        out_shape=jax.ShapeDtypeStruct((G, B, n, lc), F32),
        scratch_shapes=[pltpu.VMEM((2, rows, 2 * S5_STATE), F32), pltpu.VMEM((rows, 4 * S5_STATE), F32)],
        compiler_params=_cparams(("parallel",)),
        name="s5",
    )(ug, tmat, we, wy, al)


def _memkv_kernel(m_ref, g_ref, w_ref, o_ref):
    x = m_ref[...]
    ms = jnp.mean(x * x, axis=-1, keepdims=True)
    h = (x * lax.rsqrt(ms + RMS_EPS) * g_ref[...]).astype(BF16)
    o_ref[...] = jnp.dot(h, w_ref[...], preferred_element_type=F32).astype(BF16)


def _memkv(mem, norm_mem, w_bf):
    B, M, D = mem.shape
    L = w_bf.shape[0]
    return pl.pallas_call(
        _memkv_kernel,
        grid=(L, B),
        in_specs=[pl.BlockSpec((None, M, D), lambda l, b: (b, 0, 0)),
                  pl.BlockSpec((None, 1, D), lambda l, b: (l, 0, 0)),
                  pl.BlockSpec((None, D, 2 * XA_WIDTH), lambda l, b: (l, 0, 0))],
        out_specs=pl.BlockSpec((None, None, M, 2 * XA_WIDTH), lambda l, b: (l, b, 0, 0)),
        out_shape=jax.ShapeDtypeStruct((L, B, M, 2 * XA_WIDTH), BF16),
        compiler_params=_cparams(("parallel", "parallel")),
        name="memkv",
    )(mem, norm_mem.reshape(L, 1, D), w_bf)


def _gelu_tanh(x):
    return 0.5 * x * (1.0 + jnp.tanh(math.sqrt(2.0 / math.pi) * (x + 0.044715 * (x * x * x))))


def _router(z):
    lane = lax.broadcasted_iota(jnp.int32, z.shape, 1).astype(F32)
    ninf = jnp.full_like(z, -jnp.inf)
    far = jnp.full_like(z, 1e9)
    cm = (lane >= MOE_EXPERTS) & (lane < MOE_EXPERTS + MOE_GROUPS)
    cmax = jnp.max(jnp.where(cm, z, ninf), axis=-1, keepdims=True)
    glane = jnp.min(jnp.where(cm & (z == cmax), lane, far), axis=-1, keepdims=True)
    psum = jnp.sum(jnp.where(cm, jnp.exp(jnp.where(cm, z, cmax) - cmax), 0.0), axis=-1, keepdims=True)
    p_grp = 1.0 / psum
    f0 = (glane - MOE_EXPERTS) * MOE_EPG
    fm = (lane >= f0) & (lane < f0 + MOE_EPG)
    v1 = jnp.max(jnp.where(fm, z, ninf), axis=-1, keepdims=True)
    i1 = jnp.min(jnp.where(fm & (z == v1), lane, far), axis=-1, keepdims=True)
    fm2 = fm & (lane != i1)
    v2 = jnp.max(jnp.where(fm2, z, ninf), axis=-1, keepdims=True)
    i2 = jnp.min(jnp.where(fm2 & (z == v2), lane, far), axis=-1, keepdims=True)
    t = jnp.exp(v2 - v1)
    w1 = p_grp / (1.0 + t)
    w2 = p_grp * t / (1.0 + t)
    comb = jnp.where(lane == i1, w1, 0.0) + jnp.where(lane == i2, w2, 0.0)
    a = jnp.minimum(i1, i2) - f0
    b = jnp.maximum(i1, i2) - f0
    pair = a * (7.0 - a) * 0.5 + b - a - 1.0
    return comb, (glane - MOE_EXPERTS) * MOE_PAIRS + pair


def _merge_kernel(x_ref, ona_ref, ys5_ref, u_ref, qx_ref, gate_ref, mkv_ref,
                  d_ref, wglu_ref, wna_ref, ws5_ref, wxa_ref, wout_ref, nffn_ref, wrh_ref, wrl_ref, br_ref,
                  xo_ref, hp_ref, meta_ref, cnt_ref, mrg_ref, ys_ref):
    tm = x_ref.shape[0]
    _group_to_time_major(ys5_ref, ys_ref, tm // S5_CHUNK)
    ys = jnp.concatenate([ys_ref[0], ys_ref[1]], axis=1)
    y = _gelu_tanh(ys + d_ref[...] * u_ref[...])
    g = jnp.dot(y.astype(BF16), wglu_ref[...], preferred_element_type=F32)
    os5 = (g[:, :S5_WIDTH] * jax.nn.sigmoid(g[:, S5_WIDTH:])).astype(BF16)
    lane = lax.broadcasted_iota(jnp.int32, (tm, 128), 1)
    first_head = lane < XA_HEAD_DIM
    oxa = []
    for p in range(XA_HEADS // 2):
        cs = slice(128 * p, 128 * (p + 1))
        q2 = qx_ref[:, cs]
        k2 = mkv_ref[:, cs]
        v2 = mkv_ref[:, XA_WIDTH + 128 * p:XA_WIDTH + 128 * (p + 1)]
        outs = []
        for hh in range(2):
            qm = jnp.where(first_head if hh == 0 else jnp.logical_not(first_head), q2, jnp.zeros_like(q2))
            s = lax.dot_general(qm, k2, (((1,), (1,)), ((), ())), preferred_element_type=F32)
            m = jnp.max(s, axis=-1, keepdims=True)
            e = jnp.exp(s - m)
            l = jnp.sum(e, axis=-1, keepdims=True)
            outs.append(jnp.dot(e.astype(BF16), v2, preferred_element_type=F32) / l)
        oxa.append(jnp.where(first_head, outs[0], outs[1]).astype(BF16))
    oxa = jnp.concatenate(oxa, axis=-1)
    ona = ona_ref[...]
    cw = 256
    for c in range(D_MODEL // cw):
        cs = slice(cw * c, cw * (c + 1))
        m = gate_ref[:, cw * c:cw * (c + 1)].astype(F32) * jnp.dot(ona, wna_ref[:, cs], preferred_element_type=F32)
        m += gate_ref[:, D_MODEL + cw * c:D_MODEL + cw * (c + 1)].astype(F32) * jnp.dot(
            os5, ws5_ref[:, cs], preferred_element_type=F32)
        m += gate_ref[:, 2 * D_MODEL + cw * c:2 * D_MODEL + cw * (c + 1)].astype(F32) * jnp.dot(
            oxa, wxa_ref[:, cs], preferred_element_type=F32)
        mrg_ref[:, cs] = m.astype(BF16)
    xn = x_ref[...] + jnp.dot(mrg_ref[...], wout_ref[...], preferred_element_type=F32)
    xo_ref[...] = xn
    ms = jnp.mean(xn * xn, axis=-1, keepdims=True)
    h2 = xn * lax.rsqrt(ms + RMS_EPS) * nffn_ref[...]
    hi = h2.astype(BF16)
    hi32 = hi.astype(F32)
    lo = (h2 - hi32).astype(BF16)
    z = (jnp.dot(hi, wrh_ref[...], preferred_element_type=F32)
         + jnp.dot(lo, wrh_ref[...], preferred_element_type=F32)
         + jnp.dot(hi, wrl_ref[...], preferred_element_type=F32)) + br_ref[...]
    hw = D_MODEL // 2
    packed = pltpu.bitcast(hi32[:, :hw], jnp.uint32) | (pltpu.bitcast(hi32[:, hw:], jnp.uint32) >> 16)
    for j in range(hw // 128):
        hp_ref[pl.ds(j, tm, stride=TOKEN_REC_ROWS), :] = packed[:, 128 * j:128 * (j + 1)]
    comb, gid = _router(z)
    hp_ref[pl.ds(hw // 128, tm, stride=TOKEN_REC_ROWS), :] = pltpu.bitcast(comb, jnp.uint32)
    for j in range(hw // 128 + 1, TOKEN_REC_ROWS):
        hp_ref[pl.ds(j, tm, stride=TOKEN_REC_ROWS), :] = jnp.zeros((tm, 128), jnp.uint32)
    lanef = lane.astype(F32)
    onehot = lanef == gid
    row = lax.broadcasted_iota(jnp.int32, (tm, tm), 0)
    col = lax.broadcasted_iota(jnp.int32, (tm, tm), 1)
    tri = jnp.where(col <= row, 1.0, 0.0).astype(BF16)
    csum = jnp.dot(tri, jnp.where(onehot, 1.0, 0.0).astype(BF16), preferred_element_type=F32)
    rank = jnp.sum(jnp.where(onehot, csum, 0.0), axis=-1, keepdims=True) - 1.0
    meta = jnp.where(lane == 0, gid, jnp.where(lane == 1, rank, 0.0))
    meta_ref[...] = jnp.transpose(meta)[:8, :]
    cnt_ref[...] = jnp.broadcast_to(csum[tm - 1:tm, :], (8, 128))


def _merge(x, ona, ys5, u, qx, gates, mkv, params, l):
    B, S, D = x.shape
    tm = min(TOKEN_TILE, S)
    M = mkv.shape[2]
    n_i = S // tm

    def tok(width):
        return pl.BlockSpec((None, tm, width), lambda b, i: (b, i, 0))

    return pl.pallas_call(
        _merge_kernel,
        grid=(B, n_i),
        in_specs=[tok(D), tok(NA_WIDTH),
                  pl.BlockSpec((S5_GROUPS, None, tm // S5_CHUNK, S5_CHUNK * S5_GROUP_CH), lambda b, i: (0, b, i, 0)),
                  tok(S5_WIDTH), tok(XA_WIDTH), tok(N_BRANCHES * D),
                  pl.BlockSpec((None, None, M, 2 * XA_WIDTH), lambda b, i: (l, b, 0, 0))]
                 + [_layer_spec(p, l, 2) for p in params],
        out_specs=[tok(D), pl.BlockSpec((None, tm * TOKEN_REC_ROWS, 128), lambda b, i: (b, i, 0)),
                   pl.BlockSpec((None, None, 8, tm), lambda b, i: (b, i, 0, 0)),
                   pl.BlockSpec((None, None, 8, 128), lambda b, i: (b, i, 0, 0))],
        out_shape=[jax.ShapeDtypeStruct((B, S, D), F32),
                   jax.ShapeDtypeStruct((B, S * TOKEN_REC_ROWS, 128), jnp.uint32),
                   jax.ShapeDtypeStruct((B, n_i, 8, tm), F32),
                   jax.ShapeDtypeStruct((B, n_i, 8, 128), F32)],
        scratch_shapes=[pltpu.VMEM((tm, D), BF16), pltpu.VMEM((2, tm, 128), F32)],
        compiler_params=_cparams(("parallel", "parallel")),
        name="merge",
    )(x, ona, ys5, u, qx, gates, mkv, *params)


def _moe_plan(meta, cnt):
    B, n_i, _, tm = meta.shape
    n_tokens = B * n_i * tm
    gid = meta[:, :, 0, :].reshape(B * n_i, tm).astype(jnp.int32)
    rank = meta[:, :, 1, :].reshape(B * n_i, tm).astype(jnp.int32)
    tile_counts = cnt[:, :, 0, :MOE_BINS].reshape(B * n_i, MOE_BINS).astype(jnp.int32)
    counts = jnp.sum(tile_counts, axis=0)
    padded = ((counts + MOE_TILE - 1) // MOE_TILE) * MOE_TILE
    ends = jnp.cumsum(padded)
    tile_base = (ends - padded)[None, :] + jnp.cumsum(tile_counts, axis=0) - tile_counts
    pos = rank
    for g in range(MOE_BINS):
        pos = pos + jnp.where(gid == g, tile_base[:, g:g + 1], 0)
    pos = pos.reshape(n_tokens)
    n_tiles = n_tokens // MOE_TILE + MOE_BINS
    starts = jnp.arange(n_tiles, dtype=jnp.int32) * MOE_TILE
    tile_bin = jnp.minimum(jnp.sum(starts[:, None] >= ends[None, :], axis=1), MOE_BINS - 1).astype(jnp.int32)
    first = (tile_bin // MOE_PAIRS) * MOE_EPG
    pair = tile_bin % MOE_PAIRS
    tile_e1 = first + jnp.asarray(MOE_PAIR_A, jnp.int32)[pair]
    tile_e2 = first + jnp.asarray(MOE_PAIR_B, jnp.int32)[pair]
    return pos.astype(jnp.int32), ends.astype(jnp.int32), padded.astype(jnp.int32), tile_e1, tile_e2


def _rec(ref, token, count=1):
    return ref.at[pl.ds(pl.multiple_of(token * TOKEN_REC_ROWS, TOKEN_REC_ROWS), count * TOKEN_REC_ROWS), :]


DMA_UNROLL = 8


def _issue_record_dmas(count, copy_of):
    def body(blk, c):
        for k in range(DMA_UNROLL):
            copy_of(blk * DMA_UNROLL + k).start(priority=k % 2)
        return c
    lax.fori_loop(0, count // DMA_UNROLL, body, 0)


def _permute_kernel(ends_ref, padded_ref, pos_ref, h_ref, hs_ref, zero_ref, sem):
    tm = h_ref.shape[0] // TOKEN_REC_ROWS

    @pl.when(pl.program_id(0) == 0)
    def _():
        zero_ref[...] = jnp.zeros(zero_ref.shape, zero_ref.dtype)
        n_rows = hs_ref.shape[0] // TOKEN_REC_ROWS
        for g in range(MOE_BINS):
            tail = ends_ref[MOE_BINS - 1] + g * MOE_TILE
            for cond, start in ((padded_ref[g] > 0, ends_ref[g] - MOE_TILE), (tail < n_rows, tail)):
                @pl.when(cond)
                def _():
                    cp = pltpu.make_async_copy(zero_ref, _rec(hs_ref, start, MOE_TILE), sem)
                    cp.start()
                    cp.wait()

    _issue_record_dmas(tm, lambda r: pltpu.make_async_copy(_rec(h_ref, r), _rec(hs_ref, pos_ref[0, 0, r]), sem))
    pltpu.make_async_copy(h_ref, _rec(hs_ref, 0, tm), sem).wait()


def _moe_permute(hp2, pos3, ends, padded, n_rows):
    T = hp2.shape[0] // TOKEN_REC_ROWS
    tm = min(TOKEN_TILE, T)
    return pl.pallas_call(
        _permute_kernel,
        grid_spec=pltpu.PrefetchScalarGridSpec(
            num_scalar_prefetch=2,
            grid=(T // tm,),
            in_specs=[pl.BlockSpec((1, 1, tm), lambda i, e, p: (i, 0, 0), memory_space=pltpu.SMEM),
                      pl.BlockSpec((tm * TOKEN_REC_ROWS, 128), lambda i, e, p: (i, 0))],
            out_specs=pl.BlockSpec(memory_space=pl.ANY),
            scratch_shapes=[pltpu.VMEM((MOE_TILE * TOKEN_REC_ROWS, 128), jnp.uint32), pltpu.SemaphoreType.DMA(())],
        ),
        out_shape=jax.ShapeDtypeStruct((n_rows * TOKEN_REC_ROWS, 128), jnp.uint32),
        compiler_params=_cparams(("arbitrary",)),
        name="moe_permute",
    )(ends, padded, pos3, hp2)


def _moe_kernel(e1_ref, e2_ref, h_ref, wg1_ref, wu1_ref, wd1_ref, wg2_ref, wu2_ref, wd2_ref, o_ref):
    hw = D_MODEL // 2

    def field(j):
        return h_ref[pl.ds(j, MOE_TILE, stride=TOKEN_REC_ROWS), :]

    w = jnp.concatenate([field(j) for j in range(hw // 128)], axis=1)
    h = jnp.concatenate([pltpu.bitcast(w & jnp.uint32(0xFFFF0000), F32).astype(BF16),
                         pltpu.bitcast(w << 16, F32).astype(BF16)], axis=1)
    comb = pltpu.bitcast(field(hw // 128), F32)
    lane = lax.broadcasted_iota(jnp.int32, comb.shape, 1)
    i = pl.program_id(0)
    out = None
    for e_ref, wg_ref, wu_ref, wd_ref in ((e1_ref, wg1_ref, wu1_ref, wd1_ref), (e2_ref, wg2_ref, wu2_ref, wd2_ref)):
        c = jnp.sum(jnp.where(lane == e_ref[i], comb, 0.0), axis=-1, keepdims=True)
        g = jnp.dot(h, wg_ref[...], preferred_element_type=F32)
        u = jnp.dot(h, wu_ref[...], preferred_element_type=F32)
        a = (g * jax.nn.sigmoid(g) * u * c).astype(BF16)
        part = jnp.dot(a, wd_ref[...], preferred_element_type=F32)
        out = part if out is None else out + part
    for j in range(TOKEN_REC_ROWS):
        o_ref[pl.ds(j, MOE_TILE, stride=TOKEN_REC_ROWS), :] = out[:, 128 * j:128 * (j + 1)]


def _moe_experts(hs, tile_e1, tile_e2, wg, wu, wd, l):
    n_rows = hs.shape[0] // TOKEN_REC_ROWS
    n_tiles = n_rows // MOE_TILE
    rec_tile = pl.BlockSpec((MOE_TILE * TOKEN_REC_ROWS, 128), lambda i, e1, e2: (i, 0))

    def expert(shape, which):
        if which == 0:
            return pl.BlockSpec((None, None) + shape, lambda i, e1, e2: (l, e1[i], 0, 0))
        return pl.BlockSpec((None, None) + shape, lambda i, e1, e2: (l, e2[i], 0, 0))

    up, down = (D_MODEL, MOE_FF), (MOE_FF, D_MODEL)
    return pl.pallas_call(
        _moe_kernel,
        grid_spec=pltpu.PrefetchScalarGridSpec(
            num_scalar_prefetch=2,
            grid=(n_tiles,),
            in_specs=[rec_tile, expert(up, 0), expert(up, 0), expert(down, 0),
                      expert(up, 1), expert(up, 1), expert(down, 1)],
            out_specs=rec_tile,
        ),
        out_shape=jax.ShapeDtypeStruct((n_rows * TOKEN_REC_ROWS, 128), F32),
        compiler_params=_cparams(("arbitrary",)),
        name="moe_experts",
    )(tile_e1, tile_e2, hs, wg, wu, wd, wg, wu, wd)


def _final_kernel(pos_ref, x_ref, ys_ref, fn_ref, o_ref, buf_ref, sem):
    tm = x_ref.shape[0]
    _issue_record_dmas(tm, lambda r: pltpu.make_async_copy(_rec(ys_ref, pos_ref[0, 0, r]), _rec(buf_ref, r), sem))
    pltpu.make_async_copy(_rec(ys_ref, 0, tm), buf_ref, sem).wait()
    moe = jnp.concatenate([buf_ref[pl.ds(j, tm, stride=TOKEN_REC_ROWS), :] for j in range(TOKEN_REC_ROWS)], axis=1)
    xn = x_ref[...] + moe
    ms = jnp.mean(xn * xn, axis=-1, keepdims=True)
    o_ref[...] = xn * lax.rsqrt(ms + RMS_EPS) * fn_ref[...]


def _final(x2, ys, pos3, fnorm):
    T, D = x2.shape
    tm = min(TOKEN_TILE, T)
    return pl.pallas_call(
        _final_kernel,
        grid=(T // tm,),
        in_specs=[pl.BlockSpec((1, 1, tm), lambda i: (i, 0, 0), memory_space=pltpu.SMEM),
                  pl.BlockSpec((tm, D), lambda i: (i, 0)),
                  pl.BlockSpec(memory_space=pl.ANY),
                  pl.BlockSpec((1, D), lambda i: (0, 0))],
        out_specs=pl.BlockSpec((tm, D), lambda i: (i, 0)),
        out_shape=jax.ShapeDtypeStruct((T, D), F32),
        scratch_shapes=[pltpu.VMEM((tm * TOKEN_REC_ROWS, 128), F32), pltpu.SemaphoreType.DMA(())],
        compiler_params=_cparams(("arbitrary",)),
        name="final_unpermute_norm",
    )(pos3, x2, ys, fnorm.reshape(1, D))


def _moe(hp, meta, cnt, wg, wu, wd, l):
    T = hp.shape[0] * hp.shape[1] // TOKEN_REC_ROWS
    tm = min(TOKEN_TILE, T)
    pos, ends, padded, tile_e1, tile_e2 = _moe_plan(meta, cnt)
    pos3 = pos.reshape(T // tm, 1, tm)
    n_rows = T + MOE_BINS * MOE_TILE
    hs = _moe_permute(hp.reshape(T * TOKEN_REC_ROWS, 128), pos3, ends, padded, n_rows)
    return _moe_experts(hs, tile_e1, tile_e2, wg, wu, wd, l), pos3


def kernel(x, mem, norm_mix, norm_ffn, norm_mem, w_in, b_gate, na_rpb, s5_lam_re, s5_lam_im, s5_log_dt, s5_b_re, s5_b_im, s5_c_re, s5_c_im, s5_d, s5_w_glu, w_mem_kv, w_br_na, w_br_s5, w_br_xa, w_out, moe_w_coarse, moe_b_coarse, moe_w_fine, moe_b_fine, moe_w_gate, moe_w_up, moe_w_down, final_norm):
    depth, D = w_in.shape[0], w_in.shape[1]
    mkv_all = _memkv(mem, norm_mem, w_mem_kv.astype(BF16))
    w_in_bf = w_in.astype(BF16)
    norm_mix3 = norm_mix.reshape(depth, 1, D)
    b_gate3 = b_gate.reshape(depth, 1, N_BRANCHES * D)
    na_bias = jax.vmap(_na_bias_table)(na_rpb)
    s5_tabs = jax.vmap(_s5_tables)(s5_lam_re, s5_lam_im, s5_log_dt, s5_b_re, s5_b_im, s5_c_re, s5_c_im)
    wr = jnp.concatenate([moe_w_fine, moe_w_coarse], axis=2).astype(F32)
    wr = jnp.pad(wr, ((0, 0), (0, 0), (0, ROUTER_LANES - wr.shape[2])))
    wrh = wr.astype(BF16)
    wrl = (wr - wrh.astype(F32)).astype(BF16)
    br = jnp.pad(jnp.concatenate([moe_b_fine, moe_b_coarse], axis=1).astype(F32),
                 ((0, 0), (0, ROUTER_LANES - MOE_EXPERTS - MOE_GROUPS))).reshape(depth, 1, ROUTER_LANES)
    merge_params = (s5_d.reshape(depth, 1, S5_WIDTH), s5_w_glu.astype(BF16), w_br_na.astype(BF16),
                    w_br_s5.astype(BF16), w_br_xa.astype(BF16), w_out.astype(BF16),
                    norm_ffn.reshape(depth, 1, D), wrh, wrl, br)
    wg, wu, wd = moe_w_gate.astype(BF16), moe_w_up.astype(BF16), moe_w_down.astype(BF16)
    moe = None
    for l in range(depth):
        if moe is None:
            q, k, v, u, ug, qx, gates = _inproj(x, norm_mix3, w_in_bf, b_gate3, l)
        else:
            q, k, v, u, ug, qx, gates, x = _inproj(x, norm_mix3, w_in_bf, b_gate3, l, moe)
        ona = _na(q, k, v, na_bias, l)
        ys5 = _s5(ug, s5_tabs, l)
        x, hp, meta, cnt = _merge(x, ona, ys5, u, qx, gates, mkv_all, merge_params, l)
        moe = _moe(hp, meta, cnt, wg, wu, wd, l)
    B, S, _ = x.shape
    ys, pos3 = moe
    return _final(x.reshape(B * S, D), ys, pos3, final_norm).reshape(B, S, D)
```

```python
import functools
import math

import numpy as np
import jax
import jax.numpy as jnp
from jax import lax
from jax.experimental import pallas as pl
from jax.experimental.pallas import tpu as pltpu

F32 = jnp.float32
BF16 = jnp.bfloat16

D_MODEL = 1024
GRID_W = 64
RMS_EPS = 1e-6
NA_HEADS = 8
NA_HEAD_DIM = 64
NA_WIDTH = NA_HEADS * NA_HEAD_DIM
NA_WIN_R = 8
NA_WIN_C = 16
NA_KV_ROWS = 24
NA_Q_ROWS = 8
NA_ROWS_PER_STEP = 8
S5_GROUPS = 16
S5_GROUP_CH = 16
S5_WIDTH = S5_GROUPS * S5_GROUP_CH
S5_STATE = 64
S5_CHUNK = 16
S5_BPAD = 8
S5_GROUPS_PER_STEP = 2
XA_HEADS = 4
XA_HEAD_DIM = 64
XA_WIDTH = XA_HEADS * XA_HEAD_DIM
N_BRANCHES = 3
MOE_GROUPS = 4
MOE_EPG = 4
MOE_EXPERTS = MOE_GROUPS * MOE_EPG
MOE_FF = 256
MOE_ECHUNK = 4
ROUTER_LANES = 128
TOKEN_REC_ROWS = 8
MOE_PAIR_A = (0, 0, 0, 1, 1, 2)
MOE_PAIR_B = (1, 2, 3, 2, 3, 3)
MOE_PAIRS = len(MOE_PAIR_A)
MOE_BINS = MOE_GROUPS * MOE_PAIRS
MOE_TILE = 256
NEG_BIG = -1e30

VMEM_LIMIT = 52 * 1024 * 1024
TOKEN_TILE = 512


def _cparams(sem):
    return pltpu.CompilerParams(dimension_semantics=sem, vmem_limit_bytes=VMEM_LIMIT)


def _lane_block_mask(rows, j):
    lane = lax.broadcasted_iota(jnp.int32, (rows, 128), 1)
    lo = S5_GROUP_CH * (j % 8)
    return (lane >= lo) & (lane < lo + S5_GROUP_CH)


def _time_to_group_major(src_ref, rows):
    dest = [[None, None] for _ in range(S5_GROUPS)]
    for s in range(S5_CHUNK):
        halves = tuple(src_ref[hf, pl.ds(s, rows, stride=S5_CHUNK), :] for hf in range(2))
        m = _lane_block_mask(rows, s)
        for g in range(S5_GROUPS):
            shift = (S5_GROUP_CH * ((s % 8) - (g % 8))) % 128
            r = pltpu.roll(halves[g // 8], shift, axis=1) if shift else halves[g // 8]
            prev = dest[g][s // 8]
            dest[g][s // 8] = jnp.where(m, r, 0.0 if prev is None else prev)
    return [jnp.concatenate(d, axis=1) for d in dest]


def _group_to_time_major(src_ref, dst_ref, rows):
    for t in range(S5_CHUNK):
        out = [None, None]
        for g in range(S5_GROUPS):
            src = src_ref[g, :, 128 * (t // 8):128 * (t // 8 + 1)]
            shift = (S5_GROUP_CH * ((g % 8) - (t % 8))) % 128
            r = pltpu.roll(src, shift, axis=1) if shift else src
            prev = out[g // 8]
            out[g // 8] = jnp.where(_lane_block_mask(rows, g), r, 0.0 if prev is None else prev)
        for hf in range(2):
            dst_ref[hf, pl.ds(t, rows, stride=S5_CHUNK), :] = out[hf]


def _inproj_kernel(*refs, with_moe):
    if with_moe:
        (pos_ref, pos_next_ref, x_ref, ys_ref, g_ref, w_ref, bg_ref,
         q_ref, k_ref, v_ref, u_ref, ug_ref, qx_ref, gate_ref, xo_ref, uh_ref, buf_ref, sem) = refs
        tm = x_ref.shape[0]
        step = pl.program_id(0) * pl.num_programs(1) + pl.program_id(1)
        n_steps = pl.num_programs(0) * pl.num_programs(1)
        slot = lax.rem(step, 2)

        def fetch(p_ref, s):
            _issue_record_dmas(tm, lambda r: pltpu.make_async_copy(
                _rec(ys_ref, p_ref[0, 0, r]), _rec(buf_ref.at[s], r), sem.at[s]))

        @pl.when(step == 0)
        def _():
            fetch(pos_ref, 0)

        @pl.when(step + 1 < n_steps)
        def _():
            fetch(pos_next_ref, 1 - slot)

        pltpu.make_async_copy(_rec(ys_ref, 0, tm), buf_ref.at[slot], sem.at[slot]).wait()
        moe = jnp.concatenate([buf_ref[slot, pl.ds(j, tm, stride=TOKEN_REC_ROWS), :]
                               for j in range(TOKEN_REC_ROWS)], axis=1)
        x = x_ref[...] + moe
        xo_ref[...] = x
    else:
        x_ref, g_ref, w_ref, bg_ref, q_ref, k_ref, v_ref, u_ref, ug_ref, qx_ref, gate_ref, uh_ref = refs
        x = x_ref[...]
    ms = jnp.mean(x * x, axis=-1, keepdims=True)
    h = (x * lax.rsqrt(ms + RMS_EPS) * g_ref[...]).astype(BF16)

    def proj(a, b):
        return jnp.dot(h, w_ref[:, a:b], preferred_element_type=F32)

    c0 = NA_WIDTH
    q_ref[...] = (proj(0, c0) * (NA_HEAD_DIM ** -0.5)).astype(BF16)
    k_ref[...] = proj(c0, 2 * c0).astype(BF16)
    v_ref[...] = proj(2 * c0, 3 * c0).astype(BF16)
    c1 = 3 * c0
    u = proj(c1, c1 + S5_WIDTH)
    u_ref[...] = u
    uh_ref[0] = u[:, :128]
    uh_ref[1] = u[:, 128:]
    for g, ug in enumerate(_time_to_group_major(uh_ref, ug_ref.shape[1])):
        ug_ref[g] = ug.astype(BF16)
    c2 = c1 + S5_WIDTH
    qx_ref[...] = (proj(c2, c2 + XA_WIDTH) * (XA_HEAD_DIM ** -0.5)).astype(BF16)
    c3 = c2 + XA_WIDTH
    for j in range(N_BRANCHES):
        z = proj(c3 + D_MODEL * j, c3 + D_MODEL * (j + 1)) + bg_ref[:, D_MODEL * j:D_MODEL * (j + 1)]
        gate_ref[:, D_MODEL * j:D_MODEL * (j + 1)] = jax.nn.sigmoid(z).astype(BF16)


def _layer_spec(a, l, n_grid):
    zeros = (0,) * (a.ndim - 1)
    if n_grid == 1:
        index_map = lambda i: (l,) + zeros
    elif n_grid == 2:
        index_map = lambda b, i: (l,) + zeros
    else:
        raise ValueError(n_grid)
    return pl.BlockSpec((None,) + a.shape[1:], index_map, pipeline_mode=pl.Buffered(1))


def _inproj(x, g, w_bf, bg, l, moe=None):
    B, S, D = x.shape
    tm = min(TOKEN_TILE, S)
    n_i = S // tm

    def tok(width):
        return pl.BlockSpec((None, tm, width), lambda b, i: (b, i, 0))

    in_specs = [tok(D), _layer_spec(g, l, 2), _layer_spec(w_bf, l, 2), _layer_spec(bg, l, 2)]
    operands = [x, g, w_bf, bg]
    out_specs = [tok(NA_WIDTH), tok(NA_WIDTH), tok(NA_WIDTH), tok(S5_WIDTH),
                 pl.BlockSpec((S5_GROUPS, None, tm // S5_CHUNK, S5_CHUNK * S5_GROUP_CH), lambda b, i: (0, b, i, 0)),
                 tok(XA_WIDTH), tok(N_BRANCHES * D)]
    out_shape = [
        jax.ShapeDtypeStruct((B, S, NA_WIDTH), BF16),
        jax.ShapeDtypeStruct((B, S, NA_WIDTH), BF16),
        jax.ShapeDtypeStruct((B, S, NA_WIDTH), BF16),
        jax.ShapeDtypeStruct((B, S, S5_WIDTH), F32),
        jax.ShapeDtypeStruct((S5_GROUPS, B, S // S5_CHUNK, S5_CHUNK * S5_GROUP_CH), BF16),
        jax.ShapeDtypeStruct((B, S, XA_WIDTH), BF16),
        jax.ShapeDtypeStruct((B, S, N_BRANCHES * D), BF16),
    ]
    scratch = [pltpu.VMEM((2, tm, 128), F32)]
    if moe is not None:
        ys, pos3 = moe
        last = B * n_i - 1
        pos_spec = lambda nxt: pl.BlockSpec(
            (1, 1, tm), lambda b, i: (jnp.minimum(b * n_i + i + nxt, last), 0, 0), memory_space=pltpu.SMEM)
        in_specs = [pos_spec(0), pos_spec(1), in_specs[0], pl.BlockSpec(memory_space=pl.ANY)] + in_specs[1:]
        operands = [pos3, pos3, x, ys] + operands[1:]
        out_specs.append(tok(D))
        out_shape.append(jax.ShapeDtypeStruct((B, S, D), F32))
        scratch += [pltpu.VMEM((2, tm * TOKEN_REC_ROWS, 128), F32), pltpu.SemaphoreType.DMA((2,))]
    return pl.pallas_call(
        functools.partial(_inproj_kernel, with_moe=moe is not None),
        grid=(B, n_i),
        in_specs=in_specs,
        out_specs=out_specs,
        out_shape=out_shape,
        scratch_shapes=scratch,
        compiler_params=_cparams(("arbitrary", "arbitrary") if moe is not None else ("parallel", "parallel")),
        name="inproj",
    )(*operands)


def _na_bias_table(rpb):
    cols = np.arange(GRID_W)
    col_start = np.clip(cols - NA_WIN_C // 2, 0, GRID_W - NA_WIN_C)
    cj = np.arange(GRID_W)[None, :]
    valid = (cj >= col_start[:, None]) & (cj < col_start[:, None] + NA_WIN_C)
    col_idx = np.clip(cj - cols[:, None] + (NA_WIN_C - 1), 0, 2 * NA_WIN_C - 2)
    col_sel = ((col_idx[:, :, None] == np.arange(2 * NA_WIN_C - 1)) & valid[:, :, None]).astype(np.float32)
    t = jnp.einsum('hrk,cjk->hrcj', rpb.astype(F32), col_sel, precision=lax.Precision.HIGHEST)
    t = t + jnp.where(valid, 0.0, NEG_BIG)[None, None].astype(F32)
    return jnp.concatenate([t[:, :-1], t[:, 1:]], axis=-1)


def _na_kernel(q_ref, k_ref, v_ref, bias_ref, o_ref, *, rows_total):
    r0 = pl.program_id(1) * NA_Q_ROWS
    kstart = jnp.clip(r0 - NA_Q_ROWS, 0, rows_total - NA_KV_ROWS)
    lane = lax.broadcasted_iota(jnp.int32, (GRID_W, 128), 1)
    first_head = lane < NA_HEAD_DIM
    win = NA_WIN_R * GRID_W

    def body(it, carry):
        rows = []
        for sub in range(NA_ROWS_PER_STEP):
            i = it * NA_ROWS_PER_STEP + sub
            r = r0 + i
            rs = jnp.clip(r - NA_WIN_R // 2, 0, rows_total - NA_WIN_R)
            rows.append((pl.multiple_of(i * GRID_W, GRID_W), pl.multiple_of((rs - kstart) * GRID_W, GRID_W), r - rs))
        scores = []
        for qoff, koff, didx in rows:
            for h in range(NA_HEADS):
                cs = slice(128 * (h // 2), 128 * (h // 2 + 1))
                q2 = q_ref[pl.ds(qoff, GRID_W), cs]
                qm = jnp.where(first_head if h % 2 == 0 else jnp.logical_not(first_head), q2, jnp.zeros_like(q2))
                s = lax.dot_general(qm, k_ref[pl.ds(koff, win), cs], (((1,), (1,)), ((), ())),
                                    preferred_element_type=F32)
                bias = [bias_ref[h, 2 * kk + (NA_WIN_R - 1) - didx] for kk in range(NA_WIN_R // 2)]
                scores.append(s + jnp.concatenate(bias, axis=1))
        probs = []
        for s in scores:
            m = jnp.max(s, axis=-1, keepdims=True)
            e = jnp.exp(s - m)
            probs.append((e.astype(BF16), jnp.sum(e, axis=-1, keepdims=True)))
        for n, (qoff, koff, didx) in enumerate(rows):
            outs = []
            for h in range(NA_HEADS):
                e, l = probs[n * NA_HEADS + h]
                cs = slice(128 * (h // 2), 128 * (h // 2 + 1))
                outs.append(jnp.dot(e, v_ref[pl.ds(koff, win), cs], preferred_element_type=F32) / l)
            for p in range(NA_HEADS // 2):
                o_ref[pl.ds(qoff, GRID_W), 128 * p:128 * (p + 1)] = jnp.where(
                    first_head, outs[2 * p], outs[2 * p + 1]).astype(BF16)
        return carry

    lax.fori_loop(0, NA_Q_ROWS // NA_ROWS_PER_STEP, body, 0)


def _na(q, k, v, bias, l):
    B, S, _ = q.shape
    rows = S // GRID_W
    assert rows >= NA_KV_ROWS and rows % NA_Q_ROWS == 0
    qtok = NA_Q_ROWS * GRID_W
    kvtok = NA_KV_ROWS * GRID_W

    def kv_map(b, rb):
        return (b, jnp.clip(rb * NA_Q_ROWS - NA_Q_ROWS, 0, rows - NA_KV_ROWS) * GRID_W, 0)

    kv_spec = pl.BlockSpec((None, pl.Element(kvtok), pl.Element(NA_WIDTH)), kv_map)
    return pl.pallas_call(
        functools.partial(_na_kernel, rows_total=rows),
        grid=(B, rows // NA_Q_ROWS),
        in_specs=[
            pl.BlockSpec((None, qtok, NA_WIDTH), lambda b, rb: (b, rb, 0)),
            kv_spec,
            kv_spec,
            _layer_spec(bias, l, 2),
        ],
        out_specs=pl.BlockSpec((None, qtok, NA_WIDTH), lambda b, rb: (b, rb, 0)),
        out_shape=jax.ShapeDtypeStruct((B, S, NA_WIDTH), BF16),
        compiler_params=_cparams(("parallel", "parallel")),
        name="na",
    )(q, k, v, bias)


def _s5_tables(lam_re, lam_im, log_dt, b_re, b_im, c_re, c_im):
    L = S5_CHUNK
    lr = jnp.minimum(lam_re.astype(F32), -1e-4)
    li = lam_im.astype(F32)
    dt = jnp.exp(log_dt.astype(F32))[:, :, None]
    mag = jnp.exp(lr * dt)
    ar = mag * jnp.cos(li * dt)
    ai = mag * jnp.sin(li * dt)
    den = lr * lr + li * li
    fr = ((ar - 1.0) * lr + ai * li) / den
    fi = (ai * lr - (ar - 1.0) * li) / den
    br = b_re.astype(F32)
    bi = b_im.astype(F32)
    bbr = fr[..., None] * br - fi[..., None] * bi
    bbi = fr[..., None] * bi + fi[..., None] * br
    cr = c_re.astype(F32)
    ci = c_im.astype(F32)
    j = jnp.arange(L + 1, dtype=F32)[:, None, None, None]
    pm = jnp.exp(j * (lr * dt)[None])
    pr = pm * jnp.cos(j * (li * dt)[None])
    pi_ = pm * jnp.sin(j * (li * dt)[None])
    abr = pr[..., None] * bbr[None] - pi_[..., None] * bbi[None]
    abi = pr[..., None] * bbi[None] + pi_[..., None] * bbr[None]
    kern = jnp.einsum('dgcp,jdgpe->jdgce', cr, abr) - jnp.einsum('dgcp,jdgpe->jdgce', ci, abi)
    s_idx = np.arange(L)[:, None]
    t_idx = np.arange(L)[None, :]
    lags = np.arange(L + 1)[:, None, None]
    sel = np.stack([(t_idx - s_idx == lags), (s_idx - t_idx == lags)], axis=1).astype(np.float32)
    tk = jnp.einsum('jdst,jdgce->gsetc', sel, kern, precision=lax.Precision.HIGHEST)
    tmat = tk.reshape(S5_GROUPS, L * S5_GROUP_CH, L * S5_GROUP_CH)
    ef_r = abr[:L, 0][::-1]
    ef_i = abi[:L, 0][::-1]
    eb_r = abr[:L, 1]
    eb_i = abi[:L, 1]

    def to_rows(a):
        return jnp.transpose(a, (1, 0, 3, 2)).reshape(S5_GROUPS, L * S5_GROUP_CH, S5_STATE)

    we = jnp.concatenate([to_rows(ef_r), to_rows(eb_r), to_rows(ef_i), to_rows(eb_i)], axis=-1)
    def readout(d, descending):
        p_r = pr[1:L + 1, d]
        p_i = pi_[1:L + 1, d]
        if descending:
            p_r, p_i = p_r[::-1], p_i[::-1]
        m_re = cr[d][None] * p_r[:, :, None, :] - ci[d][None] * p_i[:, :, None, :]
        m_im = cr[d][None] * p_i[:, :, None, :] + ci[d][None] * p_r[:, :, None, :]
        to_cols = lambda a: jnp.transpose(a, (1, 3, 0, 2)).reshape(S5_GROUPS, S5_STATE, L * S5_GROUP_CH)
        return to_cols(m_re), to_cols(-m_im)

    yf_re, yf_im = readout(0, False)
    yb_re, yb_im = readout(1, True)
    wy = jnp.concatenate([yf_re, yb_re, yf_im, yb_im], axis=1)
    al = jnp.stack([jnp.concatenate([pr[L, 0], pr[L, 1]], axis=-1),
                    jnp.concatenate([pi_[L, 0], pi_[L, 1]], axis=-1)], axis=1)
    return tmat.astype(BF16), we.astype(BF16), wy.astype(BF16), al


def _s5_kernel(u_ref, t_ref, we_ref, wy_ref, al_ref, y_ref, e_ref, xp_ref, *, n_chunks, batch):
    lc = S5_CHUNK * S5_GROUP_CH
    half = 2 * S5_STATE
    slots = S5_BPAD // S5_GROUPS_PER_STEP
    if batch < slots:
        e_ref[...] = jnp.zeros(e_ref.shape, F32)
    for gg in range(S5_GROUPS_PER_STEP):
        u = u_ref[gg].reshape(batch * n_chunks, lc)
        y_ref[gg] = jnp.dot(u, t_ref[gg], preferred_element_type=F32).reshape(batch, n_chunks, lc)
        e = jnp.dot(u, we_ref[gg], preferred_element_type=F32)
        for b in range(batch):
            for hf in range(2):
                e_ref[hf, pl.ds(gg * slots + b, n_chunks, stride=S5_BPAD), :] = e[
                    b * n_chunks:(b + 1) * n_chunks, hf * half:(hf + 1) * half]
    lane = lax.broadcasted_iota(jnp.int32, (S5_BPAD, half), 1)
    sub = lax.broadcasted_iota(jnp.int32, (S5_BPAD, half), 0)
    is_fwd = lane < S5_STATE
    a_re = jnp.broadcast_to(al_ref[0, 0:1, :], (S5_BPAD, half))
    a_im = jnp.broadcast_to(al_ref[0, 1:2, :], (S5_BPAD, half))
    for gg in range(1, S5_GROUPS_PER_STEP):
        a_re = jnp.where(sub >= gg * slots, al_ref[gg, 0:1, :], a_re)
        a_im = jnp.where(sub >= gg * slots, al_ref[gg, 1:2, :], a_im)

    def body(j, carry):
        xr, xi = carry
        rf = pl.multiple_of(j * S5_BPAD, S5_BPAD)
        rb = pl.multiple_of((n_chunks - 1 - j) * S5_BPAD, S5_BPAD)
        xp_ref[pl.ds(rf, S5_BPAD), 0:S5_STATE] = xr[:, 0:S5_STATE]
        xp_ref[pl.ds(rb, S5_BPAD), S5_STATE:half] = xr[:, S5_STATE:half]
        xp_ref[pl.ds(rf, S5_BPAD), half:half + S5_STATE] = xi[:, 0:S5_STATE]
        xp_ref[pl.ds(rb, S5_BPAD), half + S5_STATE:2 * half] = xi[:, S5_STATE:half]
        er = jnp.where(is_fwd, e_ref[0, pl.ds(rf, S5_BPAD), :], e_ref[0, pl.ds(rb, S5_BPAD), :])
        ei = jnp.where(is_fwd, e_ref[1, pl.ds(rf, S5_BPAD), :], e_ref[1, pl.ds(rb, S5_BPAD), :])
        nxr = a_re * xr - a_im * xi + er
        nxi = a_re * xi + a_im * xr + ei
        return nxr, nxi

    zero = jnp.zeros((S5_BPAD, half), F32)
    lax.fori_loop(0, n_chunks, body, (zero, zero))
    xp = xp_ref[...].astype(BF16)
    for gg in range(S5_GROUPS_PER_STEP):
        yi = jnp.dot(xp, wy_ref[gg], preferred_element_type=F32)
        e_ref[0] = yi[:, :half]
        e_ref[1] = yi[:, half:]
        for b in range(batch):
            y_ref[gg, b] += jnp.concatenate(
                [e_ref[hf, pl.ds(gg * slots + b, n_chunks, stride=S5_BPAD), :] for hf in range(2)], axis=1)


def _s5(ug, tabs, l):
    tmat, we, wy, al = tabs
    G, B, n, lc = ug.shape
    gs = S5_GROUPS_PER_STEP
    assert B * gs <= S5_BPAD and lc == 4 * S5_STATE and G % gs == 0
    rows = n * S5_BPAD
    grp = lambda shape: pl.BlockSpec((gs,) + shape, lambda g: (g,) + (0,) * len(shape))
    tab = lambda shape: pl.BlockSpec((None, gs) + shape, lambda g: (l, g) + (0,) * len(shape))
    return pl.pallas_call(
        functools.partial(_s5_kernel, n_chunks=n, batch=B),
        grid=(G // gs,),
        in_specs=[grp((B, n, lc)), tab((lc, lc)), tab((lc, 4 * S5_STATE)), tab((4 * S5_STATE, lc)),
                  tab((2, 2 * S5_STATE))],
        out_specs=grp((B, n, lc)),
        out_shape=jax.ShapeDtypeStruct((G, B, n, lc), F32),
        scratch_shapes=[pltpu.VMEM((2, rows, 2 * S5_STATE), F32), pltpu.VMEM((rows, 4 * S5_STATE), F32)],
        compiler_params=_cparams(("parallel",)),
        name="s5",
    )(ug, tmat, we, wy, al)


def _memkv_kernel(m_ref, g_ref, w_ref, o_ref):
    x = m_ref[...]
    ms = jnp.mean(x * x, axis=-1, keepdims=True)
    h = (x * lax.rsqrt(ms + RMS_EPS) * g_ref[...]).astype(BF16)
    o_ref[...] = jnp.dot(h, w_ref[...], preferred_element_type=F32).astype(BF16)


def _memkv(mem, norm_mem, w_bf):
    B, M, D = mem.shape
    L = w_bf.shape[0]
    return pl.pallas_call(
        _memkv_kernel,
        grid=(L, B),
        in_specs=[pl.BlockSpec((None, M, D), lambda l, b: (b, 0, 0)),
                  pl.BlockSpec((None, 1, D), lambda l, b: (l, 0, 0)),
                  pl.BlockSpec((None, D, 2 * XA_WIDTH), lambda l, b: (l, 0, 0))],
        out_specs=pl.BlockSpec((None, None, M, 2 * XA_WIDTH), lambda l, b: (l, b, 0, 0)),
        out_shape=jax.ShapeDtypeStruct((L, B, M, 2 * XA_WIDTH), BF16),
        compiler_params=_cparams(("parallel", "parallel")),
        name="memkv",
    )(mem, norm_mem.reshape(L, 1, D), w_bf)


def _gelu_tanh(x):
    return 0.5 * x * (1.0 + jnp.tanh(math.sqrt(2.0 / math.pi) * (x + 0.044715 * (x * x * x))))


def _router(z):
    lane = lax.broadcasted_iota(jnp.int32, z.shape, 1).astype(F32)
    ninf = jnp.full_like(z, -jnp.inf)
    far = jnp.full_like(z, 1e9)
    cm = (lane >= MOE_EXPERTS) & (lane < MOE_EXPERTS + MOE_GROUPS)
    cmax = jnp.max(jnp.where(cm, z, ninf), axis=-1, keepdims=True)
    glane = jnp.min(jnp.where(cm & (z == cmax), lane, far), axis=-1, keepdims=True)
    psum = jnp.sum(jnp.where(cm, jnp.exp(jnp.where(cm, z, cmax) - cmax), 0.0), axis=-1, keepdims=True)
    p_grp = 1.0 / psum
    f0 = (glane - MOE_EXPERTS) * MOE_EPG
    fm = (lane >= f0) & (lane < f0 + MOE_EPG)
    v1 = jnp.max(jnp.where(fm, z, ninf), axis=-1, keepdims=True)
    i1 = jnp.min(jnp.where(fm & (z == v1), lane, far), axis=-1, keepdims=True)
    fm2 = fm & (lane != i1)
    v2 = jnp.max(jnp.where(fm2, z, ninf), axis=-1, keepdims=True)
    i2 = jnp.min(jnp.where(fm2 & (z == v2), lane, far), axis=-1, keepdims=True)
    t = jnp.exp(v2 - v1)
    w1 = p_grp / (1.0 + t)
    w2 = p_grp * t / (1.0 + t)
    comb = jnp.where(lane == i1, w1, 0.0) + jnp.where(lane == i2, w2, 0.0)
    a = jnp.minimum(i1, i2) - f0
    b = jnp.maximum(i1, i2) - f0
    pair = a * (7.0 - a) * 0.5 + b - a - 1.0
    return comb, (glane - MOE_EXPERTS) * MOE_PAIRS + pair


def _merge_kernel(x_ref, ona_ref, ys5_ref, u_ref, qx_ref, gate_ref, mkv_ref,
                  d_ref, wglu_ref, wna_ref, ws5_ref, wxa_ref, wout_ref, nffn_ref, wr2_ref, br_ref,
                  xo_ref, hp_ref, meta_ref, cnt_ref, mrg_ref, ys_ref):
    tm = x_ref.shape[0]
    cw = 256
    n_cw = D_MODEL // cw
    lane = lax.broadcasted_iota(jnp.int32, (tm, 128), 1)
    first_head = lane < XA_HEAD_DIM

    def gate(branch, c):
        return gate_ref[:, branch * D_MODEL + cw * c:branch * D_MODEL + cw * (c + 1)].astype(F32)

    scores = []
    for h in range(XA_HEADS):
        cs = slice(128 * (h // 2), 128 * (h // 2 + 1))
        q2 = qx_ref[:, cs]
        qm = jnp.where(first_head if h % 2 == 0 else jnp.logical_not(first_head), q2, jnp.zeros_like(q2))
        scores.append(lax.dot_general(qm, mkv_ref[:, cs], (((1,), (1,)), ((), ())), preferred_element_type=F32))
    ona = ona_ref[...]
    merged = [gate(0, c) * jnp.dot(ona, wna_ref[:, cw * c:cw * (c + 1)], preferred_element_type=F32)
              for c in range(n_cw)]
    _group_to_time_major(ys5_ref, ys_ref, tm // S5_CHUNK)
    ys = jnp.concatenate([ys_ref[0], ys_ref[1]], axis=1)
    y = _gelu_tanh(ys + d_ref[...] * u_ref[...])
    g = jnp.dot(y.astype(BF16), wglu_ref[...], preferred_element_type=F32)
    os5 = (g[:, :S5_WIDTH] * jax.nn.sigmoid(g[:, S5_WIDTH:])).astype(BF16)
    for c in range(n_cw):
        merged[c] += gate(1, c) * jnp.dot(os5, ws5_ref[:, cw * c:cw * (c + 1)], preferred_element_type=F32)
    outs = []
    for h, s in enumerate(scores):
        m = jnp.max(s, axis=-1, keepdims=True)
        e = jnp.exp(s - m)
        l = jnp.sum(e, axis=-1, keepdims=True)
        v2 = mkv_ref[:, XA_WIDTH + 128 * (h // 2):XA_WIDTH + 128 * (h // 2 + 1)]
        outs.append(jnp.dot(e.astype(BF16), v2, preferred_element_type=F32) / l)
    oxa = jnp.concatenate([jnp.where(first_head, outs[2 * p], outs[2 * p + 1]).astype(BF16)
                           for p in range(XA_HEADS // 2)], axis=-1)
    for c in range(n_cw):
        m = merged[c] + gate(2, c) * jnp.dot(oxa, wxa_ref[:, cw * c:cw * (c + 1)], preferred_element_type=F32)
        mrg_ref[:, cw * c:cw * (c + 1)] = m.astype(BF16)
    xn = x_ref[...] + jnp.dot(mrg_ref[...], wout_ref[...], preferred_element_type=F32)
    xo_ref[...] = xn
    ms = jnp.mean(xn * xn, axis=-1, keepdims=True)
    h2 = xn * lax.rsqrt(ms + RMS_EPS) * nffn_ref[...]
    hi = h2.astype(BF16)
    hi32 = hi.astype(F32)
    lo = (h2 - hi32).astype(BF16)
    z2 = jnp.dot(hi, wr2_ref[...], preferred_element_type=F32)
    z = (z2[:, :ROUTER_LANES] + z2[:, ROUTER_LANES:]
         + jnp.dot(lo, wr2_ref[:, :ROUTER_LANES], preferred_element_type=F32)) + br_ref[...]
    hw = D_MODEL // 2
    packed = pltpu.bitcast(hi32[:, :hw], jnp.uint32) | (pltpu.bitcast(hi32[:, hw:], jnp.uint32) >> 16)
    for j in range(hw // 128):
        hp_ref[pl.ds(j, tm, stride=TOKEN_REC_ROWS), :] = packed[:, 128 * j:128 * (j + 1)]
    comb, gid = _router(z)
    hp_ref[pl.ds(hw // 128, tm, stride=TOKEN_REC_ROWS), :] = pltpu.bitcast(comb, jnp.uint32)
    for j in range(hw // 128 + 1, TOKEN_REC_ROWS):
        hp_ref[pl.ds(j, tm, stride=TOKEN_REC_ROWS), :] = jnp.zeros((tm, 128), jnp.uint32)
    lanef = lane.astype(F32)
    onehot = lanef == gid
    row = lax.broadcasted_iota(jnp.int32, (tm, tm), 0)
    col = lax.broadcasted_iota(jnp.int32, (tm, tm), 1)
    tri = jnp.where(col <= row, 1.0, 0.0).astype(BF16)
    csum = jnp.dot(tri, jnp.where(onehot, 1.0, 0.0).astype(BF16), preferred_element_type=F32)
    rank = jnp.sum(jnp.where(onehot, csum, 0.0), axis=-1, keepdims=True) - 1.0
    meta = jnp.where(lane == 0, gid, jnp.where(lane == 1, rank, 0.0))
    meta_ref[...] = jnp.transpose(meta)[:8, :]
    cnt_ref[...] = jnp.broadcast_to(csum[tm - 1:tm, :], (8, 128))


def _merge(x, ona, ys5, u, qx, gates, mkv, params, l):
    B, S, D = x.shape
    tm = min(TOKEN_TILE, S)
    M = mkv.shape[2]
    n_i = S // tm

    def tok(width):
        return pl.BlockSpec((None, tm, width), lambda b, i: (b, i, 0))

    return pl.pallas_call(
        _merge_kernel,
        grid=(B, n_i),
        in_specs=[tok(D), tok(NA_WIDTH),
                  pl.BlockSpec((S5_GROUPS, None, tm // S5_CHUNK, S5_CHUNK * S5_GROUP_CH), lambda b, i: (0, b, i, 0)),
                  tok(S5_WIDTH), tok(XA_WIDTH), tok(N_BRANCHES * D),
                  pl.BlockSpec((None, None, M, 2 * XA_WIDTH), lambda b, i: (l, b, 0, 0))]
                 + [_layer_spec(p, l, 2) for p in params],
        out_specs=[tok(D), pl.BlockSpec((None, tm * TOKEN_REC_ROWS, 128), lambda b, i: (b, i, 0)),
                   pl.BlockSpec((None, None, 8, tm), lambda b, i: (b, i, 0, 0)),
                   pl.BlockSpec((None, None, 8, 128), lambda b, i: (b, i, 0, 0))],
        out_shape=[jax.ShapeDtypeStruct((B, S, D), F32),
                   jax.ShapeDtypeStruct((B, S * TOKEN_REC_ROWS, 128), jnp.uint32),
                   jax.ShapeDtypeStruct((B, n_i, 8, tm), F32),
                   jax.ShapeDtypeStruct((B, n_i, 8, 128), F32)],
        scratch_shapes=[pltpu.VMEM((tm, D), BF16), pltpu.VMEM((2, tm, 128), F32)],
        compiler_params=_cparams(("parallel", "parallel")),
        name="merge",
    )(x, ona, ys5, u, qx, gates, mkv, *params)


def _moe_plan(meta, cnt):
    B, n_i, _, tm = meta.shape
    n_tokens = B * n_i * tm
    gid = meta[:, :, 0, :].reshape(B * n_i, tm).astype(jnp.int32)
    rank = meta[:, :, 1, :].reshape(B * n_i, tm).astype(jnp.int32)
    tile_counts = cnt[:, :, 0, :MOE_BINS].reshape(B * n_i, MOE_BINS).astype(jnp.int32)
    counts = jnp.sum(tile_counts, axis=0)
    padded = ((counts + MOE_TILE - 1) // MOE_TILE) * MOE_TILE
    ends = jnp.cumsum(padded)
    tile_base = (ends - padded)[None, :] + jnp.cumsum(tile_counts, axis=0) - tile_counts
    pos = rank
    for g in range(MOE_BINS):
        pos = pos + jnp.where(gid == g, tile_base[:, g:g + 1], 0)
    pos = pos.reshape(n_tokens)
    n_tiles = n_tokens // MOE_TILE + MOE_BINS
    starts = jnp.arange(n_tiles, dtype=jnp.int32) * MOE_TILE
    tile_bin = jnp.minimum(jnp.sum(starts[:, None] >= ends[None, :], axis=1), MOE_BINS - 1).astype(jnp.int32)
    first = (tile_bin // MOE_PAIRS) * MOE_EPG
    pair = tile_bin % MOE_PAIRS
    tile_e1 = first + jnp.asarray(MOE_PAIR_A, jnp.int32)[pair]
    tile_e2 = first + jnp.asarray(MOE_PAIR_B, jnp.int32)[pair]
    return pos.astype(jnp.int32), ends.astype(jnp.int32), padded.astype(jnp.int32), tile_e1, tile_e2


def _rec(ref, token, count=1):
    return ref.at[pl.ds(pl.multiple_of(token * TOKEN_REC_ROWS, TOKEN_REC_ROWS), count * TOKEN_REC_ROWS), :]


DMA_UNROLL = 8


def _issue_record_dmas(count, copy_of):
    def body(blk, c):
        for k in range(DMA_UNROLL):
            copy_of(blk * DMA_UNROLL + k).start(priority=k % 2)
        return c
    lax.fori_loop(0, count // DMA_UNROLL, body, 0)


def _permute_kernel(ends_ref, padded_ref, pos_ref, h_ref, hs_ref, zero_ref, sem):
    tm = h_ref.shape[0] // TOKEN_REC_ROWS

    @pl.when(pl.program_id(0) == 0)
    def _():
        zero_ref[...] = jnp.zeros(zero_ref.shape, zero_ref.dtype)
        n_rows = hs_ref.shape[0] // TOKEN_REC_ROWS
        for g in range(MOE_BINS):
            tail = ends_ref[MOE_BINS - 1] + g * MOE_TILE
            for cond, start in ((padded_ref[g] > 0, ends_ref[g] - MOE_TILE), (tail < n_rows, tail)):
                @pl.when(cond)
                def _():
                    cp = pltpu.make_async_copy(zero_ref, _rec(hs_ref, start, MOE_TILE), sem)
                    cp.start()
                    cp.wait()

    _issue_record_dmas(tm, lambda r: pltpu.make_async_copy(_rec(h_ref, r), _rec(hs_ref, pos_ref[0, 0, r]), sem))
    pltpu.make_async_copy(h_ref, _rec(hs_ref, 0, tm), sem).wait()


def _moe_permute(hp2, pos3, ends, padded, n_rows):
    T = hp2.shape[0] // TOKEN_REC_ROWS
    tm = min(TOKEN_TILE, T)
    return pl.pallas_call(
        _permute_kernel,
        grid_spec=pltpu.PrefetchScalarGridSpec(
            num_scalar_prefetch=2,
            grid=(T // tm,),
            in_specs=[pl.BlockSpec((1, 1, tm), lambda i, e, p: (i, 0, 0), memory_space=pltpu.SMEM),
                      pl.BlockSpec((tm * TOKEN_REC_ROWS, 128), lambda i, e, p: (i, 0))],
            out_specs=pl.BlockSpec(memory_space=pl.ANY),
            scratch_shapes=[pltpu.VMEM((MOE_TILE * TOKEN_REC_ROWS, 128), jnp.uint32), pltpu.SemaphoreType.DMA(())],
        ),
        out_shape=jax.ShapeDtypeStruct((n_rows * TOKEN_REC_ROWS, 128), jnp.uint32),
        compiler_params=_cparams(("arbitrary",)),
        name="moe_permute",
    )(ends, padded, pos3, hp2)


def _moe_kernel(e1_ref, e2_ref, h_ref, wg1_ref, wu1_ref, wd1_ref, wg2_ref, wu2_ref, wd2_ref, o_ref):
    hw = D_MODEL // 2

    def field(j):
        return h_ref[pl.ds(j, MOE_TILE, stride=TOKEN_REC_ROWS), :]

    w = jnp.concatenate([field(j) for j in range(hw // 128)], axis=1)
    h = jnp.concatenate([pltpu.bitcast(w & jnp.uint32(0xFFFF0000), F32).astype(BF16),
                         pltpu.bitcast(w << 16, F32).astype(BF16)], axis=1)
    comb = pltpu.bitcast(field(hw // 128), F32)
    lane = lax.broadcasted_iota(jnp.int32, comb.shape, 1)
    i = pl.program_id(0)
    out = None
    for e_ref, wg_ref, wu_ref, wd_ref in ((e1_ref, wg1_ref, wu1_ref, wd1_ref), (e2_ref, wg2_ref, wu2_ref, wd2_ref)):
        c = jnp.sum(jnp.where(lane == e_ref[i], comb, 0.0), axis=-1, keepdims=True)
        g = jnp.dot(h, wg_ref[...], preferred_element_type=F32)
        u = jnp.dot(h, wu_ref[...], preferred_element_type=F32)
        a = (g * jax.nn.sigmoid(g) * u * c).astype(BF16)
        part = jnp.dot(a, wd_ref[...], preferred_element_type=F32)
        out = part if out is None else out + part
    for j in range(TOKEN_REC_ROWS):
        o_ref[pl.ds(j, MOE_TILE, stride=TOKEN_REC_ROWS), :] = out[:, 128 * j:128 * (j + 1)]


def _moe_experts(hs, tile_e1, tile_e2, wg, wu, wd, l):
    n_rows = hs.shape[0] // TOKEN_REC_ROWS
    n_tiles = n_rows // MOE_TILE
    rec_tile = pl.BlockSpec((MOE_TILE * TOKEN_REC_ROWS, 128), lambda i, e1, e2: (i, 0))

    def expert(shape, which):
        if which == 0:
            return pl.BlockSpec((None, None) + shape, lambda i, e1, e2: (l, e1[i], 0, 0))
        return pl.BlockSpec((None, None) + shape, lambda i, e1, e2: (l, e2[i], 0, 0))

    up, down = (D_MODEL, MOE_FF), (MOE_FF, D_MODEL)
    return pl.pallas_call(
        _moe_kernel,
        grid_spec=pltpu.PrefetchScalarGridSpec(
            num_scalar_prefetch=2,
            grid=(n_tiles,),
            in_specs=[rec_tile, expert(up, 0), expert(up, 0), expert(down, 0),
                      expert(up, 1), expert(up, 1), expert(down, 1)],
            out_specs=rec_tile,
        ),
        out_shape=jax.ShapeDtypeStruct((n_rows * TOKEN_REC_ROWS, 128), F32),
        compiler_params=_cparams(("arbitrary",)),
        name="moe_experts",
    )(tile_e1, tile_e2, hs, wg, wu, wd, wg, wu, wd)


def _final_kernel(pos_ref, x_ref, ys_ref, fn_ref, o_ref, buf_ref, sem):
    tm = x_ref.shape[0]
    _issue_record_dmas(tm, lambda r: pltpu.make_async_copy(_rec(ys_ref, pos_ref[0, 0, r]), _rec(buf_ref, r), sem))
    pltpu.make_async_copy(_rec(ys_ref, 0, tm), buf_ref, sem).wait()
    moe = jnp.concatenate([buf_ref[pl.ds(j, tm, stride=TOKEN_REC_ROWS), :] for j in range(TOKEN_REC_ROWS)], axis=1)
    xn = x_ref[...] + moe
    ms = jnp.mean(xn * xn, axis=-1, keepdims=True)
    o_ref[...] = xn * lax.rsqrt(ms + RMS_EPS) * fn_ref[...]


def _final(x2, ys, pos3, fnorm):
    T, D = x2.shape
    tm = min(TOKEN_TILE, T)
    return pl.pallas_call(
        _final_kernel,
        grid=(T // tm,),
        in_specs=[pl.BlockSpec((1, 1, tm), lambda i: (i, 0, 0), memory_space=pltpu.SMEM),
                  pl.BlockSpec((tm, D), lambda i: (i, 0)),
                  pl.BlockSpec(memory_space=pl.ANY),
                  pl.BlockSpec((1, D), lambda i: (0, 0))],
        out_specs=pl.BlockSpec((tm, D), lambda i: (i, 0)),
        out_shape=jax.ShapeDtypeStruct((T, D), F32),
        scratch_shapes=[pltpu.VMEM((tm * TOKEN_REC_ROWS, 128), F32), pltpu.SemaphoreType.DMA(())],
        compiler_params=_cparams(("arbitrary",)),
        name="final_unpermute_norm",
    )(pos3, x2, ys, fnorm.reshape(1, D))


def _moe(hp, meta, cnt, wg, wu, wd, l):
    T = hp.shape[0] * hp.shape[1] // TOKEN_REC_ROWS
    tm = min(TOKEN_TILE, T)
    pos, ends, padded, tile_e1, tile_e2 = _moe_plan(meta, cnt)
    pos3 = pos.reshape(T // tm, 1, tm)
    n_rows = T + MOE_BINS * MOE_TILE
    hs = _moe_permute(hp.reshape(T * TOKEN_REC_ROWS, 128), pos3, ends, padded, n_rows)
    return _moe_experts(hs, tile_e1, tile_e2, wg, wu, wd, l), pos3


def kernel(x, mem, norm_mix, norm_ffn, norm_mem, w_in, b_gate, na_rpb, s5_lam_re, s5_lam_im, s5_log_dt, s5_b_re, s5_b_im, s5_c_re, s5_c_im, s5_d, s5_w_glu, w_mem_kv, w_br_na, w_br_s5, w_br_xa, w_out, moe_w_coarse, moe_b_coarse, moe_w_fine, moe_b_fine, moe_w_gate, moe_w_up, moe_w_down, final_norm):
    depth, D = w_in.shape[0], w_in.shape[1]
    mkv_all = _memkv(mem, norm_mem, w_mem_kv.astype(BF16))
    w_in_bf = w_in.astype(BF16)
    norm_mix3 = norm_mix.reshape(depth, 1, D)
    b_gate3 = b_gate.reshape(depth, 1, N_BRANCHES * D)
    na_bias = jax.vmap(_na_bias_table)(na_rpb)
    s5_tabs = jax.vmap(_s5_tables)(s5_lam_re, s5_lam_im, s5_log_dt, s5_b_re, s5_b_im, s5_c_re, s5_c_im)
    wr = jnp.concatenate([moe_w_fine, moe_w_coarse], axis=2).astype(F32)
    wr = jnp.pad(wr, ((0, 0), (0, 0), (0, ROUTER_LANES - wr.shape[2])))
    wrh = wr.astype(BF16)
    wrl = (wr - wrh.astype(F32)).astype(BF16)
    br = jnp.pad(jnp.concatenate([moe_b_fine, moe_b_coarse], axis=1).astype(F32),
                 ((0, 0), (0, ROUTER_LANES - MOE_EXPERTS - MOE_GROUPS))).reshape(depth, 1, ROUTER_LANES)
    merge_params = (s5_d.reshape(depth, 1, S5_WIDTH), s5_w_glu.astype(BF16), w_br_na.astype(BF16),
                    w_br_s5.astype(BF16), w_br_xa.astype(BF16), w_out.astype(BF16),
                    norm_ffn.reshape(depth, 1, D), jnp.concatenate([wrh, wrl], axis=-1), br)
    wg, wu, wd = moe_w_gate.astype(BF16), moe_w_up.astype(BF16), moe_w_down.astype(BF16)
    moe = None
    for l in range(depth):
        if moe is None:
            q, k, v, u, ug, qx, gates = _inproj(x, norm_mix3, w_in_bf, b_gate3, l)
        else:
            q, k, v, u, ug, qx, gates, x = _inproj(x, norm_mix3, w_in_bf, b_gate3, l, moe)
        ona = _na(q, k, v, na_bias, l)
        ys5 = _s5(ug, s5_tabs, l)
        x, hp, meta, cnt = _merge(x, ona, ys5, u, qx, gates, mkv_all, merge_params, l)
        moe = _moe(hp, meta, cnt, wg, wu, wd, l)
    B, S, _ = x.shape
    ys, pos3 = moe
    return _final(x.reshape(B * S, D), ys, pos3, final_norm).reshape(B, S, D)
```

```python
import functools
import math

import numpy as np
import jax
import jax.numpy as jnp
from jax import lax
from jax.experimental import pallas as pl
from jax.experimental.pallas import tpu as pltpu

F32 = jnp.float32
BF16 = jnp.bfloat16

D_MODEL = 1024
GRID_W = 64
RMS_EPS = 1e-6
NA_HEADS = 8
NA_HEAD_DIM = 64
NA_WIDTH = NA_HEADS * NA_HEAD_DIM
NA_WIN_R = 8
NA_WIN_C = 16
NA_KV_ROWS = 24
NA_Q_ROWS = 8
NA_ROWS_PER_STEP = 8
S5_GROUPS = 16
S5_GROUP_CH = 16
S5_WIDTH = S5_GROUPS * S5_GROUP_CH
S5_STATE = 64
S5_CHUNK = 16
S5_STRIP_LANES = 512
S5_BPAD = 8
S5_GROUPS_PER_STEP = 2
XA_HEADS = 4
XA_HEAD_DIM = 64
XA_WIDTH = XA_HEADS * XA_HEAD_DIM
N_BRANCHES = 3
MOE_GROUPS = 4
MOE_EPG = 4
MOE_EXPERTS = MOE_GROUPS * MOE_EPG
MOE_FF = 256
MOE_ECHUNK = 4
ROUTER_LANES = 128
TOKEN_REC_ROWS = 8
MOE_PAIR_A = (0, 0, 0, 1, 1, 2)
MOE_PAIR_B = (1, 2, 3, 2, 3, 3)
MOE_PAIRS = len(MOE_PAIR_A)
MOE_BINS = MOE_GROUPS * MOE_PAIRS
MOE_TILE = 256
NEG_BIG = -1e30

VMEM_LIMIT = 52 * 1024 * 1024
TOKEN_TILE = 512


def _cparams(sem):
    return pltpu.CompilerParams(dimension_semantics=sem, vmem_limit_bytes=VMEM_LIMIT)


def _lane_block_mask(rows, j):
    lane = lax.broadcasted_iota(jnp.int32, (rows, 128), 1)
    lo = S5_GROUP_CH * (j % 8)
    return (lane >= lo) & (lane < lo + S5_GROUP_CH)


def _time_to_group_major(src_ref, rows):
    dest = [[None, None] for _ in range(S5_GROUPS)]
    for s in range(S5_CHUNK):
        halves = tuple(src_ref[hf, pl.ds(s, rows, stride=S5_CHUNK), :] for hf in range(2))
        m = _lane_block_mask(rows, s)
        for g in range(S5_GROUPS):
            shift = (S5_GROUP_CH * ((s % 8) - (g % 8))) % 128
            r = pltpu.roll(halves[g // 8], shift, axis=1) if shift else halves[g // 8]
            prev = dest[g][s // 8]
            dest[g][s // 8] = jnp.where(m, r, 0.0 if prev is None else prev)
    return [jnp.concatenate(d, axis=1) for d in dest]


def _group_to_time_major(src_ref, dst_ref, rows):
    for t in range(S5_CHUNK):
        out = [None, None]
        for g in range(S5_GROUPS):
            src = src_ref[g, :, 128 * (t // 8):128 * (t // 8 + 1)]
            shift = (S5_GROUP_CH * ((g % 8) - (t % 8))) % 128
            r = pltpu.roll(src, shift, axis=1) if shift else src
            prev = out[g // 8]
            out[g // 8] = jnp.where(_lane_block_mask(rows, g), r, 0.0 if prev is None else prev)
        for hf in range(2):
            dst_ref[hf, pl.ds(t, rows, stride=S5_CHUNK), :] = out[hf]


def _inproj_kernel(*refs, with_moe):
    if with_moe:
        (pos_ref, pos_next_ref, x_ref, ys_ref, g_ref, w_ref, bg_ref,
         q_ref, k_ref, v_ref, u_ref, ug_ref, qx_ref, gate_ref, xo_ref, uh_ref, buf_ref, sem) = refs
        tm = x_ref.shape[0]
        step = pl.program_id(0) * pl.num_programs(1) + pl.program_id(1)
        n_steps = pl.num_programs(0) * pl.num_programs(1)
        slot = lax.rem(step, 2)

        def fetch(p_ref, s):
            _issue_record_dmas(tm, lambda r: pltpu.make_async_copy(
                _rec(ys_ref, p_ref[0, 0, r]), _rec(buf_ref.at[s], r), sem.at[s]))

        @pl.when(step == 0)
        def _():
            fetch(pos_ref, 0)

        @pl.when(step + 1 < n_steps)
        def _():
            fetch(pos_next_ref, 1 - slot)

        pltpu.make_async_copy(_rec(ys_ref, 0, tm), buf_ref.at[slot], sem.at[slot]).wait()
        moe = jnp.concatenate([buf_ref[slot, pl.ds(j, tm, stride=TOKEN_REC_ROWS), :]
                               for j in range(TOKEN_REC_ROWS)], axis=1)
        x = x_ref[...] + moe
        xo_ref[...] = x
    else:
        x_ref, g_ref, w_ref, bg_ref, q_ref, k_ref, v_ref, u_ref, ug_ref, qx_ref, gate_ref, uh_ref = refs
        x = x_ref[...]
    ms = jnp.mean(x * x, axis=-1, keepdims=True)
    h = (x * lax.rsqrt(ms + RMS_EPS) * g_ref[...]).astype(BF16)

    def proj(a, b):
        return jnp.dot(h, w_ref[:, a:b], preferred_element_type=F32)

    c0 = NA_WIDTH
    q_ref[...] = (proj(0, c0) * (NA_HEAD_DIM ** -0.5)).astype(BF16)
    k_ref[...] = proj(c0, 2 * c0).astype(BF16)
    v_ref[...] = proj(2 * c0, 3 * c0).astype(BF16)
    c1 = 3 * c0
    u = proj(c1, c1 + S5_WIDTH)
    u_ref[...] = u
    uh_ref[0] = u[:, :128]
    uh_ref[1] = u[:, 128:]
    for g, ug in enumerate(_time_to_group_major(uh_ref, ug_ref.shape[1])):
        ug_ref[g] = ug.astype(BF16)
    c2 = c1 + S5_WIDTH
    qx_ref[...] = (proj(c2, c2 + XA_WIDTH) * (XA_HEAD_DIM ** -0.5)).astype(BF16)
    c3 = c2 + XA_WIDTH
    for j in range(N_BRANCHES):
        z = proj(c3 + D_MODEL * j, c3 + D_MODEL * (j + 1)) + bg_ref[:, D_MODEL * j:D_MODEL * (j + 1)]
        gate_ref[:, D_MODEL * j:D_MODEL * (j + 1)] = jax.nn.sigmoid(z).astype(BF16)


def _layer_spec(a, l, n_grid):
    zeros = (0,) * (a.ndim - 1)
    if n_grid == 1:
        index_map = lambda i: (l,) + zeros
    elif n_grid == 2:
        index_map = lambda b, i: (l,) + zeros
    else:
        raise ValueError(n_grid)
    return pl.BlockSpec((None,) + a.shape[1:], index_map, pipeline_mode=pl.Buffered(1))


def _inproj(x, g, w_bf, bg, l, moe=None):
    B, S, D = x.shape
    tm = min(TOKEN_TILE, S)
    n_i = S // tm

    def tok(width):
        return pl.BlockSpec((None, tm, width), lambda b, i: (b, i, 0))

    in_specs = [tok(D), _layer_spec(g, l, 2), _layer_spec(w_bf, l, 2), _layer_spec(bg, l, 2)]
    operands = [x, g, w_bf, bg]
    out_specs = [tok(NA_WIDTH), tok(NA_WIDTH), tok(NA_WIDTH), tok(S5_WIDTH),
                 pl.BlockSpec((S5_GROUPS, None, tm // S5_CHUNK, S5_CHUNK * S5_GROUP_CH), lambda b, i: (0, b, i, 0)),
                 tok(XA_WIDTH), tok(N_BRANCHES * D)]
    out_shape = [
        jax.ShapeDtypeStruct((B, S, NA_WIDTH), BF16),
        jax.ShapeDtypeStruct((B, S, NA_WIDTH), BF16),
        jax.ShapeDtypeStruct((B, S, NA_WIDTH), BF16),
        jax.ShapeDtypeStruct((B, S, S5_WIDTH), F32),
        jax.ShapeDtypeStruct((S5_GROUPS, B, S // S5_CHUNK, S5_CHUNK * S5_GROUP_CH), BF16),
        jax.ShapeDtypeStruct((B, S, XA_WIDTH), BF16),
        jax.ShapeDtypeStruct((B, S, N_BRANCHES * D), BF16),
    ]
    scratch = [pltpu.VMEM((2, tm, 128), F32)]
    if moe is not None:
        ys, pos3 = moe
        last = B * n_i - 1
        pos_spec = lambda nxt: pl.BlockSpec(
            (1, 1, tm), lambda b, i: (jnp.minimum(b * n_i + i + nxt, last), 0, 0), memory_space=pltpu.SMEM)
        in_specs = [pos_spec(0), pos_spec(1), in_specs[0], pl.BlockSpec(memory_space=pl.ANY)] + in_specs[1:]
        operands = [pos3, pos3, x, ys] + operands[1:]
        out_specs.append(tok(D))
        out_shape.append(jax.ShapeDtypeStruct((B, S, D), F32))
        scratch += [pltpu.VMEM((2, tm * TOKEN_REC_ROWS, 128), F32), pltpu.SemaphoreType.DMA((2,))]
    return pl.pallas_call(
        functools.partial(_inproj_kernel, with_moe=moe is not None),
        grid=(B, n_i),
        in_specs=in_specs,
        out_specs=out_specs,
        out_shape=out_shape,
        scratch_shapes=scratch,
        compiler_params=_cparams(("arbitrary", "arbitrary") if moe is not None else ("parallel", "parallel")),
        name="inproj",
    )(*operands)


def _na_bias_table(rpb):
    cols = np.arange(GRID_W)
    col_start = np.clip(cols - NA_WIN_C // 2, 0, GRID_W - NA_WIN_C)
    cj = np.arange(GRID_W)[None, :]
    valid = (cj >= col_start[:, None]) & (cj < col_start[:, None] + NA_WIN_C)
    col_idx = np.clip(cj - cols[:, None] + (NA_WIN_C - 1), 0, 2 * NA_WIN_C - 2)
    col_sel = ((col_idx[:, :, None] == np.arange(2 * NA_WIN_C - 1)) & valid[:, :, None]).astype(np.float32)
    t = jnp.einsum('hrk,cjk->hrcj', rpb.astype(F32), col_sel, precision=lax.Precision.HIGHEST)
    t = t + jnp.where(valid, 0.0, NEG_BIG)[None, None].astype(F32)
    return jnp.concatenate([t[:, :-1], t[:, 1:]], axis=-1)


def _na_kernel(q_ref, k_ref, v_ref, bias_ref, o_ref, *, rows_total):
    r0 = pl.program_id(1) * NA_Q_ROWS
    kstart = jnp.clip(r0 - NA_Q_ROWS, 0, rows_total - NA_KV_ROWS)
    lane = lax.broadcasted_iota(jnp.int32, (GRID_W, 128), 1)
    first_head = lane < NA_HEAD_DIM
    win = NA_WIN_R * GRID_W

    def body(it, carry):
        rows = []
        for sub in range(NA_ROWS_PER_STEP):
            i = it * NA_ROWS_PER_STEP + sub
            r = r0 + i
            rs = jnp.clip(r - NA_WIN_R // 2, 0, rows_total - NA_WIN_R)
            rows.append((pl.multiple_of(i * GRID_W, GRID_W), pl.multiple_of((rs - kstart) * GRID_W, GRID_W), r - rs))
        scores = []
        for qoff, koff, didx in rows:
            for h in range(NA_HEADS):
                cs = slice(128 * (h // 2), 128 * (h // 2 + 1))
                q2 = q_ref[pl.ds(qoff, GRID_W), cs]
                qm = jnp.where(first_head if h % 2 == 0 else jnp.logical_not(first_head), q2, jnp.zeros_like(q2))
                s = lax.dot_general(qm, k_ref[pl.ds(koff, win), cs], (((1,), (1,)), ((), ())),
                                    preferred_element_type=F32)
                bias = [bias_ref[h, 2 * kk + (NA_WIN_R - 1) - didx] for kk in range(NA_WIN_R // 2)]
                scores.append(s + jnp.concatenate(bias, axis=1))
        probs = []
        for s in scores:
            m = jnp.max(s, axis=-1, keepdims=True)
            e = jnp.exp(s - m)
            probs.append((e.astype(BF16), jnp.sum(e, axis=-1, keepdims=True)))
        for n, (qoff, koff, didx) in enumerate(rows):
            outs = []
            for h in range(NA_HEADS):
                e, l = probs[n * NA_HEADS + h]
                cs = slice(128 * (h // 2), 128 * (h // 2 + 1))
                outs.append(jnp.dot(e, v_ref[pl.ds(koff, win), cs], preferred_element_type=F32) / l)
            for p in range(NA_HEADS // 2):
                o_ref[pl.ds(qoff, GRID_W), 128 * p:128 * (p + 1)] = jnp.where(
                    first_head, outs[2 * p], outs[2 * p + 1]).astype(BF16)
        return carry

    lax.fori_loop(0, NA_Q_ROWS // NA_ROWS_PER_STEP, body, 0)


def _na(q, k, v, bias, l):
    B, S, _ = q.shape
    rows = S // GRID_W
    assert rows >= NA_KV_ROWS and rows % NA_Q_ROWS == 0
    qtok = NA_Q_ROWS * GRID_W
    kvtok = NA_KV_ROWS * GRID_W

    def kv_map(b, rb):
        return (b, jnp.clip(rb * NA_Q_ROWS - NA_Q_ROWS, 0, rows - NA_KV_ROWS) * GRID_W, 0)

    kv_spec = pl.BlockSpec((None, pl.Element(kvtok), pl.Element(NA_WIDTH)), kv_map)
    return pl.pallas_call(
        functools.partial(_na_kernel, rows_total=rows),
        grid=(B, rows // NA_Q_ROWS),
        in_specs=[
            pl.BlockSpec((None, qtok, NA_WIDTH), lambda b, rb: (b, rb, 0)),
            kv_spec,
            kv_spec,
            _layer_spec(bias, l, 2),
        ],
        out_specs=pl.BlockSpec((None, qtok, NA_WIDTH), lambda b, rb: (b, rb, 0)),
        out_shape=jax.ShapeDtypeStruct((B, S, NA_WIDTH), BF16),
        compiler_params=_cparams(("parallel", "parallel")),
        name="na",
    )(q, k, v, bias)


def _s5_tables(lam_re, lam_im, log_dt, b_re, b_im, c_re, c_im):
    L = S5_CHUNK
    lr = jnp.minimum(lam_re.astype(F32), -1e-4)
    li = lam_im.astype(F32)
    dt = jnp.exp(log_dt.astype(F32))[:, :, None]
    mag = jnp.exp(lr * dt)
    ar = mag * jnp.cos(li * dt)
    ai = mag * jnp.sin(li * dt)
    den = lr * lr + li * li
    fr = ((ar - 1.0) * lr + ai * li) / den
    fi = (ai * lr - (ar - 1.0) * li) / den
    br = b_re.astype(F32)
    bi = b_im.astype(F32)
    bbr = fr[..., None] * br - fi[..., None] * bi
    bbi = fr[..., None] * bi + fi[..., None] * br
    cr = c_re.astype(F32)
    ci = c_im.astype(F32)
    j = jnp.arange(L + 1, dtype=F32)[:, None, None, None]
    pm = jnp.exp(j * (lr * dt)[None])
    pr = pm * jnp.cos(j * (li * dt)[None])
    pi_ = pm * jnp.sin(j * (li * dt)[None])
    abr = pr[..., None] * bbr[None] - pi_[..., None] * bbi[None]
    abi = pr[..., None] * bbi[None] + pi_[..., None] * bbr[None]
    kern = jnp.einsum('dgcp,jdgpe->jdgce', cr, abr) - jnp.einsum('dgcp,jdgpe->jdgce', ci, abi)
    blocks = jnp.concatenate([kern[1:L, 1][::-1], (kern[0, 0] + kern[0, 1])[None], kern[1:L, 0]], axis=0)
    strip = jnp.transpose(blocks, (1, 3, 0, 2)).reshape(S5_GROUPS, S5_GROUP_CH, (2 * L - 1) * S5_GROUP_CH)
    tmat = jnp.pad(strip, ((0, 0), (0, 0), (0, S5_STRIP_LANES - strip.shape[-1])))
    ef_r = abr[:L, 0][::-1]
    ef_i = abi[:L, 0][::-1]
    eb_r = abr[:L, 1]
    eb_i = abi[:L, 1]

    def to_rows(a):
        return jnp.transpose(a, (1, 0, 3, 2)).reshape(S5_GROUPS, L * S5_GROUP_CH, S5_STATE)

    we = jnp.concatenate([to_rows(ef_r), to_rows(eb_r), to_rows(ef_i), to_rows(eb_i)], axis=-1)
    def readout(d, descending):
        p_r = pr[1:L + 1, d]
        p_i = pi_[1:L + 1, d]
        if descending:
            p_r, p_i = p_r[::-1], p_i[::-1]
        m_re = cr[d][None] * p_r[:, :, None, :] - ci[d][None] * p_i[:, :, None, :]
        m_im = cr[d][None] * p_i[:, :, None, :] + ci[d][None] * p_r[:, :, None, :]
        to_cols = lambda a: jnp.transpose(a, (1, 3, 0, 2)).reshape(S5_GROUPS, S5_STATE, L * S5_GROUP_CH)
        return to_cols(m_re), to_cols(-m_im)

    yf_re, yf_im = readout(0, False)
    yb_re, yb_im = readout(1, True)
    wy = jnp.concatenate([yf_re, yb_re, yf_im, yb_im], axis=1)
    al = jnp.stack([jnp.concatenate([pr[L, 0], pr[L, 1]], axis=-1),
                    jnp.concatenate([pi_[L, 0], pi_[L, 1]], axis=-1)], axis=1)
    return tmat, we.astype(BF16), wy.astype(BF16), al


def _s5_kernel(u_ref, t_ref, we_ref, wy_ref, al_ref, y_ref, e_ref, xp_ref, *, n_chunks, batch):
    lc = S5_CHUNK * S5_GROUP_CH
    half = 2 * S5_STATE
    slots = S5_BPAD // S5_GROUPS_PER_STEP
    if batch < slots:
        e_ref[...] = jnp.zeros(e_ref.shape, F32)
    for gg in range(S5_GROUPS_PER_STEP):
        u = u_ref[gg].reshape(batch * n_chunks, lc)
        strip = t_ref[gg]
        tmat = jnp.concatenate(
            [strip[:, S5_GROUP_CH * (S5_CHUNK - 1 - s):S5_GROUP_CH * (S5_CHUNK - 1 - s) + lc]
             for s in range(S5_CHUNK)], axis=0).astype(BF16)
        y_ref[gg] = jnp.dot(u, tmat, preferred_element_type=F32).reshape(batch, n_chunks, lc)
        e = jnp.dot(u, we_ref[gg], preferred_element_type=F32)
        for b in range(batch):
            for hf in range(2):
                e_ref[hf, pl.ds(gg * slots + b, n_chunks, stride=S5_BPAD), :] = e[
                    b * n_chunks:(b + 1) * n_chunks, hf * half:(hf + 1) * half]
    lane = lax.broadcasted_iota(jnp.int32, (S5_BPAD, half), 1)
    sub = lax.broadcasted_iota(jnp.int32, (S5_BPAD, half), 0)
    is_fwd = lane < S5_STATE
    a_re = jnp.broadcast_to(al_ref[0, 0:1, :], (S5_BPAD, half))
    a_im = jnp.broadcast_to(al_ref[0, 1:2, :], (S5_BPAD, half))
    for gg in range(1, S5_GROUPS_PER_STEP):
        a_re = jnp.where(sub >= gg * slots, al_ref[gg, 0:1, :], a_re)
        a_im = jnp.where(sub >= gg * slots, al_ref[gg, 1:2, :], a_im)

    def body(j, carry):
        xr, xi = carry
        rf = pl.multiple_of(j * S5_BPAD, S5_BPAD)
        rb = pl.multiple_of((n_chunks - 1 - j) * S5_BPAD, S5_BPAD)
        xp_ref[pl.ds(rf, S5_BPAD), 0:S5_STATE] = xr[:, 0:S5_STATE]
        xp_ref[pl.ds(rb, S5_BPAD), S5_STATE:half] = xr[:, S5_STATE:half]
        xp_ref[pl.ds(rf, S5_BPAD), half:half + S5_STATE] = xi[:, 0:S5_STATE]
        xp_ref[pl.ds(rb, S5_BPAD), half + S5_STATE:2 * half] = xi[:, S5_STATE:half]
        er = jnp.where(is_fwd, e_ref[0, pl.ds(rf, S5_BPAD), :], e_ref[0, pl.ds(rb, S5_BPAD), :])
        ei = jnp.where(is_fwd, e_ref[1, pl.ds(rf, S5_BPAD), :], e_ref[1, pl.ds(rb, S5_BPAD), :])
        nxr = a_re * xr - a_im * xi + er
        nxi = a_re * xi + a_im * xr + ei
        return nxr, nxi

    zero = jnp.zeros((S5_BPAD, half), F32)
    lax.fori_loop(0, n_chunks, body, (zero, zero))
    xp = xp_ref[...].astype(BF16)
    for gg in range(S5_GROUPS_PER_STEP):
        yi = jnp.dot(xp, wy_ref[gg], preferred_element_type=F32)
        e_ref[0] = yi[:, :half]
        e_ref[1] = yi[:, half:]
        for b in range(batch):
            y_ref[gg, b] += jnp.concatenate(
                [e_ref[hf, pl.ds(gg * slots + b, n_chunks, stride=S5_BPAD), :] for hf in range(2)], axis=1)


def _s5(ug, tabs, l):
    tmat, we, wy, al = tabs
    G, B, n, lc = ug.shape
    gs = S5_GROUPS_PER_STEP
    assert B * gs <= S5_BPAD and lc == 4 * S5_STATE and G % gs == 0
    rows = n * S5_BPAD
    grp = lambda shape: pl.BlockSpec((gs,) + shape, lambda g: (g,) + (0,) * len(shape))
    tab = lambda shape: pl.BlockSpec((None, gs) + shape, lambda g: (l, g) + (0,) * len(shape))
    return pl.pallas_call(
        functools.partial(_s5_kernel, n_chunks=n, batch=B),
        grid=(G // gs,),
        in_specs=[grp((B, n, lc)), tab((S5_GROUP_CH, S5_STRIP_LANES)), tab((lc, 4 * S5_STATE)),
                  tab((4 * S5_STATE, lc)),
                  tab((2, 2 * S5_STATE))],
        out_specs=grp((B, n, lc)),
        out_shape=jax.ShapeDtypeStruct((G, B, n, lc), F32),
        scratch_shapes=[pltpu.VMEM((2, rows, 2 * S5_STATE), F32), pltpu.VMEM((rows, 4 * S5_STATE), F32)],
        compiler_params=_cparams(("parallel",)),
        name="s5",
    )(ug, tmat, we, wy, al)


def _memkv_kernel(m_ref, g_ref, w_ref, o_ref):
    x = m_ref[...]
    ms = jnp.mean(x * x, axis=-1, keepdims=True)
    h = (x * lax.rsqrt(ms + RMS_EPS) * g_ref[...]).astype(BF16)
    o_ref[...] = jnp.dot(h, w_ref[...], preferred_element_type=F32).astype(BF16)


def _memkv(mem, norm_mem, w_bf):
    B, M, D = mem.shape
    L = w_bf.shape[0]
    return pl.pallas_call(
        _memkv_kernel,
        grid=(L, B),
        in_specs=[pl.BlockSpec((None, M, D), lambda l, b: (b, 0, 0)),
                  pl.BlockSpec((None, 1, D), lambda l, b: (l, 0, 0)),
                  pl.BlockSpec((None, D, 2 * XA_WIDTH), lambda l, b: (l, 0, 0))],
        out_specs=pl.BlockSpec((None, None, M, 2 * XA_WIDTH), lambda l, b: (l, b, 0, 0)),
        out_shape=jax.ShapeDtypeStruct((L, B, M, 2 * XA_WIDTH), BF16),
        compiler_params=_cparams(("parallel", "parallel")),
        name="memkv",
    )(mem, norm_mem.reshape(L, 1, D), w_bf)


def _gelu_tanh(x):
    return 0.5 * x * (1.0 + jnp.tanh(math.sqrt(2.0 / math.pi) * (x + 0.044715 * (x * x * x))))


def _router(z):
    lane = lax.broadcasted_iota(jnp.int32, z.shape, 1).astype(F32)
    ninf = jnp.full_like(z, -jnp.inf)
    far = jnp.full_like(z, 1e9)
    cm = (lane >= MOE_EXPERTS) & (lane < MOE_EXPERTS + MOE_GROUPS)
    cmax = jnp.max(jnp.where(cm, z, ninf), axis=-1, keepdims=True)
    glane = jnp.min(jnp.where(cm & (z == cmax), lane, far), axis=-1, keepdims=True)
    psum = jnp.sum(jnp.where(cm, jnp.exp(jnp.where(cm, z, cmax) - cmax), 0.0), axis=-1, keepdims=True)
    p_grp = 1.0 / psum
    f0 = (glane - MOE_EXPERTS) * MOE_EPG
    fm = (lane >= f0) & (lane < f0 + MOE_EPG)
    v1 = jnp.max(jnp.where(fm, z, ninf), axis=-1, keepdims=True)
    i1 = jnp.min(jnp.where(fm & (z == v1), lane, far), axis=-1, keepdims=True)
    fm2 = fm & (lane != i1)
    v2 = jnp.max(jnp.where(fm2, z, ninf), axis=-1, keepdims=True)
    i2 = jnp.min(jnp.where(fm2 & (z == v2), lane, far), axis=-1, keepdims=True)
    t = jnp.exp(v2 - v1)
    w1 = p_grp / (1.0 + t)
    w2 = p_grp * t / (1.0 + t)
    comb = jnp.where(lane == i1, w1, 0.0) + jnp.where(lane == i2, w2, 0.0)
    a = jnp.minimum(i1, i2) - f0
    b = jnp.maximum(i1, i2) - f0
    pair = a * (7.0 - a) * 0.5 + b - a - 1.0
    return comb, (glane - MOE_EXPERTS) * MOE_PAIRS + pair


def _merge_kernel(x_ref, ona_ref, ys5_ref, u_ref, qx_ref, gate_ref, mkv_ref,
                  d_ref, wglu_ref, wna_ref, ws5_ref, wxa_ref, wout_ref, nffn_ref, wr2_ref, br_ref,
                  xo_ref, hp_ref, meta_ref, cnt_ref, mrg_ref, ys_ref):
    tm = x_ref.shape[0]
    cw = 256
    n_cw = D_MODEL // cw
    lane = lax.broadcasted_iota(jnp.int32, (tm, 128), 1)
    first_head = lane < XA_HEAD_DIM

    def gate(branch, c):
        return gate_ref[:, branch * D_MODEL + cw * c:branch * D_MODEL + cw * (c + 1)].astype(F32)

    scores = []
    for h in range(XA_HEADS):
        cs = slice(128 * (h // 2), 128 * (h // 2 + 1))
        q2 = qx_ref[:, cs]
        qm = jnp.where(first_head if h % 2 == 0 else jnp.logical_not(first_head), q2, jnp.zeros_like(q2))
        scores.append(lax.dot_general(qm, mkv_ref[:, cs], (((1,), (1,)), ((), ())), preferred_element_type=F32))
    ona = ona_ref[...]
    merged = [gate(0, c) * jnp.dot(ona, wna_ref[:, cw * c:cw * (c + 1)], preferred_element_type=F32)
              for c in range(n_cw)]
    _group_to_time_major(ys5_ref, ys_ref, tm // S5_CHUNK)
    ys = jnp.concatenate([ys_ref[0], ys_ref[1]], axis=1)
    y = _gelu_tanh(ys + d_ref[...] * u_ref[...])
    g = jnp.dot(y.astype(BF16), wglu_ref[...], preferred_element_type=F32)
    os5 = (g[:, :S5_WIDTH] * jax.nn.sigmoid(g[:, S5_WIDTH:])).astype(BF16)
    for c in range(n_cw):
        merged[c] += gate(1, c) * jnp.dot(os5, ws5_ref[:, cw * c:cw * (c + 1)], preferred_element_type=F32)
    outs = []
    for h, s in enumerate(scores):
        m = jnp.max(s, axis=-1, keepdims=True)
        e = jnp.exp(s - m)
        l = jnp.sum(e, axis=-1, keepdims=True)
        v2 = mkv_ref[:, XA_WIDTH + 128 * (h // 2):XA_WIDTH + 128 * (h // 2 + 1)]
        outs.append(jnp.dot(e.astype(BF16), v2, preferred_element_type=F32) / l)
    oxa = jnp.concatenate([jnp.where(first_head, outs[2 * p], outs[2 * p + 1]).astype(BF16)
                           for p in range(XA_HEADS // 2)], axis=-1)
    for c in range(n_cw):
        m = merged[c] + gate(2, c) * jnp.dot(oxa, wxa_ref[:, cw * c:cw * (c + 1)], preferred_element_type=F32)
        mrg_ref[:, cw * c:cw * (c + 1)] = m.astype(BF16)
    xn = x_ref[...] + jnp.dot(mrg_ref[...], wout_ref[...], preferred_element_type=F32)
    xo_ref[...] = xn
    ms = jnp.mean(xn * xn, axis=-1, keepdims=True)
    h2 = xn * lax.rsqrt(ms + RMS_EPS) * nffn_ref[...]
    hi = h2.astype(BF16)
    hi32 = hi.astype(F32)
    lo = (h2 - hi32).astype(BF16)
    z2 = jnp.dot(hi, wr2_ref[...], preferred_element_type=F32)
    z = (z2[:, :ROUTER_LANES] + z2[:, ROUTER_LANES:]
         + jnp.dot(lo, wr2_ref[:, :ROUTER_LANES], preferred_element_type=F32)) + br_ref[...]
    hw = D_MODEL // 2
    packed = pltpu.bitcast(hi32[:, :hw], jnp.uint32) | (pltpu.bitcast(hi32[:, hw:], jnp.uint32) >> 16)
    for j in range(hw // 128):
        hp_ref[pl.ds(j, tm, stride=TOKEN_REC_ROWS), :] = packed[:, 128 * j:128 * (j + 1)]
    comb, gid = _router(z)
    hp_ref[pl.ds(hw // 128, tm, stride=TOKEN_REC_ROWS), :] = pltpu.bitcast(comb, jnp.uint32)
    for j in range(hw // 128 + 1, TOKEN_REC_ROWS):
        hp_ref[pl.ds(j, tm, stride=TOKEN_REC_ROWS), :] = jnp.zeros((tm, 128), jnp.uint32)
    lanef = lane.astype(F32)
    onehot = lanef == gid
    row = lax.broadcasted_iota(jnp.int32, (tm, tm), 0)
    col = lax.broadcasted_iota(jnp.int32, (tm, tm), 1)
    tri = jnp.where(col <= row, 1.0, 0.0).astype(BF16)
    csum = jnp.dot(tri, jnp.where(onehot, 1.0, 0.0).astype(BF16), preferred_element_type=F32)
    rank = jnp.sum(jnp.where(onehot, csum, 0.0), axis=-1, keepdims=True) - 1.0
    meta = jnp.where(lane == 0, gid, jnp.where(lane == 1, rank, 0.0))
    meta_ref[...] = jnp.transpose(meta)[:8, :]
    cnt_ref[...] = jnp.broadcast_to(csum[tm - 1:tm, :], (8, 128))


def _merge(x, ona, ys5, u, qx, gates, mkv, params, l):
    B, S, D = x.shape
    tm = min(TOKEN_TILE, S)
    M = mkv.shape[2]
    n_i = S // tm

    def tok(width):
        return pl.BlockSpec((None, tm, width), lambda b, i: (b, i, 0))

    return pl.pallas_call(
        _merge_kernel,
        grid=(B, n_i),
        in_specs=[tok(D), tok(NA_WIDTH),
                  pl.BlockSpec((S5_GROUPS, None, tm // S5_CHUNK, S5_CHUNK * S5_GROUP_CH), lambda b, i: (0, b, i, 0)),
                  tok(S5_WIDTH), tok(XA_WIDTH), tok(N_BRANCHES * D),
                  pl.BlockSpec((None, None, M, 2 * XA_WIDTH), lambda b, i: (l, b, 0, 0))]
                 + [_layer_spec(p, l, 2) for p in params],
        out_specs=[tok(D), pl.BlockSpec((None, tm * TOKEN_REC_ROWS, 128), lambda b, i: (b, i, 0)),
                   pl.BlockSpec((None, None, 8, tm), lambda b, i: (b, i, 0, 0)),
                   pl.BlockSpec((None, None, 8, 128), lambda b, i: (b, i, 0, 0))],
        out_shape=[jax.ShapeDtypeStruct((B, S, D), F32),
                   jax.ShapeDtypeStruct((B, S * TOKEN_REC_ROWS, 128), jnp.uint32),
                   jax.ShapeDtypeStruct((B, n_i, 8, tm), F32),
                   jax.ShapeDtypeStruct((B, n_i, 8, 128), F32)],
        scratch_shapes=[pltpu.VMEM((tm, D), BF16), pltpu.VMEM((2, tm, 128), F32)],
        compiler_params=_cparams(("parallel", "parallel")),
        name="merge",
    )(x, ona, ys5, u, qx, gates, mkv, *params)


def _moe_plan(meta, cnt):
    B, n_i, _, tm = meta.shape
    n_tokens = B * n_i * tm
    gid = meta[:, :, 0, :].reshape(B * n_i, tm).astype(jnp.int32)
    rank = meta[:, :, 1, :].reshape(B * n_i, tm).astype(jnp.int32)
    tile_counts = cnt[:, :, 0, :MOE_BINS].reshape(B * n_i, MOE_BINS).astype(jnp.int32)
    counts = jnp.sum(tile_counts, axis=0)
    padded = ((counts + MOE_TILE - 1) // MOE_TILE) * MOE_TILE
    ends = jnp.cumsum(padded)
    tile_base = (ends - padded)[None, :] + jnp.cumsum(tile_counts, axis=0) - tile_counts
    pos = rank
    for g in range(MOE_BINS):
        pos = pos + jnp.where(gid == g, tile_base[:, g:g + 1], 0)
    pos = pos.reshape(n_tokens)
    n_tiles = n_tokens // MOE_TILE + MOE_BINS
    starts = jnp.arange(n_tiles, dtype=jnp.int32) * MOE_TILE
    tile_bin = jnp.minimum(jnp.sum(starts[:, None] >= ends[None, :], axis=1), MOE_BINS - 1).astype(jnp.int32)
    first = (tile_bin // MOE_PAIRS) * MOE_EPG
    pair = tile_bin % MOE_PAIRS
    tile_e1 = first + jnp.asarray(MOE_PAIR_A, jnp.int32)[pair]
    tile_e2 = first + jnp.asarray(MOE_PAIR_B, jnp.int32)[pair]
    return pos.astype(jnp.int32), ends.astype(jnp.int32), padded.astype(jnp.int32), tile_e1, tile_e2


def _rec(ref, token, count=1):
    return ref.at[pl.ds(pl.multiple_of(token * TOKEN_REC_ROWS, TOKEN_REC_ROWS), count * TOKEN_REC_ROWS), :]


DMA_UNROLL = 8


def _issue_record_dmas(count, copy_of):
    def body(blk, c):
        for k in range(DMA_UNROLL):
            copy_of(blk * DMA_UNROLL + k).start(priority=k % 2)
        return c
    lax.fori_loop(0, count // DMA_UNROLL, body, 0)


def _permute_kernel(ends_ref, padded_ref, pos_ref, h_ref, hs_ref, zero_ref, sem):
    tm = h_ref.shape[0] // TOKEN_REC_ROWS

    @pl.when(pl.program_id(0) == 0)
    def _():
        zero_ref[...] = jnp.zeros(zero_ref.shape, zero_ref.dtype)
        n_rows = hs_ref.shape[0] // TOKEN_REC_ROWS
        for g in range(MOE_BINS):
            tail = ends_ref[MOE_BINS - 1] + g * MOE_TILE
            for cond, start in ((padded_ref[g] > 0, ends_ref[g] - MOE_TILE), (tail < n_rows, tail)):
                @pl.when(cond)
                def _():
                    cp = pltpu.make_async_copy(zero_ref, _rec(hs_ref, start, MOE_TILE), sem)
                    cp.start()
                    cp.wait()

    _issue_record_dmas(tm, lambda r: pltpu.make_async_copy(_rec(h_ref, r), _rec(hs_ref, pos_ref[0, 0, r]), sem))
    pltpu.make_async_copy(h_ref, _rec(hs_ref, 0, tm), sem).wait()


def _moe_permute(hp2, pos3, ends, padded, n_rows):
    T = hp2.shape[0] // TOKEN_REC_ROWS
    tm = min(TOKEN_TILE, T)
    return pl.pallas_call(
        _permute_kernel,
        grid_spec=pltpu.PrefetchScalarGridSpec(
            num_scalar_prefetch=2,
            grid=(T // tm,),
            in_specs=[pl.BlockSpec((1, 1, tm), lambda i, e, p: (i, 0, 0), memory_space=pltpu.SMEM),
                      pl.BlockSpec((tm * TOKEN_REC_ROWS, 128), lambda i, e, p: (i, 0))],
            out_specs=pl.BlockSpec(memory_space=pl.ANY),
            scratch_shapes=[pltpu.VMEM((MOE_TILE * TOKEN_REC_ROWS, 128), jnp.uint32), pltpu.SemaphoreType.DMA(())],
        ),
        out_shape=jax.ShapeDtypeStruct((n_rows * TOKEN_REC_ROWS, 128), jnp.uint32),
        compiler_params=_cparams(("arbitrary",)),
        name="moe_permute",
    )(ends, padded, pos3, hp2)


def _moe_kernel(e1_ref, e2_ref, h_ref, wg1_ref, wu1_ref, wd1_ref, wg2_ref, wu2_ref, wd2_ref, o_ref):
    hw = D_MODEL // 2

    def field(j):
        return h_ref[pl.ds(j, MOE_TILE, stride=TOKEN_REC_ROWS), :]

    w = jnp.concatenate([field(j) for j in range(hw // 128)], axis=1)
    h = jnp.concatenate([pltpu.bitcast(w & jnp.uint32(0xFFFF0000), F32).astype(BF16),
                         pltpu.bitcast(w << 16, F32).astype(BF16)], axis=1)
    comb = pltpu.bitcast(field(hw // 128), F32)
    lane = lax.broadcasted_iota(jnp.int32, comb.shape, 1)
    i = pl.program_id(0)
    out = None
    for e_ref, wg_ref, wu_ref, wd_ref in ((e1_ref, wg1_ref, wu1_ref, wd1_ref), (e2_ref, wg2_ref, wu2_ref, wd2_ref)):
        c = jnp.sum(jnp.where(lane == e_ref[i], comb, 0.0), axis=-1, keepdims=True)
        g = jnp.dot(h, wg_ref[...], preferred_element_type=F32)
        u = jnp.dot(h, wu_ref[...], preferred_element_type=F32)
        a = (g * jax.nn.sigmoid(g) * u * c).astype(BF16)
        part = jnp.dot(a, wd_ref[...], preferred_element_type=F32)
        out = part if out is None else out + part
    for j in range(TOKEN_REC_ROWS):
        o_ref[pl.ds(j, MOE_TILE, stride=TOKEN_REC_ROWS), :] = out[:, 128 * j:128 * (j + 1)]


def _moe_experts(hs, tile_e1, tile_e2, wg, wu, wd, l):
    n_rows = hs.shape[0] // TOKEN_REC_ROWS
    n_tiles = n_rows // MOE_TILE
    rec_tile = pl.BlockSpec((MOE_TILE * TOKEN_REC_ROWS, 128), lambda i, e1, e2: (i, 0))

    def expert(shape, which):
        if which == 0:
            return pl.BlockSpec((None, None) + shape, lambda i, e1, e2: (l, e1[i], 0, 0))
        return pl.BlockSpec((None, None) + shape, lambda i, e1, e2: (l, e2[i], 0, 0))

    up, down = (D_MODEL, MOE_FF), (MOE_FF, D_MODEL)
    return pl.pallas_call(
        _moe_kernel,
        grid_spec=pltpu.PrefetchScalarGridSpec(
            num_scalar_prefetch=2,
            grid=(n_tiles,),
            in_specs=[rec_tile, expert(up, 0), expert(up, 0), expert(down, 0),
                      expert(up, 1), expert(up, 1), expert(down, 1)],
            out_specs=rec_tile,
        ),
        out_shape=jax.ShapeDtypeStruct((n_rows * TOKEN_REC_ROWS, 128), F32),
        compiler_params=_cparams(("arbitrary",)),
        name="moe_experts",
    )(tile_e1, tile_e2, hs, wg, wu, wd, wg, wu, wd)


def _final_kernel(pos_ref, x_ref, ys_ref, fn_ref, o_ref, buf_ref, sem):
    tm = x_ref.shape[0]
    _issue_record_dmas(tm, lambda r: pltpu.make_async_copy(_rec(ys_ref, pos_ref[0, 0, r]), _rec(buf_ref, r), sem))
    pltpu.make_async_copy(_rec(ys_ref, 0, tm), buf_ref, sem).wait()
    moe = jnp.concatenate([buf_ref[pl.ds(j, tm, stride=TOKEN_REC_ROWS), :] for j in range(TOKEN_REC_ROWS)], axis=1)
    xn = x_ref[...] + moe
    ms = jnp.mean(xn * xn, axis=-1, keepdims=True)
    o_ref[...] = xn * lax.rsqrt(ms + RMS_EPS) * fn_ref[...]


def _final(x2, ys, pos3, fnorm):
    T, D = x2.shape
    tm = min(TOKEN_TILE, T)
    return pl.pallas_call(
        _final_kernel,
        grid=(T // tm,),
        in_specs=[pl.BlockSpec((1, 1, tm), lambda i: (i, 0, 0), memory_space=pltpu.SMEM),
                  pl.BlockSpec((tm, D), lambda i: (i, 0)),
                  pl.BlockSpec(memory_space=pl.ANY),
                  pl.BlockSpec((1, D), lambda i: (0, 0))],
        out_specs=pl.BlockSpec((tm, D), lambda i: (i, 0)),
        out_shape=jax.ShapeDtypeStruct((T, D), F32),
        scratch_shapes=[pltpu.VMEM((tm * TOKEN_REC_ROWS, 128), F32), pltpu.SemaphoreType.DMA(())],
        compiler_params=_cparams(("arbitrary",)),
        name="final_unpermute_norm",
    )(pos3, x2, ys, fnorm.reshape(1, D))


def _moe(hp, meta, cnt, wg, wu, wd, l):
    T = hp.shape[0] * hp.shape[1] // TOKEN_REC_ROWS
    tm = min(TOKEN_TILE, T)
    pos, ends, padded, tile_e1, tile_e2 = _moe_plan(meta, cnt)
    pos3 = pos.reshape(T // tm, 1, tm)
    n_rows = T + MOE_BINS * MOE_TILE
    hs = _moe_permute(hp.reshape(T * TOKEN_REC_ROWS, 128), pos3, ends, padded, n_rows)
    return _moe_experts(hs, tile_e1, tile_e2, wg, wu, wd, l), pos3


def kernel(x, mem, norm_mix, norm_ffn, norm_mem, w_in, b_gate, na_rpb, s5_lam_re, s5_lam_im, s5_log_dt, s5_b_re, s5_b_im, s5_c_re, s5_c_im, s5_d, s5_w_glu, w_mem_kv, w_br_na, w_br_s5, w_br_xa, w_out, moe_w_coarse, moe_b_coarse, moe_w_fine, moe_b_fine, moe_w_gate, moe_w_up, moe_w_down, final_norm):
    depth, D = w_in.shape[0], w_in.shape[1]
    mkv_all = _memkv(mem, norm_mem, w_mem_kv.astype(BF16))
    w_in_bf = w_in.astype(BF16)
    norm_mix3 = norm_mix.reshape(depth, 1, D)
    b_gate3 = b_gate.reshape(depth, 1, N_BRANCHES * D)
    na_bias = jax.vmap(_na_bias_table)(na_rpb)
    s5_tabs = jax.vmap(_s5_tables)(s5_lam_re, s5_lam_im, s5_log_dt, s5_b_re, s5_b_im, s5_c_re, s5_c_im)
    wr = jnp.concatenate([moe_w_fine, moe_w_coarse], axis=2).astype(F32)
    wr = jnp.pad(wr, ((0, 0), (0, 0), (0, ROUTER_LANES - wr.shape[2])))
    wrh = wr.astype(BF16)
    wrl = (wr - wrh.astype(F32)).astype(BF16)
    br = jnp.pad(jnp.concatenate([moe_b_fine, moe_b_coarse], axis=1).astype(F32),
                 ((0, 0), (0, ROUTER_LANES - MOE_EXPERTS - MOE_GROUPS))).reshape(depth, 1, ROUTER_LANES)
    merge_params = (s5_d.reshape(depth, 1, S5_WIDTH), s5_w_glu.astype(BF16), w_br_na.astype(BF16),
                    w_br_s5.astype(BF16), w_br_xa.astype(BF16), w_out.astype(BF16),
                    norm_ffn.reshape(depth, 1, D), jnp.concatenate([wrh, wrl], axis=-1), br)
    wg, wu, wd = moe_w_gate.astype(BF16), moe_w_up.astype(BF16), moe_w_down.astype(BF16)
    moe = None
    for l in range(depth):
        if moe is None:
            q, k, v, u, ug, qx, gates = _inproj(x, norm_mix3, w_in_bf, b_gate3, l)
        else:
            q, k, v, u, ug, qx, gates, x = _inproj(x, norm_mix3, w_in_bf, b_gate3, l, moe)
        ona = _na(q, k, v, na_bias, l)
        ys5 = _s5(ug, s5_tabs, l)
        x, hp, meta, cnt = _merge(x, ona, ys5, u, qx, gates, mkv_all, merge_params, l)
        moe = _moe(hp, meta, cnt, wg, wu, wd, l)
    B, S, _ = x.shape
    ys, pos3 = moe
    return _final(x.reshape(B * S, D), ys, pos3, final_norm).reshape(B, S, D)
```

```python
import functools
import math

import numpy as np
import jax
import jax.numpy as jnp
from jax import lax
from jax.experimental import pallas as pl
from jax.experimental.pallas import tpu as pltpu

F32 = jnp.float32
BF16 = jnp.bfloat16

D_MODEL = 1024
GRID_W = 64
RMS_EPS = 1e-6
NA_HEADS = 8
NA_HEAD_DIM = 64
NA_WIDTH = NA_HEADS * NA_HEAD_DIM
NA_WIN_R = 8
NA_WIN_C = 16
NA_KV_ROWS = 24
NA_Q_ROWS = 8
NA_ROWS_PER_STEP = 8
S5_GROUPS = 16
S5_GROUP_CH = 16
S5_WIDTH = S5_GROUPS * S5_GROUP_CH
S5_STATE = 64
S5_CHUNK = 16
S5_STRIP_LANES = 512
S5_BPAD = 8
S5_GROUPS_PER_STEP = 2
XA_HEADS = 4
XA_HEAD_DIM = 64
XA_WIDTH = XA_HEADS * XA_HEAD_DIM
N_BRANCHES = 3
MOE_GROUPS = 4
MOE_EPG = 4
MOE_EXPERTS = MOE_GROUPS * MOE_EPG
MOE_FF = 256
MOE_ECHUNK = 4
ROUTER_LANES = 128
TOKEN_REC_ROWS = 8
MOE_PAIR_A = (0, 0, 0, 1, 1, 2)
MOE_PAIR_B = (1, 2, 3, 2, 3, 3)
MOE_PAIRS = len(MOE_PAIR_A)
MOE_BINS = MOE_GROUPS * MOE_PAIRS
MOE_TILE = 512
NEG_BIG = -1e30

VMEM_LIMIT = 52 * 1024 * 1024
TOKEN_TILE = 512


def _cparams(sem):
    return pltpu.CompilerParams(dimension_semantics=sem, vmem_limit_bytes=VMEM_LIMIT)


def _lane_block_mask(rows, j):
    lane = lax.broadcasted_iota(jnp.int32, (rows, 128), 1)
    lo = S5_GROUP_CH * (j % 8)
    return (lane >= lo) & (lane < lo + S5_GROUP_CH)


def _time_to_group_major(src_ref, rows):
    dest = [[None, None] for _ in range(S5_GROUPS)]
    for s in range(S5_CHUNK):
        halves = tuple(src_ref[hf, pl.ds(s, rows, stride=S5_CHUNK), :] for hf in range(2))
        m = _lane_block_mask(rows, s)
        for g in range(S5_GROUPS):
            shift = (S5_GROUP_CH * ((s % 8) - (g % 8))) % 128
            r = pltpu.roll(halves[g // 8], shift, axis=1) if shift else halves[g // 8]
            prev = dest[g][s // 8]
            dest[g][s // 8] = jnp.where(m, r, 0.0 if prev is None else prev)
    return [jnp.concatenate(d, axis=1) for d in dest]


def _group_to_time_major(src_ref, dst_ref, rows):
    for t in range(S5_CHUNK):
        out = [None, None]
        for g in range(S5_GROUPS):
            src = src_ref[g, :, 128 * (t // 8):128 * (t // 8 + 1)]
            shift = (S5_GROUP_CH * ((g % 8) - (t % 8))) % 128
            r = pltpu.roll(src, shift, axis=1) if shift else src
            prev = out[g // 8]
            out[g // 8] = jnp.where(_lane_block_mask(rows, g), r, 0.0 if prev is None else prev)
        for hf in range(2):
            dst_ref[hf, pl.ds(t, rows, stride=S5_CHUNK), :] = out[hf]


def _inproj_kernel(*refs, with_moe):
    if with_moe:
        (pos_ref, pos_next_ref, x_ref, ys_ref, g_ref, w_ref, bg_ref,
         q_ref, k_ref, v_ref, u_ref, ug_ref, qx_ref, gate_ref, xo_ref, uh_ref, buf_ref, sem) = refs
        tm = x_ref.shape[0]
        step = pl.program_id(0) * pl.num_programs(1) + pl.program_id(1)
        n_steps = pl.num_programs(0) * pl.num_programs(1)
        slot = lax.rem(step, 2)

        def fetch(p_ref, s):
            _issue_record_dmas(tm, lambda r: pltpu.make_async_copy(
                _rec(ys_ref, p_ref[0, 0, r]), _rec(buf_ref.at[s], r), sem.at[s]))

        @pl.when(step == 0)
        def _():
            fetch(pos_ref, 0)

        @pl.when(step + 1 < n_steps)
        def _():
            fetch(pos_next_ref, 1 - slot)

        pltpu.make_async_copy(_rec(ys_ref, 0, tm), buf_ref.at[slot], sem.at[slot]).wait()
        moe = jnp.concatenate([buf_ref[slot, pl.ds(j, tm, stride=TOKEN_REC_ROWS), :]
                               for j in range(TOKEN_REC_ROWS)], axis=1)
        x = x_ref[...] + moe
        xo_ref[...] = x
    else:
        x_ref, g_ref, w_ref, bg_ref, q_ref, k_ref, v_ref, u_ref, ug_ref, qx_ref, gate_ref, uh_ref = refs
        x = x_ref[...]
    ms = jnp.mean(x * x, axis=-1, keepdims=True)
    h = (x * lax.rsqrt(ms + RMS_EPS) * g_ref[...]).astype(BF16)

    def proj(a, b):
        return jnp.dot(h, w_ref[:, a:b], preferred_element_type=F32)

    c0 = NA_WIDTH
    q_ref[...] = (proj(0, c0) * (NA_HEAD_DIM ** -0.5)).astype(BF16)
    k_ref[...] = proj(c0, 2 * c0).astype(BF16)
    v_ref[...] = proj(2 * c0, 3 * c0).astype(BF16)
    c1 = 3 * c0
    u = proj(c1, c1 + S5_WIDTH)
    u_ref[...] = u
    uh_ref[0] = u[:, :128]
    uh_ref[1] = u[:, 128:]
    for g, ug in enumerate(_time_to_group_major(uh_ref, ug_ref.shape[1])):
        ug_ref[g] = ug.astype(BF16)
    c2 = c1 + S5_WIDTH
    qx_ref[...] = (proj(c2, c2 + XA_WIDTH) * (XA_HEAD_DIM ** -0.5)).astype(BF16)
    c3 = c2 + XA_WIDTH
    for j in range(N_BRANCHES):
        z = proj(c3 + D_MODEL * j, c3 + D_MODEL * (j + 1)) + bg_ref[:, D_MODEL * j:D_MODEL * (j + 1)]
        gate_ref[:, D_MODEL * j:D_MODEL * (j + 1)] = jax.nn.sigmoid(z).astype(BF16)


def _layer_spec(a, l, n_grid):
    zeros = (0,) * (a.ndim - 1)
    if n_grid == 1:
        index_map = lambda i: (l,) + zeros
    elif n_grid == 2:
        index_map = lambda b, i: (l,) + zeros
    else:
        raise ValueError(n_grid)
    return pl.BlockSpec((None,) + a.shape[1:], index_map, pipeline_mode=pl.Buffered(1))


def _inproj(x, g, w_bf, bg, l, moe=None):
    B, S, D = x.shape
    tm = min(TOKEN_TILE, S)
    n_i = S // tm

    def tok(width):
        return pl.BlockSpec((None, tm, width), lambda b, i: (b, i, 0))

    in_specs = [tok(D), _layer_spec(g, l, 2), _layer_spec(w_bf, l, 2), _layer_spec(bg, l, 2)]
    operands = [x, g, w_bf, bg]
    out_specs = [tok(NA_WIDTH), tok(NA_WIDTH), tok(NA_WIDTH), tok(S5_WIDTH),
                 pl.BlockSpec((S5_GROUPS, None, tm // S5_CHUNK, S5_CHUNK * S5_GROUP_CH), lambda b, i: (0, b, i, 0)),
                 tok(XA_WIDTH), tok(N_BRANCHES * D)]
    out_shape = [
        jax.ShapeDtypeStruct((B, S, NA_WIDTH), BF16),
        jax.ShapeDtypeStruct((B, S, NA_WIDTH), BF16),
        jax.ShapeDtypeStruct((B, S, NA_WIDTH), BF16),
        jax.ShapeDtypeStruct((B, S, S5_WIDTH), F32),
        jax.ShapeDtypeStruct((S5_GROUPS, B, S // S5_CHUNK, S5_CHUNK * S5_GROUP_CH), BF16),
        jax.ShapeDtypeStruct((B, S, XA_WIDTH), BF16),
        jax.ShapeDtypeStruct((B, S, N_BRANCHES * D), BF16),
    ]
    scratch = [pltpu.VMEM((2, tm, 128), F32)]
    if moe is not None:
        ys, pos3 = moe
        last = B * n_i - 1
        pos_spec = lambda nxt: pl.BlockSpec(
            (1, 1, tm), lambda b, i: (jnp.minimum(b * n_i + i + nxt, last), 0, 0), memory_space=pltpu.SMEM)
        in_specs = [pos_spec(0), pos_spec(1), in_specs[0], pl.BlockSpec(memory_space=pl.ANY)] + in_specs[1:]
        operands = [pos3, pos3, x, ys] + operands[1:]
        out_specs.append(tok(D))
        out_shape.append(jax.ShapeDtypeStruct((B, S, D), F32))
        scratch += [pltpu.VMEM((2, tm * TOKEN_REC_ROWS, 128), F32), pltpu.SemaphoreType.DMA((2,))]
    return pl.pallas_call(
        functools.partial(_inproj_kernel, with_moe=moe is not None),
        grid=(B, n_i),
        in_specs=in_specs,
        out_specs=out_specs,
        out_shape=out_shape,
        scratch_shapes=scratch,
        compiler_params=_cparams(("arbitrary", "arbitrary") if moe is not None else ("parallel", "parallel")),
        name="inproj",
    )(*operands)


def _na_bias_table(rpb):
    cols = np.arange(GRID_W)
    col_start = np.clip(cols - NA_WIN_C // 2, 0, GRID_W - NA_WIN_C)
    cj = np.arange(GRID_W)[None, :]
    valid = (cj >= col_start[:, None]) & (cj < col_start[:, None] + NA_WIN_C)
    col_idx = np.clip(cj - cols[:, None] + (NA_WIN_C - 1), 0, 2 * NA_WIN_C - 2)
    col_sel = ((col_idx[:, :, None] == np.arange(2 * NA_WIN_C - 1)) & valid[:, :, None]).astype(np.float32)
    t = jnp.einsum('hrk,cjk->hrcj', rpb.astype(F32), col_sel, precision=lax.Precision.HIGHEST)
    t = t + jnp.where(valid, 0.0, NEG_BIG)[None, None].astype(F32)
    return jnp.concatenate([t[:, :-1], t[:, 1:]], axis=-1)


def _na_kernel(q_ref, k_ref, v_ref, bias_ref, o_ref, *, rows_total):
    r0 = pl.program_id(1) * NA_Q_ROWS
    kstart = jnp.clip(r0 - NA_Q_ROWS, 0, rows_total - NA_KV_ROWS)
    lane = lax.broadcasted_iota(jnp.int32, (GRID_W, 128), 1)
    first_head = lane < NA_HEAD_DIM
    win = NA_WIN_R * GRID_W

    def body(it, carry):
        rows = []
        for sub in range(NA_ROWS_PER_STEP):
            i = it * NA_ROWS_PER_STEP + sub
            r = r0 + i
            rs = jnp.clip(r - NA_WIN_R // 2, 0, rows_total - NA_WIN_R)
            rows.append((pl.multiple_of(i * GRID_W, GRID_W), pl.multiple_of((rs - kstart) * GRID_W, GRID_W), r - rs))
        scores = []
        for qoff, koff, didx in rows:
            for h in range(NA_HEADS):
                cs = slice(128 * (h // 2), 128 * (h // 2 + 1))
                q2 = q_ref[pl.ds(qoff, GRID_W), cs]
                qm = jnp.where(first_head if h % 2 == 0 else jnp.logical_not(first_head), q2, jnp.zeros_like(q2))
                s = lax.dot_general(qm, k_ref[pl.ds(koff, win), cs], (((1,), (1,)), ((), ())),
                                    preferred_element_type=F32)
                bias = [bias_ref[h, 2 * kk + (NA_WIN_R - 1) - didx] for kk in range(NA_WIN_R // 2)]
                scores.append(s + jnp.concatenate(bias, axis=1))
        probs = []
        for s in scores:
            m = jnp.max(s, axis=-1, keepdims=True)
            e = jnp.exp(s - m)
            probs.append((e.astype(BF16), jnp.sum(e, axis=-1, keepdims=True)))
        for n, (qoff, koff, didx) in enumerate(rows):
            outs = []
            for h in range(NA_HEADS):
                e, l = probs[n * NA_HEADS + h]
                cs = slice(128 * (h // 2), 128 * (h // 2 + 1))
                outs.append(jnp.dot(e, v_ref[pl.ds(koff, win), cs], preferred_element_type=F32) / l)
            for p in range(NA_HEADS // 2):
                o_ref[pl.ds(qoff, GRID_W), 128 * p:128 * (p + 1)] = jnp.where(
                    first_head, outs[2 * p], outs[2 * p + 1]).astype(BF16)
        return carry

    lax.fori_loop(0, NA_Q_ROWS // NA_ROWS_PER_STEP, body, 0)


def _na(q, k, v, bias, l):
    B, S, _ = q.shape
    rows = S // GRID_W
    assert rows >= NA_KV_ROWS and rows % NA_Q_ROWS == 0
    qtok = NA_Q_ROWS * GRID_W
    kvtok = NA_KV_ROWS * GRID_W

    def kv_map(b, rb):
        return (b, jnp.clip(rb * NA_Q_ROWS - NA_Q_ROWS, 0, rows - NA_KV_ROWS) * GRID_W, 0)

    kv_spec = pl.BlockSpec((None, pl.Element(kvtok), pl.Element(NA_WIDTH)), kv_map)
    return pl.pallas_call(
        functools.partial(_na_kernel, rows_total=rows),
        grid=(B, rows // NA_Q_ROWS),
        in_specs=[
            pl.BlockSpec((None, qtok, NA_WIDTH), lambda b, rb: (b, rb, 0)),
            kv_spec,
            kv_spec,
            _layer_spec(bias, l, 2),
        ],
        out_specs=pl.BlockSpec((None, qtok, NA_WIDTH), lambda b, rb: (b, rb, 0)),
        out_shape=jax.ShapeDtypeStruct((B, S, NA_WIDTH), BF16),
        compiler_params=_cparams(("parallel", "parallel")),
        name="na",
    )(q, k, v, bias)


def _s5_tables(lam_re, lam_im, log_dt, b_re, b_im, c_re, c_im):
    L = S5_CHUNK
    lr = jnp.minimum(lam_re.astype(F32), -1e-4)
    li = lam_im.astype(F32)
    dt = jnp.exp(log_dt.astype(F32))[:, :, None]
    mag = jnp.exp(lr * dt)
    ar = mag * jnp.cos(li * dt)
    ai = mag * jnp.sin(li * dt)
    den = lr * lr + li * li
    fr = ((ar - 1.0) * lr + ai * li) / den
    fi = (ai * lr - (ar - 1.0) * li) / den
    br = b_re.astype(F32)
    bi = b_im.astype(F32)
    bbr = fr[..., None] * br - fi[..., None] * bi
    bbi = fr[..., None] * bi + fi[..., None] * br
    cr = c_re.astype(F32)
    ci = c_im.astype(F32)
    j = jnp.arange(L + 1, dtype=F32)[:, None, None, None]
    pm = jnp.exp(j * (lr * dt)[None])
    pr = pm * jnp.cos(j * (li * dt)[None])
    pi_ = pm * jnp.sin(j * (li * dt)[None])
    abr = pr[..., None] * bbr[None] - pi_[..., None] * bbi[None]
    abi = pr[..., None] * bbi[None] + pi_[..., None] * bbr[None]
    kern = jnp.einsum('dgcp,jdgpe->jdgce', cr, abr) - jnp.einsum('dgcp,jdgpe->jdgce', ci, abi)
    blocks = jnp.concatenate([kern[1:L, 1][::-1], (kern[0, 0] + kern[0, 1])[None], kern[1:L, 0]], axis=0)
    strip = jnp.transpose(blocks, (1, 3, 0, 2)).reshape(S5_GROUPS, S5_GROUP_CH, (2 * L - 1) * S5_GROUP_CH)
    tmat = jnp.pad(strip, ((0, 0), (0, 0), (0, S5_STRIP_LANES - strip.shape[-1])))
    ef_r = abr[:L, 0][::-1]
    ef_i = abi[:L, 0][::-1]
    eb_r = abr[:L, 1]
    eb_i = abi[:L, 1]

    def to_rows(a):
        return jnp.transpose(a, (1, 0, 3, 2)).reshape(S5_GROUPS, L * S5_GROUP_CH, S5_STATE)

    we = jnp.concatenate([to_rows(ef_r), to_rows(eb_r), to_rows(ef_i), to_rows(eb_i)], axis=-1)
    def readout(d, descending):
        p_r = pr[1:L + 1, d]
        p_i = pi_[1:L + 1, d]
        if descending:
            p_r, p_i = p_r[::-1], p_i[::-1]
        m_re = cr[d][None] * p_r[:, :, None, :] - ci[d][None] * p_i[:, :, None, :]
        m_im = cr[d][None] * p_i[:, :, None, :] + ci[d][None] * p_r[:, :, None, :]
        to_cols = lambda a: jnp.transpose(a, (1, 3, 0, 2)).reshape(S5_GROUPS, S5_STATE, L * S5_GROUP_CH)
        return to_cols(m_re), to_cols(-m_im)

    yf_re, yf_im = readout(0, False)
    yb_re, yb_im = readout(1, True)
    wy = jnp.concatenate([yf_re, yb_re, yf_im, yb_im], axis=1)
    al = jnp.stack([jnp.concatenate([pr[L, 0], pr[L, 1]], axis=-1),
                    jnp.concatenate([pi_[L, 0], pi_[L, 1]], axis=-1)], axis=1)
    return tmat, we.astype(BF16), wy.astype(BF16), al


def _s5_kernel(u_ref, t_ref, we_ref, wy_ref, al_ref, y_ref, e_ref, xp_ref, *, n_chunks, batch):
    lc = S5_CHUNK * S5_GROUP_CH
    half = 2 * S5_STATE
    slots = S5_BPAD // S5_GROUPS_PER_STEP
    if batch < slots:
        e_ref[...] = jnp.zeros(e_ref.shape, F32)
    for gg in range(S5_GROUPS_PER_STEP):
        u = u_ref[gg].reshape(batch * n_chunks, lc)
        strip = t_ref[gg]
        tmat = jnp.concatenate(
            [strip[:, S5_GROUP_CH * (S5_CHUNK - 1 - s):S5_GROUP_CH * (S5_CHUNK - 1 - s) + lc]
             for s in range(S5_CHUNK)], axis=0).astype(BF16)
        y_ref[gg] = jnp.dot(u, tmat, preferred_element_type=F32).reshape(batch, n_chunks, lc)
        e = jnp.dot(u, we_ref[gg], preferred_element_type=F32)
        for b in range(batch):
            for hf in range(2):
                e_ref[hf, pl.ds(gg * slots + b, n_chunks, stride=S5_BPAD), :] = e[
                    b * n_chunks:(b + 1) * n_chunks, hf * half:(hf + 1) * half]
    lane = lax.broadcasted_iota(jnp.int32, (S5_BPAD, half), 1)
    sub = lax.broadcasted_iota(jnp.int32, (S5_BPAD, half), 0)
    is_fwd = lane < S5_STATE
    a_re = jnp.broadcast_to(al_ref[0, 0:1, :], (S5_BPAD, half))
    a_im = jnp.broadcast_to(al_ref[0, 1:2, :], (S5_BPAD, half))
    for gg in range(1, S5_GROUPS_PER_STEP):
        a_re = jnp.where(sub >= gg * slots, al_ref[gg, 0:1, :], a_re)
        a_im = jnp.where(sub >= gg * slots, al_ref[gg, 1:2, :], a_im)

    def body(j, carry):
        xr, xi = carry
        rf = pl.multiple_of(j * S5_BPAD, S5_BPAD)
        rb = pl.multiple_of((n_chunks - 1 - j) * S5_BPAD, S5_BPAD)
        xp_ref[pl.ds(rf, S5_BPAD), 0:S5_STATE] = xr[:, 0:S5_STATE]
        xp_ref[pl.ds(rb, S5_BPAD), S5_STATE:half] = xr[:, S5_STATE:half]
        xp_ref[pl.ds(rf, S5_BPAD), half:half + S5_STATE] = xi[:, 0:S5_STATE]
        xp_ref[pl.ds(rb, S5_BPAD), half + S5_STATE:2 * half] = xi[:, S5_STATE:half]
        er = jnp.where(is_fwd, e_ref[0, pl.ds(rf, S5_BPAD), :], e_ref[0, pl.ds(rb, S5_BPAD), :])
        ei = jnp.where(is_fwd, e_ref[1, pl.ds(rf, S5_BPAD), :], e_ref[1, pl.ds(rb, S5_BPAD), :])
        nxr = a_re * xr - a_im * xi + er
        nxi = a_re * xi + a_im * xr + ei
        return nxr, nxi

    zero = jnp.zeros((S5_BPAD, half), F32)
    lax.fori_loop(0, n_chunks, body, (zero, zero))
    xp = xp_ref[...].astype(BF16)
    for gg in range(S5_GROUPS_PER_STEP):
        yi = jnp.dot(xp, wy_ref[gg], preferred_element_type=F32)
        e_ref[0] = yi[:, :half]
        e_ref[1] = yi[:, half:]
        for b in range(batch):
            y_ref[gg, b] += jnp.concatenate(
                [e_ref[hf, pl.ds(gg * slots + b, n_chunks, stride=S5_BPAD), :] for hf in range(2)], axis=1)


def _s5(ug, tabs, l):
    tmat, we, wy, al = tabs
    G, B, n, lc = ug.shape
    gs = S5_GROUPS_PER_STEP
    assert B * gs <= S5_BPAD and lc == 4 * S5_STATE and G % gs == 0
    rows = n * S5_BPAD
    grp = lambda shape: pl.BlockSpec((gs,) + shape, lambda g: (g,) + (0,) * len(shape))
    tab = lambda shape: pl.BlockSpec((None, gs) + shape, lambda g: (l, g) + (0,) * len(shape))
    return pl.pallas_call(
        functools.partial(_s5_kernel, n_chunks=n, batch=B),
        grid=(G // gs,),
        in_specs=[grp((B, n, lc)), tab((S5_GROUP_CH, S5_STRIP_LANES)), tab((lc, 4 * S5_STATE)),
                  tab((4 * S5_STATE, lc)),
                  tab((2, 2 * S5_STATE))],
        out_specs=grp((B, n, lc)),
        out_shape=jax.ShapeDtypeStruct((G, B, n, lc), F32),
        scratch_shapes=[pltpu.VMEM((2, rows, 2 * S5_STATE), F32), pltpu.VMEM((rows, 4 * S5_STATE), F32)],
        compiler_params=_cparams(("parallel",)),
        name="s5",
    )(ug, tmat, we, wy, al)


def _memkv_kernel(m_ref, g_ref, w_ref, o_ref):
    x = m_ref[...]
    ms = jnp.mean(x * x, axis=-1, keepdims=True)
    h = (x * lax.rsqrt(ms + RMS_EPS) * g_ref[...]).astype(BF16)
    o_ref[...] = jnp.dot(h, w_ref[...], preferred_element_type=F32).astype(BF16)


def _memkv(mem, norm_mem, w_bf):
    B, M, D = mem.shape
    L = w_bf.shape[0]
    return pl.pallas_call(
        _memkv_kernel,
        grid=(L, B),
        in_specs=[pl.BlockSpec((None, M, D), lambda l, b: (b, 0, 0)),
                  pl.BlockSpec((None, 1, D), lambda l, b: (l, 0, 0)),
                  pl.BlockSpec((None, D, 2 * XA_WIDTH), lambda l, b: (l, 0, 0))],
        out_specs=pl.BlockSpec((None, None, M, 2 * XA_WIDTH), lambda l, b: (l, b, 0, 0)),
        out_shape=jax.ShapeDtypeStruct((L, B, M, 2 * XA_WIDTH), BF16),
        compiler_params=_cparams(("parallel", "parallel")),
        name="memkv",
    )(mem, norm_mem.reshape(L, 1, D), w_bf)


def _gelu_tanh(x):
    return 0.5 * x * (1.0 + jnp.tanh(math.sqrt(2.0 / math.pi) * (x + 0.044715 * (x * x * x))))


def _router(z):
    lane = lax.broadcasted_iota(jnp.int32, z.shape, 1).astype(F32)
    ninf = jnp.full_like(z, -jnp.inf)
    far = jnp.full_like(z, 1e9)
    cm = (lane >= MOE_EXPERTS) & (lane < MOE_EXPERTS + MOE_GROUPS)
    cmax = jnp.max(jnp.where(cm, z, ninf), axis=-1, keepdims=True)
    glane = jnp.min(jnp.where(cm & (z == cmax), lane, far), axis=-1, keepdims=True)
    psum = jnp.sum(jnp.where(cm, jnp.exp(jnp.where(cm, z, cmax) - cmax), 0.0), axis=-1, keepdims=True)
    p_grp = 1.0 / psum
    f0 = (glane - MOE_EXPERTS) * MOE_EPG
    fm = (lane >= f0) & (lane < f0 + MOE_EPG)
    v1 = jnp.max(jnp.where(fm, z, ninf), axis=-1, keepdims=True)
    i1 = jnp.min(jnp.where(fm & (z == v1), lane, far), axis=-1, keepdims=True)
    fm2 = fm & (lane != i1)
    v2 = jnp.max(jnp.where(fm2, z, ninf), axis=-1, keepdims=True)
    i2 = jnp.min(jnp.where(fm2 & (z == v2), lane, far), axis=-1, keepdims=True)
    t = jnp.exp(v2 - v1)
    w1 = p_grp / (1.0 + t)
    w2 = p_grp * t / (1.0 + t)
    comb = jnp.where(lane == i1, w1, 0.0) + jnp.where(lane == i2, w2, 0.0)
    a = jnp.minimum(i1, i2) - f0
    b = jnp.maximum(i1, i2) - f0
    pair = a * (7.0 - a) * 0.5 + b - a - 1.0
    return comb, (glane - MOE_EXPERTS) * MOE_PAIRS + pair


def _merge_kernel(x_ref, ona_ref, ys5_ref, u_ref, qx_ref, gate_ref, mkv_ref,
                  d_ref, wglu_ref, wna_ref, ws5_ref, wxa_ref, wout_ref, nffn_ref, wr2_ref, br_ref,
                  xo_ref, hp_ref, meta_ref, cnt_ref, mrg_ref, ys_ref):
    tm = x_ref.shape[0]
    cw = 256
    n_cw = D_MODEL // cw
    lane = lax.broadcasted_iota(jnp.int32, (tm, 128), 1)
    first_head = lane < XA_HEAD_DIM

    def gate(branch, c):
        return gate_ref[:, branch * D_MODEL + cw * c:branch * D_MODEL + cw * (c + 1)].astype(F32)

    scores = []
    for h in range(XA_HEADS):
        cs = slice(128 * (h // 2), 128 * (h // 2 + 1))
        q2 = qx_ref[:, cs]
        qm = jnp.where(first_head if h % 2 == 0 else jnp.logical_not(first_head), q2, jnp.zeros_like(q2))
        scores.append(lax.dot_general(qm, mkv_ref[:, cs], (((1,), (1,)), ((), ())), preferred_element_type=F32))
    ona = ona_ref[...]
    merged = [gate(0, c) * jnp.dot(ona, wna_ref[:, cw * c:cw * (c + 1)], preferred_element_type=F32)
              for c in range(n_cw)]
    _group_to_time_major(ys5_ref, ys_ref, tm // S5_CHUNK)
    ys = jnp.concatenate([ys_ref[0], ys_ref[1]], axis=1)
    y = _gelu_tanh(ys + d_ref[...] * u_ref[...])
    g = jnp.dot(y.astype(BF16), wglu_ref[...], preferred_element_type=F32)
    os5 = (g[:, :S5_WIDTH] * jax.nn.sigmoid(g[:, S5_WIDTH:])).astype(BF16)
    for c in range(n_cw):
        merged[c] += gate(1, c) * jnp.dot(os5, ws5_ref[:, cw * c:cw * (c + 1)], preferred_element_type=F32)
    outs = []
    for h, s in enumerate(scores):
        m = jnp.max(s, axis=-1, keepdims=True)
        e = jnp.exp(s - m)
        l = jnp.sum(e, axis=-1, keepdims=True)
        v2 = mkv_ref[:, XA_WIDTH + 128 * (h // 2):XA_WIDTH + 128 * (h // 2 + 1)]
        outs.append(jnp.dot(e.astype(BF16), v2, preferred_element_type=F32) / l)
    oxa = jnp.concatenate([jnp.where(first_head, outs[2 * p], outs[2 * p + 1]).astype(BF16)
                           for p in range(XA_HEADS // 2)], axis=-1)
    for c in range(n_cw):
        m = merged[c] + gate(2, c) * jnp.dot(oxa, wxa_ref[:, cw * c:cw * (c + 1)], preferred_element_type=F32)
        mrg_ref[:, cw * c:cw * (c + 1)] = m.astype(BF16)
    xn = x_ref[...] + jnp.dot(mrg_ref[...], wout_ref[...], preferred_element_type=F32)
    xo_ref[...] = xn
    ms = jnp.mean(xn * xn, axis=-1, keepdims=True)
    h2 = xn * lax.rsqrt(ms + RMS_EPS) * nffn_ref[...]
    hi = h2.astype(BF16)
    hi32 = hi.astype(F32)
    lo = (h2 - hi32).astype(BF16)
    z2 = jnp.dot(hi, wr2_ref[...], preferred_element_type=F32)
    z = (z2[:, :ROUTER_LANES] + z2[:, ROUTER_LANES:]
         + jnp.dot(lo, wr2_ref[:, :ROUTER_LANES], preferred_element_type=F32)) + br_ref[...]
    hw = D_MODEL // 2
    packed = pltpu.bitcast(hi32[:, :hw], jnp.uint32) | (pltpu.bitcast(hi32[:, hw:], jnp.uint32) >> 16)
    for j in range(hw // 128):
        hp_ref[pl.ds(j, tm, stride=TOKEN_REC_ROWS), :] = packed[:, 128 * j:128 * (j + 1)]
    comb, gid = _router(z)
    hp_ref[pl.ds(hw // 128, tm, stride=TOKEN_REC_ROWS), :] = pltpu.bitcast(comb, jnp.uint32)
    for j in range(hw // 128 + 1, TOKEN_REC_ROWS):
        hp_ref[pl.ds(j, tm, stride=TOKEN_REC_ROWS), :] = jnp.zeros((tm, 128), jnp.uint32)
    lanef = lane.astype(F32)
    onehot = lanef == gid
    row = lax.broadcasted_iota(jnp.int32, (tm, tm), 0)
    col = lax.broadcasted_iota(jnp.int32, (tm, tm), 1)
    tri = jnp.where(col <= row, 1.0, 0.0).astype(BF16)
    csum = jnp.dot(tri, jnp.where(onehot, 1.0, 0.0).astype(BF16), preferred_element_type=F32)
    rank = jnp.sum(jnp.where(onehot, csum, 0.0), axis=-1, keepdims=True) - 1.0
    meta = jnp.where(lane == 0, gid, jnp.where(lane == 1, rank, 0.0))
    meta_ref[...] = jnp.transpose(meta)[:8, :]
    cnt_ref[...] = jnp.broadcast_to(csum[tm - 1:tm, :], (8, 128))


def _merge(x, ona, ys5, u, qx, gates, mkv, params, l):
    B, S, D = x.shape
    tm = min(TOKEN_TILE, S)
    M = mkv.shape[2]
    n_i = S // tm

    def tok(width):
        return pl.BlockSpec((None, tm, width), lambda b, i: (b, i, 0))

    return pl.pallas_call(
        _merge_kernel,
        grid=(B, n_i),
        in_specs=[tok(D), tok(NA_WIDTH),
                  pl.BlockSpec((S5_GROUPS, None, tm // S5_CHUNK, S5_CHUNK * S5_GROUP_CH), lambda b, i: (0, b, i, 0)),
                  tok(S5_WIDTH), tok(XA_WIDTH), tok(N_BRANCHES * D),
                  pl.BlockSpec((None, None, M, 2 * XA_WIDTH), lambda b, i: (l, b, 0, 0))]
                 + [_layer_spec(p, l, 2) for p in params],
        out_specs=[tok(D), pl.BlockSpec((None, tm * TOKEN_REC_ROWS, 128), lambda b, i: (b, i, 0)),
                   pl.BlockSpec((None, None, 8, tm), lambda b, i: (b, i, 0, 0)),
                   pl.BlockSpec((None, None, 8, 128), lambda b, i: (b, i, 0, 0))],
        out_shape=[jax.ShapeDtypeStruct((B, S, D), F32),
                   jax.ShapeDtypeStruct((B, S * TOKEN_REC_ROWS, 128), jnp.uint32),
                   jax.ShapeDtypeStruct((B, n_i, 8, tm), F32),
                   jax.ShapeDtypeStruct((B, n_i, 8, 128), F32)],
        scratch_shapes=[pltpu.VMEM((tm, D), BF16), pltpu.VMEM((2, tm, 128), F32)],
        compiler_params=_cparams(("parallel", "parallel")),
        name="merge",
    )(x, ona, ys5, u, qx, gates, mkv, *params)


def _moe_plan(meta, cnt):
    B, n_i, _, tm = meta.shape
    n_tokens = B * n_i * tm
    gid = meta[:, :, 0, :].reshape(B * n_i, tm).astype(jnp.int32)
    rank = meta[:, :, 1, :].reshape(B * n_i, tm).astype(jnp.int32)
    tile_counts = cnt[:, :, 0, :MOE_BINS].reshape(B * n_i, MOE_BINS).astype(jnp.int32)
    counts = jnp.sum(tile_counts, axis=0)
    padded = ((counts + MOE_TILE - 1) // MOE_TILE) * MOE_TILE
    ends = jnp.cumsum(padded)
    tile_base = (ends - padded)[None, :] + jnp.cumsum(tile_counts, axis=0) - tile_counts
    pos = rank
    for g in range(MOE_BINS):
        pos = pos + jnp.where(gid == g, tile_base[:, g:g + 1], 0)
    pos = pos.reshape(n_tokens)
    n_tiles = n_tokens // MOE_TILE + MOE_BINS
    starts = jnp.arange(n_tiles, dtype=jnp.int32) * MOE_TILE
    tile_bin = jnp.minimum(jnp.sum(starts[:, None] >= ends[None, :], axis=1), MOE_BINS - 1).astype(jnp.int32)
    first = (tile_bin // MOE_PAIRS) * MOE_EPG
    pair = tile_bin % MOE_PAIRS
    tile_e1 = first + jnp.asarray(MOE_PAIR_A, jnp.int32)[pair]
    tile_e2 = first + jnp.asarray(MOE_PAIR_B, jnp.int32)[pair]
    return pos.astype(jnp.int32), ends.astype(jnp.int32), padded.astype(jnp.int32), tile_e1, tile_e2


def _rec(ref, token, count=1):
    return ref.at[pl.ds(pl.multiple_of(token * TOKEN_REC_ROWS, TOKEN_REC_ROWS), count * TOKEN_REC_ROWS), :]


DMA_UNROLL = 8


def _issue_record_dmas(count, copy_of):
    def body(blk, c):
        for k in range(DMA_UNROLL):
            copy_of(blk * DMA_UNROLL + k).start(priority=k % 2)
        return c
    lax.fori_loop(0, count // DMA_UNROLL, body, 0)


def _permute_kernel(ends_ref, padded_ref, pos_ref, h_ref, hs_ref, zero_ref, sem):
    tm = h_ref.shape[0] // TOKEN_REC_ROWS

    @pl.when(pl.program_id(0) == 0)
    def _():
        zero_ref[...] = jnp.zeros(zero_ref.shape, zero_ref.dtype)
        n_rows = hs_ref.shape[0] // TOKEN_REC_ROWS
        for g in range(MOE_BINS):
            tail = ends_ref[MOE_BINS - 1] + g * MOE_TILE
            for cond, start in ((padded_ref[g] > 0, ends_ref[g] - MOE_TILE), (tail < n_rows, tail)):
                @pl.when(cond)
                def _():
                    cp = pltpu.make_async_copy(zero_ref, _rec(hs_ref, start, MOE_TILE), sem)
                    cp.start()
                    cp.wait()

    _issue_record_dmas(tm, lambda r: pltpu.make_async_copy(_rec(h_ref, r), _rec(hs_ref, pos_ref[0, 0, r]), sem))
    pltpu.make_async_copy(h_ref, _rec(hs_ref, 0, tm), sem).wait()


def _moe_permute(hp2, pos3, ends, padded, n_rows):
    T = hp2.shape[0] // TOKEN_REC_ROWS
    tm = min(TOKEN_TILE, T)
    return pl.pallas_call(
        _permute_kernel,
        grid_spec=pltpu.PrefetchScalarGridSpec(
            num_scalar_prefetch=2,
            grid=(T // tm,),
            in_specs=[pl.BlockSpec((1, 1, tm), lambda i, e, p: (i, 0, 0), memory_space=pltpu.SMEM),
                      pl.BlockSpec((tm * TOKEN_REC_ROWS, 128), lambda i, e, p: (i, 0))],
            out_specs=pl.BlockSpec(memory_space=pl.ANY),
            scratch_shapes=[pltpu.VMEM((MOE_TILE * TOKEN_REC_ROWS, 128), jnp.uint32), pltpu.SemaphoreType.DMA(())],
        ),
        out_shape=jax.ShapeDtypeStruct((n_rows * TOKEN_REC_ROWS, 128), jnp.uint32),
        compiler_params=_cparams(("arbitrary",)),
        name="moe_permute",
    )(ends, padded, pos3, hp2)


def _moe_kernel(e1_ref, e2_ref, h_ref, wg1_ref, wu1_ref, wd1_ref, wg2_ref, wu2_ref, wd2_ref, o_ref):
    hw = D_MODEL // 2

    def field(j):
        return h_ref[pl.ds(j, MOE_TILE, stride=TOKEN_REC_ROWS), :]

    w = jnp.concatenate([field(j) for j in range(hw // 128)], axis=1)
    h = jnp.concatenate([pltpu.bitcast(w & jnp.uint32(0xFFFF0000), F32).astype(BF16),
                         pltpu.bitcast(w << 16, F32).astype(BF16)], axis=1)
    comb = pltpu.bitcast(field(hw // 128), F32)
    lane = lax.broadcasted_iota(jnp.int32, comb.shape, 1)
    i = pl.program_id(0)
    out = None
    for e_ref, wg_ref, wu_ref, wd_ref in ((e1_ref, wg1_ref, wu1_ref, wd1_ref), (e2_ref, wg2_ref, wu2_ref, wd2_ref)):
        c = jnp.sum(jnp.where(lane == e_ref[i], comb, 0.0), axis=-1, keepdims=True)
        g = jnp.dot(h, wg_ref[...], preferred_element_type=F32)
        u = jnp.dot(h, wu_ref[...], preferred_element_type=F32)
        a = (g * jax.nn.sigmoid(g) * u * c).astype(BF16)
        part = jnp.dot(a, wd_ref[...], preferred_element_type=F32)
        out = part if out is None else out + part
    for j in range(TOKEN_REC_ROWS):
        o_ref[pl.ds(j, MOE_TILE, stride=TOKEN_REC_ROWS), :] = out[:, 128 * j:128 * (j + 1)]


def _moe_experts(hs, tile_e1, tile_e2, wg, wu, wd, l):
    n_rows = hs.shape[0] // TOKEN_REC_ROWS
    n_tiles = n_rows // MOE_TILE
    rec_tile = pl.BlockSpec((MOE_TILE * TOKEN_REC_ROWS, 128), lambda i, e1, e2: (i, 0))

    def expert(shape, which):
        if which == 0:
            return pl.BlockSpec((None, None) + shape, lambda i, e1, e2: (l, e1[i], 0, 0))
        return pl.BlockSpec((None, None) + shape, lambda i, e1, e2: (l, e2[i], 0, 0))

    up, down = (D_MODEL, MOE_FF), (MOE_FF, D_MODEL)
    return pl.pallas_call(
        _moe_kernel,
        grid_spec=pltpu.PrefetchScalarGridSpec(
            num_scalar_prefetch=2,
            grid=(n_tiles,),
            in_specs=[rec_tile, expert(up, 0), expert(up, 0), expert(down, 0),
                      expert(up, 1), expert(up, 1), expert(down, 1)],
            out_specs=rec_tile,
        ),
        out_shape=jax.ShapeDtypeStruct((n_rows * TOKEN_REC_ROWS, 128), F32),
        compiler_params=_cparams(("arbitrary",)),
        name="moe_experts",
    )(tile_e1, tile_e2, hs, wg, wu, wd, wg, wu, wd)


def _final_kernel(pos_ref, x_ref, ys_ref, fn_ref, o_ref, buf_ref, sem):
    tm = x_ref.shape[0]
    _issue_record_dmas(tm, lambda r: pltpu.make_async_copy(_rec(ys_ref, pos_ref[0, 0, r]), _rec(buf_ref, r), sem))
    pltpu.make_async_copy(_rec(ys_ref, 0, tm), buf_ref, sem).wait()
    moe = jnp.concatenate([buf_ref[pl.ds(j, tm, stride=TOKEN_REC_ROWS), :] for j in range(TOKEN_REC_ROWS)], axis=1)
    xn = x_ref[...] + moe
    ms = jnp.mean(xn * xn, axis=-1, keepdims=True)
    o_ref[...] = xn * lax.rsqrt(ms + RMS_EPS) * fn_ref[...]


def _final(x2, ys, pos3, fnorm):
    T, D = x2.shape
    tm = min(TOKEN_TILE, T)
    return pl.pallas_call(
        _final_kernel,
        grid=(T // tm,),
        in_specs=[pl.BlockSpec((1, 1, tm), lambda i: (i, 0, 0), memory_space=pltpu.SMEM),
                  pl.BlockSpec((tm, D), lambda i: (i, 0)),
                  pl.BlockSpec(memory_space=pl.ANY),
                  pl.BlockSpec((1, D), lambda i: (0, 0))],
        out_specs=pl.BlockSpec((tm, D), lambda i: (i, 0)),
        out_shape=jax.ShapeDtypeStruct((T, D), F32),
        scratch_shapes=[pltpu.VMEM((tm * TOKEN_REC_ROWS, 128), F32), pltpu.SemaphoreType.DMA(())],
        compiler_params=_cparams(("arbitrary",)),
        name="final_unpermute_norm",
    )(pos3, x2, ys, fnorm.reshape(1, D))


def _moe(hp, meta, cnt, wg, wu, wd, l):
    T = hp.shape[0] * hp.shape[1] // TOKEN_REC_ROWS
    tm = min(TOKEN_TILE, T)
    pos, ends, padded, tile_e1, tile_e2 = _moe_plan(meta, cnt)
    pos3 = pos.reshape(T // tm, 1, tm)
    n_rows = T + MOE_BINS * MOE_TILE
    hs = _moe_permute(hp.reshape(T * TOKEN_REC_ROWS, 128), pos3, ends, padded, n_rows)
    return _moe_experts(hs, tile_e1, tile_e2, wg, wu, wd, l), pos3


def kernel(x, mem, norm_mix, norm_ffn, norm_mem, w_in, b_gate, na_rpb, s5_lam_re, s5_lam_im, s5_log_dt, s5_b_re, s5_b_im, s5_c_re, s5_c_im, s5_d, s5_w_glu, w_mem_kv, w_br_na, w_br_s5, w_br_xa, w_out, moe_w_coarse, moe_b_coarse, moe_w_fine, moe_b_fine, moe_w_gate, moe_w_up, moe_w_down, final_norm):
    depth, D = w_in.shape[0], w_in.shape[1]
    mkv_all = _memkv(mem, norm_mem, w_mem_kv.astype(BF16))
    w_in_bf = w_in.astype(BF16)
    norm_mix3 = norm_mix.reshape(depth, 1, D)
    b_gate3 = b_gate.reshape(depth, 1, N_BRANCHES * D)
    na_bias = jax.vmap(_na_bias_table)(na_rpb)
    s5_tabs = jax.vmap(_s5_tables)(s5_lam_re, s5_lam_im, s5_log_dt, s5_b_re, s5_b_im, s5_c_re, s5_c_im)
    wr = jnp.concatenate([moe_w_fine, moe_w_coarse], axis=2).astype(F32)
    wr = jnp.pad(wr, ((0, 0), (0, 0), (0, ROUTER_LANES - wr.shape[2])))
    wrh = wr.astype(BF16)
    wrl = (wr - wrh.astype(F32)).astype(BF16)
    br = jnp.pad(jnp.concatenate([moe_b_fine, moe_b_coarse], axis=1).astype(F32),
                 ((0, 0), (0, ROUTER_LANES - MOE_EXPERTS - MOE_GROUPS))).reshape(depth, 1, ROUTER_LANES)
    merge_params = (s5_d.reshape(depth, 1, S5_WIDTH), s5_w_glu.astype(BF16), w_br_na.astype(BF16),
                    w_br_s5.astype(BF16), w_br_xa.astype(BF16), w_out.astype(BF16),
                    norm_ffn.reshape(depth, 1, D), jnp.concatenate([wrh, wrl], axis=-1), br)
    wg, wu, wd = moe_w_gate.astype(BF16), moe_w_up.astype(BF16), moe_w_down.astype(BF16)
    moe = None
    for l in range(depth):
        if moe is None:
            q, k, v, u, ug, qx, gates = _inproj(x, norm_mix3, w_in_bf, b_gate3, l)
        else:
            q, k, v, u, ug, qx, gates, x = _inproj(x, norm_mix3, w_in_bf, b_gate3, l, moe)
        ona = _na(q, k, v, na_bias, l)
        ys5 = _s5(ug, s5_tabs, l)
        x, hp, meta, cnt = _merge(x, ona, ys5, u, qx, gates, mkv_all, merge_params, l)
        moe = _moe(hp, meta, cnt, wg, wu, wd, l)
    B, S, _ = x.shape
    ys, pos3 = moe
    return _final(x.reshape(B * S, D), ys, pos3, final_norm).reshape(B, S, D)
```

```python
import functools
import math

import numpy as np
import jax
import jax.numpy as jnp
from jax import lax
from jax.experimental import pallas as pl
from jax.experimental.pallas import tpu as pltpu

F32 = jnp.float32
BF16 = jnp.bfloat16

D_MODEL = 1024
GRID_W = 64
RMS_EPS = 1e-6
NA_HEADS = 8
NA_HEAD_DIM = 64
NA_WIDTH = NA_HEADS * NA_HEAD_DIM
NA_WIN_R = 8
NA_WIN_C = 16
NA_KV_ROWS = 24
NA_Q_ROWS = 8
NA_ROWS_PER_STEP = 8
S5_GROUPS = 16
S5_GROUP_CH = 16
S5_WIDTH = S5_GROUPS * S5_GROUP_CH
S5_STATE = 64
S5_CHUNK = 16
S5_STRIP_LANES = 512
S5_BPAD = 8
S5_GROUPS_PER_STEP = 2
XA_HEADS = 4
XA_HEAD_DIM = 64
XA_WIDTH = XA_HEADS * XA_HEAD_DIM
N_BRANCHES = 3
MOE_GROUPS = 4
MOE_EPG = 4
MOE_EXPERTS = MOE_GROUPS * MOE_EPG
MOE_FF = 256
MOE_ECHUNK = 4
ROUTER_LANES = 128
TOKEN_REC_ROWS = 8
MOE_PAIR_A = (0, 0, 0, 1, 1, 2)
MOE_PAIR_B = (1, 2, 3, 2, 3, 3)
MOE_PAIRS = len(MOE_PAIR_A)
MOE_BINS = MOE_GROUPS * MOE_PAIRS
MOE_TILE = 512
NEG_BIG = -1e30

VMEM_LIMIT = 52 * 1024 * 1024
TOKEN_TILE = 512


def _cparams(sem):
    return pltpu.CompilerParams(dimension_semantics=sem, vmem_limit_bytes=VMEM_LIMIT)


def _lane_block_mask(rows, j):
    lane = lax.broadcasted_iota(jnp.int32, (rows, 128), 1)
    lo = S5_GROUP_CH * (j % 8)
    return (lane >= lo) & (lane < lo + S5_GROUP_CH)


def _time_to_group_major(src_ref, rows):
    dest = [[None, None] for _ in range(S5_GROUPS)]
    for s in range(S5_CHUNK):
        halves = tuple(src_ref[hf, pl.ds(s, rows, stride=S5_CHUNK), :] for hf in range(2))
        m = _lane_block_mask(rows, s)
        for g in range(S5_GROUPS):
            shift = (S5_GROUP_CH * ((s % 8) - (g % 8))) % 128
            r = pltpu.roll(halves[g // 8], shift, axis=1) if shift else halves[g // 8]
            prev = dest[g][s // 8]
            dest[g][s // 8] = jnp.where(m, r, 0.0 if prev is None else prev)
    return [jnp.concatenate(d, axis=1) for d in dest]


def _group_to_time_major(src_ref, dst_ref, rows):
    for t in range(S5_CHUNK):
        out = [None, None]
        for g in range(S5_GROUPS):
            src = src_ref[g, :, 128 * (t // 8):128 * (t // 8 + 1)]
            shift = (S5_GROUP_CH * ((g % 8) - (t % 8))) % 128
            r = pltpu.roll(src, shift, axis=1) if shift else src
            prev = out[g // 8]
            out[g // 8] = jnp.where(_lane_block_mask(rows, g), r, 0.0 if prev is None else prev)
        for hf in range(2):
            dst_ref[hf, pl.ds(t, rows, stride=S5_CHUNK), :] = out[hf]


def _inproj_kernel(*refs, with_moe):
    if with_moe:
        (pos_ref, pos_next_ref, x_ref, ys_ref, g_ref, w_ref, bg_ref,
         q_ref, k_ref, v_ref, u_ref, ug_ref, qx_ref, gate_ref, xo_ref, uh_ref, buf_ref, sem) = refs
        tm = x_ref.shape[0]
        step = pl.program_id(0) * pl.num_programs(1) + pl.program_id(1)
        n_steps = pl.num_programs(0) * pl.num_programs(1)
        slot = lax.rem(step, 2)

        def fetch(p_ref, s):
            _issue_record_dmas(tm, lambda r: pltpu.make_async_copy(
                _rec(ys_ref, p_ref[0, 0, r]), _rec(buf_ref.at[s], r), sem.at[s]))

        @pl.when(step == 0)
        def _():
            fetch(pos_ref, 0)

        @pl.when(step + 1 < n_steps)
        def _():
            fetch(pos_next_ref, 1 - slot)

        pltpu.make_async_copy(_rec(ys_ref, 0, tm), buf_ref.at[slot], sem.at[slot]).wait()
        moe = jnp.concatenate([buf_ref[slot, pl.ds(j, tm, stride=TOKEN_REC_ROWS), :]
                               for j in range(TOKEN_REC_ROWS)], axis=1)
        x = x_ref[...] + moe
        xo_ref[...] = x
    else:
        x_ref, g_ref, w_ref, bg_ref, q_ref, k_ref, v_ref, u_ref, ug_ref, qx_ref, gate_ref, uh_ref = refs
        x = x_ref[...]
    ms = jnp.mean(x * x, axis=-1, keepdims=True)
    h = (x * lax.rsqrt(ms + RMS_EPS) * g_ref[...]).astype(BF16)

    def proj(a, b):
        return jnp.dot(h, w_ref[:, a:b], preferred_element_type=F32)

    c0 = NA_WIDTH
    q_ref[...] = (proj(0, c0) * (NA_HEAD_DIM ** -0.5)).astype(BF16)
    k_ref[...] = proj(c0, 2 * c0).astype(BF16)
    v_ref[...] = proj(2 * c0, 3 * c0).astype(BF16)
    c1 = 3 * c0
    u = proj(c1, c1 + S5_WIDTH)
    u_ref[...] = u
    uh_ref[0] = u[:, :128]
    uh_ref[1] = u[:, 128:]
    for g, ug in enumerate(_time_to_group_major(uh_ref, ug_ref.shape[1])):
        ug_ref[g] = ug.astype(BF16)
    c2 = c1 + S5_WIDTH
    qx_ref[...] = (proj(c2, c2 + XA_WIDTH) * (XA_HEAD_DIM ** -0.5)).astype(BF16)
    c3 = c2 + XA_WIDTH
    for j in range(N_BRANCHES):
        z = proj(c3 + D_MODEL * j, c3 + D_MODEL * (j + 1)) + bg_ref[:, D_MODEL * j:D_MODEL * (j + 1)]
        gate_ref[:, D_MODEL * j:D_MODEL * (j + 1)] = jax.nn.sigmoid(z).astype(BF16)


def _layer_spec(a, l, n_grid):
    zeros = (0,) * (a.ndim - 1)
    if n_grid == 1:
        index_map = lambda i: (l,) + zeros
    elif n_grid == 2:
        index_map = lambda b, i: (l,) + zeros
    else:
        raise ValueError(n_grid)
    return pl.BlockSpec((None,) + a.shape[1:], index_map, pipeline_mode=pl.Buffered(1))


def _inproj(x, g, w_bf, bg, l, moe=None):
    B, S, D = x.shape
    tm = min(TOKEN_TILE, S)
    n_i = S // tm

    def tok(width):
        return pl.BlockSpec((None, tm, width), lambda b, i: (b, i, 0))

    in_specs = [tok(D), _layer_spec(g, l, 2), _layer_spec(w_bf, l, 2), _layer_spec(bg, l, 2)]
    operands = [x, g, w_bf, bg]
    out_specs = [tok(NA_WIDTH), tok(NA_WIDTH), tok(NA_WIDTH), tok(S5_WIDTH),
                 pl.BlockSpec((S5_GROUPS, None, tm // S5_CHUNK, S5_CHUNK * S5_GROUP_CH), lambda b, i: (0, b, i, 0)),
                 tok(XA_WIDTH), tok(N_BRANCHES * D)]
    out_shape = [
        jax.ShapeDtypeStruct((B, S, NA_WIDTH), BF16),
        jax.ShapeDtypeStruct((B, S, NA_WIDTH), BF16),
        jax.ShapeDtypeStruct((B, S, NA_WIDTH), BF16),
        jax.ShapeDtypeStruct((B, S, S5_WIDTH), F32),
        jax.ShapeDtypeStruct((S5_GROUPS, B, S // S5_CHUNK, S5_CHUNK * S5_GROUP_CH), BF16),
        jax.ShapeDtypeStruct((B, S, XA_WIDTH), BF16),
        jax.ShapeDtypeStruct((B, S, N_BRANCHES * D), BF16),
    ]
    scratch = [pltpu.VMEM((2, tm, 128), F32)]
    if moe is not None:
        ys, pos3 = moe
        last = B * n_i - 1
        pos_spec = lambda nxt: pl.BlockSpec(
            (1, 1, tm), lambda b, i: (jnp.minimum(b * n_i + i + nxt, last), 0, 0), memory_space=pltpu.SMEM)
        in_specs = [pos_spec(0), pos_spec(1), in_specs[0], pl.BlockSpec(memory_space=pl.ANY)] + in_specs[1:]
        operands = [pos3, pos3, x, ys] + operands[1:]
        out_specs.append(tok(D))
        out_shape.append(jax.ShapeDtypeStruct((B, S, D), F32))
        scratch += [pltpu.VMEM((2, tm * TOKEN_REC_ROWS, 128), F32), pltpu.SemaphoreType.DMA((2,))]
    return pl.pallas_call(
        functools.partial(_inproj_kernel, with_moe=moe is not None),
        grid=(B, n_i),
        in_specs=in_specs,
        out_specs=out_specs,
        out_shape=out_shape,
        scratch_shapes=scratch,
        compiler_params=_cparams(("arbitrary", "arbitrary") if moe is not None else ("parallel", "parallel")),
        name="inproj",
    )(*operands)


def _na_bias_table(rpb):
    cols = np.arange(GRID_W)
    col_start = np.clip(cols - NA_WIN_C // 2, 0, GRID_W - NA_WIN_C)
    cj = np.arange(GRID_W)[None, :]
    valid = (cj >= col_start[:, None]) & (cj < col_start[:, None] + NA_WIN_C)
    col_idx = np.clip(cj - cols[:, None] + (NA_WIN_C - 1), 0, 2 * NA_WIN_C - 2)
    col_sel = ((col_idx[:, :, None] == np.arange(2 * NA_WIN_C - 1)) & valid[:, :, None]).astype(np.float32)
    t = jnp.einsum('hrk,cjk->hrcj', rpb.astype(F32), col_sel, precision=lax.Precision.HIGHEST)
    t = t + jnp.where(valid, 0.0, NEG_BIG)[None, None].astype(F32)
    return jnp.concatenate([t[:, :-1], t[:, 1:]], axis=-1)


def _na_kernel(q_ref, k_ref, v_ref, bias_ref, o_ref, *, rows_total):
    r0 = pl.program_id(1) * NA_Q_ROWS
    kstart = jnp.clip(r0 - NA_Q_ROWS, 0, rows_total - NA_KV_ROWS)
    lane = lax.broadcasted_iota(jnp.int32, (GRID_W, 128), 1)
    first_head = lane < NA_HEAD_DIM
    win = NA_WIN_R * GRID_W

    def body(it, carry):
        rows = []
        for sub in range(NA_ROWS_PER_STEP):
            i = it * NA_ROWS_PER_STEP + sub
            r = r0 + i
            rs = jnp.clip(r - NA_WIN_R // 2, 0, rows_total - NA_WIN_R)
            rows.append((pl.multiple_of(i * GRID_W, GRID_W), pl.multiple_of((rs - kstart) * GRID_W, GRID_W), r - rs))
        scores = []
        for qoff, koff, didx in rows:
            for h in range(NA_HEADS):
                cs = slice(128 * (h // 2), 128 * (h // 2 + 1))
                q2 = q_ref[pl.ds(qoff, GRID_W), cs]
                qm = jnp.where(first_head if h % 2 == 0 else jnp.logical_not(first_head), q2, jnp.zeros_like(q2))
                s = lax.dot_general(qm, k_ref[pl.ds(koff, win), cs], (((1,), (1,)), ((), ())),
                                    preferred_element_type=F32)
                bias = [bias_ref[h, 2 * kk + (NA_WIN_R - 1) - didx] for kk in range(NA_WIN_R // 2)]
                scores.append(s + jnp.concatenate(bias, axis=1))
        probs = []
        for s in scores:
            m = jnp.max(s, axis=-1, keepdims=True)
            e = jnp.exp(s - m)
            probs.append((e.astype(BF16), jnp.sum(e, axis=-1, keepdims=True)))
        for n, (qoff, koff, didx) in enumerate(rows):
            outs = []
            for h in range(NA_HEADS):
                e, l = probs[n * NA_HEADS + h]
                cs = slice(128 * (h // 2), 128 * (h // 2 + 1))
                outs.append(jnp.dot(e, v_ref[pl.ds(koff, win), cs], preferred_element_type=F32) / l)
            for p in range(NA_HEADS // 2):
                o_ref[pl.ds(qoff, GRID_W), 128 * p:128 * (p + 1)] = jnp.where(
                    first_head, outs[2 * p], outs[2 * p + 1]).astype(BF16)
        return carry

    lax.fori_loop(0, NA_Q_ROWS // NA_ROWS_PER_STEP, body, 0)


def _na(q, k, v, bias, l):
    B, S, _ = q.shape
    rows = S // GRID_W
    assert rows >= NA_KV_ROWS and rows % NA_Q_ROWS == 0
    qtok = NA_Q_ROWS * GRID_W
    kvtok = NA_KV_ROWS * GRID_W

    def kv_map(b, rb):
        return (b, jnp.clip(rb * NA_Q_ROWS - NA_Q_ROWS, 0, rows - NA_KV_ROWS) * GRID_W, 0)

    kv_spec = pl.BlockSpec((None, pl.Element(kvtok), pl.Element(NA_WIDTH)), kv_map)
    return pl.pallas_call(
        functools.partial(_na_kernel, rows_total=rows),
        grid=(B, rows // NA_Q_ROWS),
        in_specs=[
            pl.BlockSpec((None, qtok, NA_WIDTH), lambda b, rb: (b, rb, 0)),
            kv_spec,
            kv_spec,
            _layer_spec(bias, l, 2),
        ],
        out_specs=pl.BlockSpec((None, qtok, NA_WIDTH), lambda b, rb: (b, rb, 0)),
        out_shape=jax.ShapeDtypeStruct((B, S, NA_WIDTH), BF16),
        compiler_params=_cparams(("parallel", "parallel")),
        name="na",
    )(q, k, v, bias)


def _s5_tables(lam_re, lam_im, log_dt, b_re, b_im, c_re, c_im):
    L = S5_CHUNK
    lr = jnp.minimum(lam_re.astype(F32), -1e-4)
    li = lam_im.astype(F32)
    dt = jnp.exp(log_dt.astype(F32))[:, :, None]
    mag = jnp.exp(lr * dt)
    ar = mag * jnp.cos(li * dt)
    ai = mag * jnp.sin(li * dt)
    den = lr * lr + li * li
    fr = ((ar - 1.0) * lr + ai * li) / den
    fi = (ai * lr - (ar - 1.0) * li) / den
    br = b_re.astype(F32)
    bi = b_im.astype(F32)
    bbr = fr[..., None] * br - fi[..., None] * bi
    bbi = fr[..., None] * bi + fi[..., None] * br
    cr = c_re.astype(F32)
    ci = c_im.astype(F32)
    j = jnp.arange(L + 1, dtype=F32)[:, None, None, None]
    pm = jnp.exp(j * (lr * dt)[None])
    pr = pm * jnp.cos(j * (li * dt)[None])
    pi_ = pm * jnp.sin(j * (li * dt)[None])
    abr = pr[..., None] * bbr[None] - pi_[..., None] * bbi[None]
    abi = pr[..., None] * bbi[None] + pi_[..., None] * bbr[None]
    kern = jnp.einsum('dgcp,jdgpe->jdgce', cr, abr) - jnp.einsum('dgcp,jdgpe->jdgce', ci, abi)
    blocks = jnp.concatenate([kern[1:L, 1][::-1], (kern[0, 0] + kern[0, 1])[None], kern[1:L, 0]], axis=0)
    strip = jnp.transpose(blocks, (1, 3, 0, 2)).reshape(S5_GROUPS, S5_GROUP_CH, (2 * L - 1) * S5_GROUP_CH)
    tmat = jnp.pad(strip, ((0, 0), (0, 0), (0, S5_STRIP_LANES - strip.shape[-1])))
    ef_r = abr[:L, 0][::-1]
    ef_i = abi[:L, 0][::-1]
    eb_r = abr[:L, 1]
    eb_i = abi[:L, 1]

    def to_rows(a):
        return jnp.transpose(a, (1, 0, 3, 2)).reshape(S5_GROUPS, L * S5_GROUP_CH, S5_STATE)

    we = jnp.concatenate([to_rows(ef_r), to_rows(eb_r), to_rows(ef_i), to_rows(eb_i)], axis=-1)
    def readout(d, descending):
        p_r = pr[1:L + 1, d]
        p_i = pi_[1:L + 1, d]
        if descending:
            p_r, p_i = p_r[::-1], p_i[::-1]
        m_re = cr[d][None] * p_r[:, :, None, :] - ci[d][None] * p_i[:, :, None, :]
        m_im = cr[d][None] * p_i[:, :, None, :] + ci[d][None] * p_r[:, :, None, :]
        to_cols = lambda a: jnp.transpose(a, (1, 3, 0, 2)).reshape(S5_GROUPS, S5_STATE, L * S5_GROUP_CH)
        return to_cols(m_re), to_cols(-m_im)

    yf_re, yf_im = readout(0, False)
    yb_re, yb_im = readout(1, True)
    wy = jnp.concatenate([yf_re, yb_re, yf_im, yb_im], axis=1)
    al = jnp.stack([jnp.concatenate([pr[L, 0], pr[L, 1]], axis=-1),
                    jnp.concatenate([pi_[L, 0], pi_[L, 1]], axis=-1)], axis=1)
    return tmat, we.astype(BF16), wy.astype(BF16), al


def _s5_kernel(u_ref, t_ref, we_ref, wy_ref, al_ref, y_ref, e_ref, xp_ref, *, n_chunks, batch):
    lc = S5_CHUNK * S5_GROUP_CH
    half = 2 * S5_STATE
    slots = S5_BPAD // S5_GROUPS_PER_STEP
    if batch < slots:
        e_ref[...] = jnp.zeros(e_ref.shape, F32)
    for gg in range(S5_GROUPS_PER_STEP):
        u = u_ref[gg].reshape(batch * n_chunks, lc)
        strip = t_ref[gg]
        tmat = jnp.concatenate(
            [strip[:, S5_GROUP_CH * (S5_CHUNK - 1 - s):S5_GROUP_CH * (S5_CHUNK - 1 - s) + lc]
             for s in range(S5_CHUNK)], axis=0).astype(BF16)
        y_ref[gg] = jnp.dot(u, tmat, preferred_element_type=F32).reshape(batch, n_chunks, lc)
        e = jnp.dot(u, we_ref[gg], preferred_element_type=F32)
        for b in range(batch):
            for hf in range(2):
                e_ref[hf, pl.ds(gg * slots + b, n_chunks, stride=S5_BPAD), :] = e[
                    b * n_chunks:(b + 1) * n_chunks, hf * half:(hf + 1) * half]
    lane = lax.broadcasted_iota(jnp.int32, (S5_BPAD, half), 1)
    sub = lax.broadcasted_iota(jnp.int32, (S5_BPAD, half), 0)
    is_fwd = lane < S5_STATE
    a_re = jnp.broadcast_to(al_ref[0, 0:1, :], (S5_BPAD, half))
    a_im = jnp.broadcast_to(al_ref[0, 1:2, :], (S5_BPAD, half))
    for gg in range(1, S5_GROUPS_PER_STEP):
        a_re = jnp.where(sub >= gg * slots, al_ref[gg, 0:1, :], a_re)
        a_im = jnp.where(sub >= gg * slots, al_ref[gg, 1:2, :], a_im)

    def body(j, carry):
        xr, xi = carry
        rf = pl.multiple_of(j * S5_BPAD, S5_BPAD)
        rb = pl.multiple_of((n_chunks - 1 - j) * S5_BPAD, S5_BPAD)
        xp_ref[pl.ds(rf, S5_BPAD), 0:S5_STATE] = xr[:, 0:S5_STATE]
        xp_ref[pl.ds(rb, S5_BPAD), S5_STATE:half] = xr[:, S5_STATE:half]
        xp_ref[pl.ds(rf, S5_BPAD), half:half + S5_STATE] = xi[:, 0:S5_STATE]
        xp_ref[pl.ds(rb, S5_BPAD), half + S5_STATE:2 * half] = xi[:, S5_STATE:half]
        er = jnp.where(is_fwd, e_ref[0, pl.ds(rf, S5_BPAD), :], e_ref[0, pl.ds(rb, S5_BPAD), :])
        ei = jnp.where(is_fwd, e_ref[1, pl.ds(rf, S5_BPAD), :], e_ref[1, pl.ds(rb, S5_BPAD), :])
        nxr = a_re * xr - a_im * xi + er
        nxi = a_re * xi + a_im * xr + ei
        return nxr, nxi

    zero = jnp.zeros((S5_BPAD, half), F32)
    lax.fori_loop(0, n_chunks, body, (zero, zero))
    xp = xp_ref[...].astype(BF16)
    for gg in range(S5_GROUPS_PER_STEP):
        yi = jnp.dot(xp, wy_ref[gg], preferred_element_type=F32)
        e_ref[0] = yi[:, :half]
        e_ref[1] = yi[:, half:]
        for b in range(batch):
            y_ref[gg, b] += jnp.concatenate(
                [e_ref[hf, pl.ds(gg * slots + b, n_chunks, stride=S5_BPAD), :] for hf in range(2)], axis=1)


def _s5(ug, tabs, l):
    tmat, we, wy, al = tabs
    G, B, n, lc = ug.shape
    gs = S5_GROUPS_PER_STEP
    assert B * gs <= S5_BPAD and lc == 4 * S5_STATE and G % gs == 0
    rows = n * S5_BPAD
    grp = lambda shape: pl.BlockSpec((gs,) + shape, lambda g: (g,) + (0,) * len(shape))
    tab = lambda shape: pl.BlockSpec((None, gs) + shape, lambda g: (l, g) + (0,) * len(shape))
    return pl.pallas_call(
        functools.partial(_s5_kernel, n_chunks=n, batch=B),
        grid=(G // gs,),
        in_specs=[grp((B, n, lc)), tab((S5_GROUP_CH, S5_STRIP_LANES)), tab((lc, 4 * S5_STATE)),
                  tab((4 * S5_STATE, lc)),
                  tab((2, 2 * S5_STATE))],
        out_specs=grp((B, n, lc)),
        out_shape=jax.ShapeDtypeStruct((G, B, n, lc), F32),
        scratch_shapes=[pltpu.VMEM((2, rows, 2 * S5_STATE), F32), pltpu.VMEM((rows, 4 * S5_STATE), F32)],
        compiler_params=_cparams(("parallel",)),
        name="s5",
    )(ug, tmat, we, wy, al)


def _memkv_kernel(m_ref, g_ref, w_ref, o_ref):
    x = m_ref[...]
    ms = jnp.mean(x * x, axis=-1, keepdims=True)
    h = (x * lax.rsqrt(ms + RMS_EPS) * g_ref[...]).astype(BF16)
    o_ref[...] = jnp.dot(h, w_ref[...], preferred_element_type=F32).astype(BF16)


def _memkv(mem, norm_mem, w_bf):
    B, M, D = mem.shape
    L = w_bf.shape[0]
    return pl.pallas_call(
        _memkv_kernel,
        grid=(L, B),
        in_specs=[pl.BlockSpec((None, M, D), lambda l, b: (b, 0, 0)),
                  pl.BlockSpec((None, 1, D), lambda l, b: (l, 0, 0)),
                  pl.BlockSpec((None, D, 2 * XA_WIDTH), lambda l, b: (l, 0, 0))],
        out_specs=pl.BlockSpec((None, None, M, 2 * XA_WIDTH), lambda l, b: (l, b, 0, 0)),
        out_shape=jax.ShapeDtypeStruct((L, B, M, 2 * XA_WIDTH), BF16),
        compiler_params=_cparams(("parallel", "parallel")),
        name="memkv",
    )(mem, norm_mem.reshape(L, 1, D), w_bf)


def _gelu_tanh(x):
    return 0.5 * x * (1.0 + jnp.tanh(math.sqrt(2.0 / math.pi) * (x + 0.044715 * (x * x * x))))


def _router(z):
    lane = lax.broadcasted_iota(jnp.int32, z.shape, 1).astype(F32)
    ninf = jnp.full_like(z, -jnp.inf)
    far = jnp.full_like(z, 1e9)
    cm = (lane >= MOE_EXPERTS) & (lane < MOE_EXPERTS + MOE_GROUPS)
    cmax = jnp.max(jnp.where(cm, z, ninf), axis=-1, keepdims=True)
    glane = jnp.min(jnp.where(cm & (z == cmax), lane, far), axis=-1, keepdims=True)
    psum = jnp.sum(jnp.where(cm, jnp.exp(jnp.where(cm, z, cmax) - cmax), 0.0), axis=-1, keepdims=True)
    p_grp = 1.0 / psum
    f0 = (glane - MOE_EXPERTS) * MOE_EPG
    fm = (lane >= f0) & (lane < f0 + MOE_EPG)
    v1 = jnp.max(jnp.where(fm, z, ninf), axis=-1, keepdims=True)
    i1 = jnp.min(jnp.where(fm & (z == v1), lane, far), axis=-1, keepdims=True)
    fm2 = fm & (lane != i1)
    v2 = jnp.max(jnp.where(fm2, z, ninf), axis=-1, keepdims=True)
    i2 = jnp.min(jnp.where(fm2 & (z == v2), lane, far), axis=-1, keepdims=True)
    t = jnp.exp(v2 - v1)
    w1 = p_grp / (1.0 + t)
    w2 = p_grp * t / (1.0 + t)
    comb = jnp.where(lane == i1, w1, 0.0) + jnp.where(lane == i2, w2, 0.0)
    a = jnp.minimum(i1, i2) - f0
    b = jnp.maximum(i1, i2) - f0
    pair = a * (7.0 - a) * 0.5 + b - a - 1.0
    return comb, (glane - MOE_EXPERTS) * MOE_PAIRS + pair


def _merge_kernel(x_ref, ona_ref, ys5_ref, u_ref, qx_ref, gate_ref, mkv_ref,
                  d_ref, wglu_ref, wna_ref, ws5_ref, wxa_ref, wout_ref, nffn_ref, wr2_ref, br_ref,
                  xo_ref, hp_ref, meta_ref, cnt_ref, mrg_ref, ys_ref):
    tm = x_ref.shape[0]
    cw = 256
    n_cw = D_MODEL // cw
    lane = lax.broadcasted_iota(jnp.int32, (tm, 128), 1)
    first_head = lane < XA_HEAD_DIM

    def gate(branch, c):
        return gate_ref[:, branch * D_MODEL + cw * c:branch * D_MODEL + cw * (c + 1)].astype(F32)

    scores = []
    for h in range(XA_HEADS):
        cs = slice(128 * (h // 2), 128 * (h // 2 + 1))
        q2 = qx_ref[:, cs]
        qm = jnp.where(first_head if h % 2 == 0 else jnp.logical_not(first_head), q2, jnp.zeros_like(q2))
        scores.append(lax.dot_general(qm, mkv_ref[:, cs], (((1,), (1,)), ((), ())), preferred_element_type=F32))
    ona = ona_ref[...]
    merged = [gate(0, c) * jnp.dot(ona, wna_ref[:, cw * c:cw * (c + 1)], preferred_element_type=F32)
              for c in range(n_cw)]
    _group_to_time_major(ys5_ref, ys_ref, tm // S5_CHUNK)
    ys = jnp.concatenate([ys_ref[0], ys_ref[1]], axis=1)
    y = _gelu_tanh(ys + d_ref[...] * u_ref[...])
    g = jnp.dot(y.astype(BF16), wglu_ref[...], preferred_element_type=F32)
    os5 = (g[:, :S5_WIDTH] * jax.nn.sigmoid(g[:, S5_WIDTH:])).astype(BF16)
    for c in range(n_cw):
        merged[c] += gate(1, c) * jnp.dot(os5, ws5_ref[:, cw * c:cw * (c + 1)], preferred_element_type=F32)
    outs = []
    for h, s in enumerate(scores):
        m = jnp.max(s, axis=-1, keepdims=True)
        e = jnp.exp(s - m)
        l = jnp.sum(e, axis=-1, keepdims=True)
        v2 = mkv_ref[:, XA_WIDTH + 128 * (h // 2):XA_WIDTH + 128 * (h // 2 + 1)]
        outs.append(jnp.dot(e.astype(BF16), v2, preferred_element_type=F32) / l)
    oxa = jnp.concatenate([jnp.where(first_head, outs[2 * p], outs[2 * p + 1]).astype(BF16)
                           for p in range(XA_HEADS // 2)], axis=-1)
    for c in range(n_cw):
        m = merged[c] + gate(2, c) * jnp.dot(oxa, wxa_ref[:, cw * c:cw * (c + 1)], preferred_element_type=F32)
        mrg_ref[:, cw * c:cw * (c + 1)] = m.astype(BF16)
    xn = x_ref[...] + jnp.dot(mrg_ref[...], wout_ref[...], preferred_element_type=F32)
    xo_ref[...] = xn
    ms = jnp.mean(xn * xn, axis=-1, keepdims=True)
    h2 = xn * lax.rsqrt(ms + RMS_EPS) * nffn_ref[...]
    hi = h2.astype(BF16)
    hi32 = hi.astype(F32)
    lo = (h2 - hi32).astype(BF16)
    z2 = jnp.dot(hi, wr2_ref[...], preferred_element_type=F32)
    z = (z2[:, :ROUTER_LANES] + z2[:, ROUTER_LANES:]
         + jnp.dot(lo, wr2_ref[:, :ROUTER_LANES], preferred_element_type=F32)) + br_ref[...]
    hw = D_MODEL // 2
    packed = pltpu.bitcast(hi32[:, :hw], jnp.uint32) | (pltpu.bitcast(hi32[:, hw:], jnp.uint32) >> 16)
    for j in range(hw // 128):
        hp_ref[pl.ds(j, tm, stride=TOKEN_REC_ROWS), :] = packed[:, 128 * j:128 * (j + 1)]
    comb, gid = _router(z)
    hp_ref[pl.ds(hw // 128, tm, stride=TOKEN_REC_ROWS), :] = pltpu.bitcast(comb, jnp.uint32)
    for j in range(hw // 128 + 1, TOKEN_REC_ROWS):
        hp_ref[pl.ds(j, tm, stride=TOKEN_REC_ROWS), :] = jnp.zeros((tm, 128), jnp.uint32)
    lanef = lane.astype(F32)
    onehot = lanef == gid
    row = lax.broadcasted_iota(jnp.int32, (tm, tm), 0)
    col = lax.broadcasted_iota(jnp.int32, (tm, tm), 1)
    tri = jnp.where(col <= row, 1.0, 0.0).astype(BF16)
    csum = jnp.dot(tri, jnp.where(onehot, 1.0, 0.0).astype(BF16), preferred_element_type=F32)
    rank = jnp.sum(jnp.where(onehot, csum, 0.0), axis=-1, keepdims=True) - 1.0
    meta = jnp.where(lane == 0, gid, jnp.where(lane == 1, rank, 0.0))
    meta_ref[...] = jnp.transpose(meta)[:8, :]
    cnt_ref[...] = jnp.broadcast_to(csum[tm - 1:tm, :], (8, 128))


def _merge(x, ona, ys5, u, qx, gates, mkv, params, l):
    B, S, D = x.shape
    tm = min(TOKEN_TILE, S)
    M = mkv.shape[2]
    n_i = S // tm

    def tok(width):
        return pl.BlockSpec((None, tm, width), lambda b, i: (b, i, 0))

    return pl.pallas_call(
        _merge_kernel,
        grid=(B, n_i),
        in_specs=[tok(D), tok(NA_WIDTH),
                  pl.BlockSpec((S5_GROUPS, None, tm // S5_CHUNK, S5_CHUNK * S5_GROUP_CH), lambda b, i: (0, b, i, 0)),
                  tok(S5_WIDTH), tok(XA_WIDTH), tok(N_BRANCHES * D),
                  pl.BlockSpec((None, None, M, 2 * XA_WIDTH), lambda b, i: (l, b, 0, 0))]
                 + [_layer_spec(p, l, 2) for p in params],
        out_specs=[tok(D), pl.BlockSpec((None, tm * TOKEN_REC_ROWS, 128), lambda b, i: (b, i, 0)),
                   pl.BlockSpec((None, None, 8, tm), lambda b, i: (b, i, 0, 0)),
                   pl.BlockSpec((None, None, 8, 128), lambda b, i: (b, i, 0, 0))],
        out_shape=[jax.ShapeDtypeStruct((B, S, D), F32),
                   jax.ShapeDtypeStruct((B, S * TOKEN_REC_ROWS, 128), jnp.uint32),
                   jax.ShapeDtypeStruct((B, n_i, 8, tm), F32),
                   jax.ShapeDtypeStruct((B, n_i, 8, 128), F32)],
        scratch_shapes=[pltpu.VMEM((tm, D), BF16), pltpu.VMEM((2, tm, 128), F32)],
        compiler_params=_cparams(("parallel", "parallel")),
        name="merge",
    )(x, ona, ys5, u, qx, gates, mkv, *params)


def _moe_plan(meta, cnt):
    B, n_i, _, tm = meta.shape
    n_tokens = B * n_i * tm
    gid = meta[:, :, 0, :].reshape(B * n_i, tm).astype(jnp.int32)
    rank = meta[:, :, 1, :].reshape(B * n_i, tm).astype(jnp.int32)
    tile_counts = cnt[:, :, 0, :MOE_BINS].reshape(B * n_i, MOE_BINS).astype(jnp.int32)
    counts = jnp.sum(tile_counts, axis=0)
    padded = ((counts + MOE_TILE - 1) // MOE_TILE) * MOE_TILE
    ends = jnp.cumsum(padded)
    tile_base = (ends - padded)[None, :] + jnp.cumsum(tile_counts, axis=0) - tile_counts
    pos = rank
    for g in range(MOE_BINS):
        pos = pos + jnp.where(gid == g, tile_base[:, g:g + 1], 0)
    pos = pos.reshape(n_tokens)
    n_tiles = n_tokens // MOE_TILE + MOE_BINS
    starts = jnp.arange(n_tiles, dtype=jnp.int32) * MOE_TILE
    tile_bin = jnp.minimum(jnp.sum(starts[:, None] >= ends[None, :], axis=1), MOE_BINS - 1).astype(jnp.int32)
    first = (tile_bin // MOE_PAIRS) * MOE_EPG
    pair = tile_bin % MOE_PAIRS
    tile_e1 = first + jnp.asarray(MOE_PAIR_A, jnp.int32)[pair]
    tile_e2 = first + jnp.asarray(MOE_PAIR_B, jnp.int32)[pair]
    return pos.astype(jnp.int32), ends.astype(jnp.int32), padded.astype(jnp.int32), tile_e1, tile_e2


def _rec(ref, token, count=1):
    return ref.at[pl.ds(pl.multiple_of(token * TOKEN_REC_ROWS, TOKEN_REC_ROWS), count * TOKEN_REC_ROWS), :]


DMA_UNROLL = 8


def _issue_record_dmas(count, copy_of):
    def body(blk, c):
        for k in range(DMA_UNROLL):
            copy_of(blk * DMA_UNROLL + k).start(priority=k % 2)
        return c
    lax.fori_loop(0, count // DMA_UNROLL, body, 0)


def _permute_kernel(ends_ref, padded_ref, pos_ref, h_ref, hs_ref, zero_ref, sem):
    tm = h_ref.shape[0] // TOKEN_REC_ROWS

    @pl.when(pl.program_id(0) == 0)
    def _():
        zero_ref[...] = jnp.zeros(zero_ref.shape, zero_ref.dtype)
        n_rows = hs_ref.shape[0] // TOKEN_REC_ROWS
        fills = []
        for g in range(MOE_BINS):
            tail = ends_ref[MOE_BINS - 1] + g * MOE_TILE
            fills += [(padded_ref[g] > 0, ends_ref[g] - MOE_TILE), (tail < n_rows, tail)]
        for wait in (False, True):
            for cond, start in fills:
                @pl.when(cond)
                def _():
                    cp = pltpu.make_async_copy(zero_ref, _rec(hs_ref, start, MOE_TILE), sem)
                    cp.wait() if wait else cp.start()

    _issue_record_dmas(tm, lambda r: pltpu.make_async_copy(_rec(h_ref, r), _rec(hs_ref, pos_ref[0, 0, r]), sem))
    pltpu.make_async_copy(h_ref, _rec(hs_ref, 0, tm), sem).wait()


def _moe_permute(hp2, pos3, ends, padded, n_rows):
    T = hp2.shape[0] // TOKEN_REC_ROWS
    tm = min(TOKEN_TILE, T)
    return pl.pallas_call(
        _permute_kernel,
        grid_spec=pltpu.PrefetchScalarGridSpec(
            num_scalar_prefetch=2,
            grid=(T // tm,),
            in_specs=[pl.BlockSpec((1, 1, tm), lambda i, e, p: (i, 0, 0), memory_space=pltpu.SMEM),
                      pl.BlockSpec((tm * TOKEN_REC_ROWS, 128), lambda i, e, p: (i, 0))],
            out_specs=pl.BlockSpec(memory_space=pl.ANY),
            scratch_shapes=[pltpu.VMEM((MOE_TILE * TOKEN_REC_ROWS, 128), jnp.uint32), pltpu.SemaphoreType.DMA(())],
        ),
        out_shape=jax.ShapeDtypeStruct((n_rows * TOKEN_REC_ROWS, 128), jnp.uint32),
        compiler_params=_cparams(("arbitrary",)),
        name="moe_permute",
    )(ends, padded, pos3, hp2)


def _moe_kernel(e1_ref, e2_ref, h_ref, wg1_ref, wu1_ref, wd1_ref, wg2_ref, wu2_ref, wd2_ref, o_ref):
    hw = D_MODEL // 2

    def field(j):
        return h_ref[pl.ds(j, MOE_TILE, stride=TOKEN_REC_ROWS), :]

    w = jnp.concatenate([field(j) for j in range(hw // 128)], axis=1)
    h = jnp.concatenate([pltpu.bitcast(w & jnp.uint32(0xFFFF0000), F32).astype(BF16),
                         pltpu.bitcast(w << 16, F32).astype(BF16)], axis=1)
    comb = pltpu.bitcast(field(hw // 128), F32)
    lane = lax.broadcasted_iota(jnp.int32, comb.shape, 1)
    i = pl.program_id(0)
    out = None
    for e_ref, wg_ref, wu_ref, wd_ref in ((e1_ref, wg1_ref, wu1_ref, wd1_ref), (e2_ref, wg2_ref, wu2_ref, wd2_ref)):
        c = jnp.sum(jnp.where(lane == e_ref[i], comb, 0.0), axis=-1, keepdims=True)
        g = jnp.dot(h, wg_ref[...], preferred_element_type=F32)
        u = jnp.dot(h, wu_ref[...], preferred_element_type=F32)
        a = (g * jax.nn.sigmoid(g) * u * c).astype(BF16)
        part = jnp.dot(a, wd_ref[...], preferred_element_type=F32)
        out = part if out is None else out + part
    for j in range(TOKEN_REC_ROWS):
        o_ref[pl.ds(j, MOE_TILE, stride=TOKEN_REC_ROWS), :] = out[:, 128 * j:128 * (j + 1)]


def _moe_experts(hs, tile_e1, tile_e2, wg, wu, wd, l):
    n_rows = hs.shape[0] // TOKEN_REC_ROWS
    n_tiles = n_rows // MOE_TILE
    rec_tile = pl.BlockSpec((MOE_TILE * TOKEN_REC_ROWS, 128), lambda i, e1, e2: (i, 0))

    def expert(shape, which):
        if which == 0:
            return pl.BlockSpec((None, None) + shape, lambda i, e1, e2: (l, e1[i], 0, 0))
        return pl.BlockSpec((None, None) + shape, lambda i, e1, e2: (l, e2[i], 0, 0))

    up, down = (D_MODEL, MOE_FF), (MOE_FF, D_MODEL)
    return pl.pallas_call(
        _moe_kernel,
        grid_spec=pltpu.PrefetchScalarGridSpec(
            num_scalar_prefetch=2,
            grid=(n_tiles,),
            in_specs=[rec_tile, expert(up, 0), expert(up, 0), expert(down, 0),
                      expert(up, 1), expert(up, 1), expert(down, 1)],
            out_specs=rec_tile,
        ),
        out_shape=jax.ShapeDtypeStruct((n_rows * TOKEN_REC_ROWS, 128), F32),
        compiler_params=_cparams(("arbitrary",)),
        name="moe_experts",
    )(tile_e1, tile_e2, hs, wg, wu, wd, wg, wu, wd)


def _final_kernel(pos_ref, x_ref, ys_ref, fn_ref, o_ref, buf_ref, sem):
    tm = x_ref.shape[0]
    _issue_record_dmas(tm, lambda r: pltpu.make_async_copy(_rec(ys_ref, pos_ref[0, 0, r]), _rec(buf_ref, r), sem))
    pltpu.make_async_copy(_rec(ys_ref, 0, tm), buf_ref, sem).wait()
    moe = jnp.concatenate([buf_ref[pl.ds(j, tm, stride=TOKEN_REC_ROWS), :] for j in range(TOKEN_REC_ROWS)], axis=1)
    xn = x_ref[...] + moe
    ms = jnp.mean(xn * xn, axis=-1, keepdims=True)
    o_ref[...] = xn * lax.rsqrt(ms + RMS_EPS) * fn_ref[...]


def _final(x2, ys, pos3, fnorm):
    T, D = x2.shape
    tm = min(TOKEN_TILE, T)
    return pl.pallas_call(
        _final_kernel,
        grid=(T // tm,),
        in_specs=[pl.BlockSpec((1, 1, tm), lambda i: (i, 0, 0), memory_space=pltpu.SMEM),
                  pl.BlockSpec((tm, D), lambda i: (i, 0)),
                  pl.BlockSpec(memory_space=pl.ANY),
                  pl.BlockSpec((1, D), lambda i: (0, 0))],
        out_specs=pl.BlockSpec((tm, D), lambda i: (i, 0)),
        out_shape=jax.ShapeDtypeStruct((T, D), F32),
        scratch_shapes=[pltpu.VMEM((tm * TOKEN_REC_ROWS, 128), F32), pltpu.SemaphoreType.DMA(())],
        compiler_params=_cparams(("arbitrary",)),
        name="final_unpermute_norm",
    )(pos3, x2, ys, fnorm.reshape(1, D))


def _moe(hp, meta, cnt, wg, wu, wd, l):
    T = hp.shape[0] * hp.shape[1] // TOKEN_REC_ROWS
    tm = min(TOKEN_TILE, T)
    pos, ends, padded, tile_e1, tile_e2 = _moe_plan(meta, cnt)
    pos3 = pos.reshape(T // tm, 1, tm)
    n_rows = T + MOE_BINS * MOE_TILE
    hs = _moe_permute(hp.reshape(T * TOKEN_REC_ROWS, 128), pos3, ends, padded, n_rows)
    return _moe_experts(hs, tile_e1, tile_e2, wg, wu, wd, l), pos3


def kernel(x, mem, norm_mix, norm_ffn, norm_mem, w_in, b_gate, na_rpb, s5_lam_re, s5_lam_im, s5_log_dt, s5_b_re, s5_b_im, s5_c_re, s5_c_im, s5_d, s5_w_glu, w_mem_kv, w_br_na, w_br_s5, w_br_xa, w_out, moe_w_coarse, moe_b_coarse, moe_w_fine, moe_b_fine, moe_w_gate, moe_w_up, moe_w_down, final_norm):
    depth, D = w_in.shape[0], w_in.shape[1]
    mkv_all = _memkv(mem, norm_mem, w_mem_kv.astype(BF16))
    w_in_bf = w_in.astype(BF16)
    norm_mix3 = norm_mix.reshape(depth, 1, D)
    b_gate3 = b_gate.reshape(depth, 1, N_BRANCHES * D)
    na_bias = jax.vmap(_na_bias_table)(na_rpb)
    s5_tabs = jax.vmap(_s5_tables)(s5_lam_re, s5_lam_im, s5_log_dt, s5_b_re, s5_b_im, s5_c_re, s5_c_im)
    wr = jnp.concatenate([moe_w_fine, moe_w_coarse], axis=2).astype(F32)
    wr = jnp.pad(wr, ((0, 0), (0, 0), (0, ROUTER_LANES - wr.shape[2])))
    wrh = wr.astype(BF16)
    wrl = (wr - wrh.astype(F32)).astype(BF16)
    br = jnp.pad(jnp.concatenate([moe_b_fine, moe_b_coarse], axis=1).astype(F32),
                 ((0, 0), (0, ROUTER_LANES - MOE_EXPERTS - MOE_GROUPS))).reshape(depth, 1, ROUTER_LANES)
    merge_params = (s5_d.reshape(depth, 1, S5_WIDTH), s5_w_glu.astype(BF16), w_br_na.astype(BF16),
                    w_br_s5.astype(BF16), w_br_xa.astype(BF16), w_out.astype(BF16),
                    norm_ffn.reshape(depth, 1, D), jnp.concatenate([wrh, wrl], axis=-1), br)
    wg, wu, wd = moe_w_gate.astype(BF16), moe_w_up.astype(BF16), moe_w_down.astype(BF16)
    moe = None
    for l in range(depth):
        if moe is None:
            q, k, v, u, ug, qx, gates = _inproj(x, norm_mix3, w_in_bf, b_gate3, l)
        else:
            q, k, v, u, ug, qx, gates, x = _inproj(x, norm_mix3, w_in_bf, b_gate3, l, moe)
        ona = _na(q, k, v, na_bias, l)
        ys5 = _s5(ug, s5_tabs, l)
        x, hp, meta, cnt = _merge(x, ona, ys5, u, qx, gates, mkv_all, merge_params, l)
        moe = _moe(hp, meta, cnt, wg, wu, wd, l)
    B, S, _ = x.shape
    ys, pos3 = moe
    return _final(x.reshape(B * S, D), ys, pos3, final_norm).reshape(B, S, D)
```
